```python
import math
import jax, jax.numpy as jnp
from jax import lax
import numpy as np

D_MODEL = 1024
BATCH = 16
SEQ = 4096
DEPTH = 2

RMS_EPS = 1e-6
SSD_D_INNER = D_MODEL
SSD_HEAD_DIM = 64
SSD_HEADS = SSD_D_INNER // SSD_HEAD_DIM
SSD_GROUPS = 2
SSD_HEADS_PER_GROUP = SSD_HEADS // SSD_GROUPS
SSD_STATE = 128
SSD_CONV = 4
SSD_CONV_DIM = SSD_D_INNER + 2 * SSD_GROUPS * SSD_STATE
SSD_CHUNK = 128
SSD_NORM_EPS = 1e-5
RET_HEADS = 4
RET_QK_DIM = D_MODEL // 2
RET_V_DIM = D_MODEL
RET_QK_HEAD = RET_QK_DIM // RET_HEADS
RET_V_HEAD = RET_V_DIM // RET_HEADS
RET_CHUNK = 128
ROPE_BASE = 10000.0
IN_WIDTHS = (SSD_D_INNER, SSD_CONV_DIM, SSD_HEADS, RET_QK_DIM, RET_QK_DIM, RET_V_DIM, RET_V_DIM)
IN_PROJ_DIM = SSD_D_INNER + SSD_CONV_DIM + SSD_HEADS + 2 * RET_QK_DIM + 2 * RET_V_DIM
MIX_WIDTH = SSD_D_INNER + RET_V_DIM
RWKV_HEAD = 64
RWKV_HEADS = D_MODEL // RWKV_HEAD
RWKV_DECAY_LORA = 64
RWKV_A_LORA = 64
RWKV_GATE_LORA = 160
RWKV_LN_EPS = 64e-5
MOE_GROUPS = 4
MOE_EXPERTS_PER_GROUP = 4
MOE_EXPERTS = MOE_GROUPS * MOE_EXPERTS_PER_GROUP
MOE_TOP_K = 2
MOE_D_FF = 512

kernel_name = "hybrid_ssd_retention_rwkv7_hmoe"


def rms_norm(x, w, eps=RMS_EPS):
    xf = x.astype(jnp.float32)
    y = xf * lax.rsqrt(jnp.mean(xf * xf, axis=-1, keepdims=True) + eps)
    return (y * w).astype(x.dtype)


def split_cols(t, widths):
    out, start = [], 0
    for w in widths:
        out.append(t[..., start:start + w])
        start += w
    return out


def to_chunks(t, c):
    b, s = t.shape[:2]
    return jnp.moveaxis(t.reshape(b, s // c, c, *t.shape[2:]), 1, 0)


def from_chunks(t):
    nc, b, c = t.shape[:3]
    return jnp.moveaxis(t, 0, 1).reshape(b, nc * c, *t.shape[3:])


def causal_depthwise_conv(x, w, b):
    K = w.shape[0]
    S = x.shape[1]
    xp = jnp.pad(x, ((0, 0), (K - 1, 0), (0, 0)))
    out = xp[:, 0:S] * w[0]
    for j in range(1, K):
        out = out + xp[:, j:j + S] * w[j]
    return out + b


def rotary(x, pos):
    d = x.shape[-1]
    half = d // 2
    inv_freq = ROPE_BASE ** (-jnp.arange(half, dtype=jnp.float32) / half)
    ang = pos[:, None] * inv_freq[None, :]
    cos = jnp.cos(ang)[None, :, None, :]
    sin = jnp.sin(ang)[None, :, None, :]
    xf = x.astype(jnp.float32)
    x1, x2 = xf[..., :half], xf[..., half:]
    return jnp.concatenate([x1 * cos - x2 * sin, x1 * sin + x2 * cos], axis=-1).astype(x.dtype)


def ssd_mixer(z, xbc, dt_raw, conv_w, conv_b, dt_bias, a_log, d_skip, norm_w):
    Bsz, S, _ = z.shape
    G, Hg, P, N = SSD_GROUPS, SSD_HEADS_PER_GROUP, SSD_HEAD_DIM, SSD_STATE
    xbc = jax.nn.silu(causal_depthwise_conv(xbc, conv_w, conv_b))
    xs, Bm, Cm = split_cols(xbc, (SSD_D_INNER, G * N, G * N))
    xs = xs.reshape(Bsz, S, G, Hg, P)
    Bm = Bm.reshape(Bsz, S, G, N)
    Cm = Cm.reshape(Bsz, S, G, N)
    dt = jax.nn.softplus(dt_raw.astype(jnp.float32) + dt_bias).reshape(Bsz, S, G, Hg)
    A = -jnp.exp(a_log.astype(jnp.float32)).reshape(G, Hg)
    a = dt * A
    xdt = xs * dt[..., None]
    L = SSD_CHUNK
    causal = jnp.tril(jnp.ones((L, L), dtype=bool))[None, :, :, None, None]

    def step(h, inp):
        xc, ac, bc, cc = inp
        acum = jnp.cumsum(ac, axis=1)
        seg = acum[:, :, None] - acum[:, None, :]
        decay = jnp.exp(jnp.where(causal, seg, -jnp.inf))
        cb = jnp.einsum('bign,bjgn->bijg', cc, bc)
        m = cb[..., None] * decay
        y_diag = jnp.einsum('bijgh,bjghp->bighp', m, xc)
        y_off = jnp.einsum('bign,bghpn->bighp', cc, h) * jnp.exp(acum)[..., None]
        decay_end = jnp.exp(acum[:, -1:] - acum)
        h = h * jnp.exp(acum[:, -1])[..., None, None] + jnp.einsum(
            'bjgn,bjghp->bghpn', bc, xc * decay_end[..., None])
        return h, y_diag + y_off

    h0 = jnp.zeros((Bsz, G, Hg, P, N), jnp.float32)
    _, y = lax.scan(step, h0, (to_chunks(xdt, L), to_chunks(a, L), to_chunks(Bm, L), to_chunks(Cm, L)))
    y = from_chunks(y)
    y = y + xs * d_skip.reshape(G, Hg)[..., None]
    y = y.reshape(Bsz, S, SSD_D_INNER) * jax.nn.silu(z.astype(jnp.float32))
    yg = y.reshape(Bsz, S, G, SSD_D_INNER // G)
    yg = yg * lax.rsqrt(jnp.mean(yg * yg, axis=-1, keepdims=True) + SSD_NORM_EPS)
    return (yg.reshape(Bsz, S, SSD_D_INNER) * norm_w).astype(z.dtype)


def retention_mixer(q, k, v, g, pos):
    Bsz, S, _ = q.shape
    H, dk, dv, C = RET_HEADS, RET_QK_HEAD, RET_V_HEAD, RET_CHUNK
    q = rotary(q.reshape(Bsz, S, H, dk), pos)
    k = rotary(k.reshape(Bsz, S, H, dk), pos) * (dk ** -0.5)
    v = v.reshape(Bsz, S, H, dv)
    log_gamma = jnp.log(1.0 - 2.0 ** (-5.0 - jnp.arange(H, dtype=jnp.float32)))
    idx = jnp.arange(C, dtype=jnp.float32)
    diff = idx[:, None] - idx[None, :]
    dmask = jnp.where(diff[None] >= 0, jnp.exp(jnp.maximum(diff, 0.0)[None] * log_gamma[:, None, None]), 0.0)
    q_decay = jnp.exp((idx[:, None] + 1.0) * log_gamma[None, :])
    k_decay = jnp.exp((C - 1.0 - idx[:, None]) * log_gamma[None, :])
    chunk_decay = jnp.exp(C * log_gamma)

    def step(R, inp):
        qc, kc, vc = inp
        s = jnp.einsum('bihd,bjhd->bhij', qc, kc) * dmask[None]
        inner = jnp.einsum('bhij,bjhv->bihv', s, vc)
        cross = jnp.einsum('bihd,bhdv->bihv', qc, R) * q_decay[None, :, :, None]
        R = R * chunk_decay[None, :, None, None] + jnp.einsum(
            'bjhd,bjhv->bhdv', kc * k_decay[None, :, :, None], vc)
        return R, inner + cross

    R0 = jnp.zeros((Bsz, H, dk, dv), jnp.float32)
    _, y = lax.scan(step, R0, (to_chunks(q, C), to_chunks(k, C), to_chunks(v, C)))
    y = from_chunks(y)
    y = y * lax.rsqrt(jnp.mean(y * y, axis=-1, keepdims=True) + RMS_EPS)
    y = y.reshape(Bsz, S, RET_V_DIM) * jax.nn.silu(g.astype(jnp.float32))
    return y.astype(q.dtype)


def ssd_retention_layer(u, pos, w_in, conv_w, conv_b, dt_bias, a_log, d_skip, norm_w, w_out):
    proj = u @ w_in
    z, xbc, dt_raw, q, k, v, g = split_cols(proj, IN_WIDTHS)
    y_a = ssd_mixer(z, xbc, dt_raw, conv_w, conv_b, dt_bias, a_log, d_skip, norm_w)
    y_b = retention_mixer(q, k, v, g, pos)
    return jnp.concatenate([y_a, y_b], axis=-1) @ w_out


def rwkv7_layer(u, mu, w_r, w_k, w_v, w_o, w0, w1, w2, a0, a1, a2, g1, g2, k_k, k_a, r_k, lnx_w, lnx_b):
    Bsz, S, D = u.shape
    H, N = RWKV_HEADS, RWKV_HEAD
    xx = jnp.pad(u, ((0, 0), (1, 0), (0, 0)))[:, :S] - u
    xr, xw, xk, xv, xa, xg = [u + xx * mu[j] for j in range(6)]
    r = xr @ w_r
    k = xk @ w_k
    v = xv @ w_v
    w_log = -jax.nn.softplus(-(w0 + jnp.tanh(xw @ w1) @ w2)) - 0.5
    decay = jnp.exp(-jnp.exp(w_log.astype(jnp.float32)))
    a = jax.nn.sigmoid((a0 + (xa @ a1) @ a2).astype(jnp.float32))
    g = jax.nn.sigmoid(xg @ g1) @ g2
    kk = (k * k_k).reshape(Bsz, S, H, N).astype(jnp.float32)
    kk = kk * lax.rsqrt(jnp.maximum(jnp.sum(kk * kk, axis=-1, keepdims=True), 1e-24))
    k = k * (1.0 + (a - 1.0) * k_a)
    heads = lambda t: t.reshape(Bsz, S, H, N)
    r_h, k_h, v_h, w_h, a_h = heads(r), heads(k), heads(v), heads(decay), heads(a)
    aa = -kk
    bb = kk * a_h

    def step(state, inp):
        r_t, w_t, k_t, v_t, aa_t, bb_t = inp
        sa = jnp.einsum('bhij,bhj->bhi', state, aa_t)
        state = (state * w_t[:, :, None, :] + sa[..., None] * bb_t[:, :, None, :]
                 + v_t[..., None] * k_t[:, :, None, :])
        return state, jnp.einsum('bhij,bhj->bhi', state, r_t)

    seq_first = lambda t: jnp.moveaxis(t, 1, 0)
    state0 = jnp.zeros((Bsz, H, N, N), jnp.float32)
    _, y = lax.scan(step, state0, (seq_first(r_h), seq_first(w_h), seq_first(k_h),
                                   seq_first(v_h), seq_first(aa), seq_first(bb)))
    y = jnp.moveaxis(y, 0, 1)
    mean = jnp.mean(y, axis=-1, keepdims=True)
    var = jnp.mean((y - mean) ** 2, axis=-1, keepdims=True)
    y = ((y - mean) * lax.rsqrt(var + RWKV_LN_EPS)).reshape(Bsz, S, D) * lnx_w + lnx_b
    bonus = jnp.sum(r_h * k_h * r_k, axis=-1, keepdims=True) * v_h
    y = y + bonus.reshape(Bsz, S, D)
    return (y * g).astype(u.dtype) @ w_o


def hier_moe(h, wg, bg, we, be, w1, w3, w2):
    Bsz, S, D = h.shape
    t = h.reshape(-1, D)
    g_logits = (t @ wg + bg).astype(jnp.float32)
    g_prob = jax.nn.softmax(g_logits, axis=-1)
    g_idx = jnp.argmax(g_logits, axis=-1)
    g_gate = jnp.take_along_axis(g_prob, g_idx[:, None], axis=-1)[:, 0]
    e_logits = (t @ we + be).astype(jnp.float32).reshape(-1, MOE_GROUPS, MOE_EXPERTS_PER_GROUP)
    e_sel = jnp.take_along_axis(e_logits, g_idx[:, None, None], axis=1)[:, 0]
    top_v, top_i = lax.top_k(e_sel, MOE_TOP_K)
    top_w = jax.nn.softmax(top_v, axis=-1) * g_gate[:, None]
    expert_id = g_idx[:, None] * MOE_EXPERTS_PER_GROUP + top_i
    combine = jnp.sum(jax.nn.one_hot(expert_id, MOE_EXPERTS, dtype=jnp.float32) * top_w[..., None], axis=1)
    out = jnp.zeros(t.shape, jnp.float32)
    for e in range(MOE_EXPERTS):
        he = jax.nn.silu(t @ w1[e]) * (t @ w3[e])
        out = out + combine[:, e:e + 1] * (he @ w2[e])
    return out.reshape(Bsz, S, D).astype(h.dtype)


def setup_inputs(seed: int = 0) -> dict:
    key = jax.random.key(seed)
    ks = iter(jax.random.split(key, 64))
    f32 = jnp.float32
    nrm = lambda shape, scale: scale * jax.random.normal(next(ks), shape, f32)
    D = D_MODEL
    NE = (DEPTH + 1) // 2
    NO = DEPTH // 2
    x = nrm((BATCH, SEQ, D), 1.0)
    norm_mix_w = 1.0 + nrm((DEPTH, D), 0.02)
    norm_ffn_w = 1.0 + nrm((DEPTH, D), 0.02)
    norm_final_w = 1.0 + nrm((D,), 0.02)
    w_in_e = nrm((NE, D, IN_PROJ_DIM), D ** -0.5)
    ssd_conv_w = nrm((NE, SSD_CONV, SSD_CONV_DIM), SSD_CONV ** -0.5)
    ssd_conv_b = nrm((NE, SSD_CONV_DIM), 0.02)
    dt = jnp.exp(jax.random.uniform(next(ks), (NE, SSD_HEADS), f32, math.log(1e-3), math.log(1e-1)))
    ssd_dt_bias = dt + jnp.log(-jnp.expm1(-dt))
    ssd_a_log = jnp.log(jax.random.uniform(next(ks), (NE, SSD_HEADS), f32, 1.0, 16.0))
    ssd_d = 1.0 + nrm((NE, SSD_HEADS), 0.1)
    ssd_norm_w = 1.0 + nrm((NE, SSD_D_INNER), 0.02)
    w_out_e = nrm((NE, MIX_WIDTH, D), MIX_WIDTH ** -0.5)
    rw_mu = jax.random.uniform(next(ks), (NO, 6, D), f32, 0.0, 1.0)
    rw_wr = nrm((NO, D, D), D ** -0.5)
    rw_wk = nrm((NO, D, D), D ** -0.5)
    rw_wv = nrm((NO, D, D), D ** -0.5)
    rw_wo = nrm((NO, D, D), D ** -0.5)
    ramp = jnp.linspace(0.0, 1.0, D, dtype=f32) ** 0.9
    rw_w0 = -6.0 + 5.0 * ramp + nrm((NO, D), 0.1)
    rw_w1 = nrm((NO, D, RWKV_DECAY_LORA), D ** -0.5)
    rw_w2 = nrm((NO, RWKV_DECAY_LORA, D), 0.1 * RWKV_DECAY_LORA ** -0.5)
    rw_a0 = nrm((NO, D), 0.1)
    rw_a1 = nrm((NO, D, RWKV_A_LORA), D ** -0.5)
    rw_a2 = nrm((NO, RWKV_A_LORA, D), 0.5 * RWKV_A_LORA ** -0.5)
    rw_g1 = nrm((NO, D, RWKV_GATE_LORA), D ** -0.5)
    rw_g2 = nrm((NO, RWKV_GATE_LORA, D), RWKV_GATE_LORA ** -0.5)
    rw_kk = 0.85 + nrm((NO, D), 0.02)
    rw_ka = 1.0 + nrm((NO, D), 0.02)
    rw_rk = nrm((NO, RWKV_HEADS, RWKV_HEAD), 0.1)
    rw_lnx_w = 1.0 + nrm((NO, D), 0.02)
    rw_lnx_b = nrm((NO, D), 0.02)
    moe_wg = nrm((DEPTH, D, MOE_GROUPS), D ** -0.5)
    moe_bg = nrm((DEPTH, MOE_GROUPS), 0.01)
    moe_we = nrm((DEPTH, D, MOE_EXPERTS), D ** -0.5)
    moe_be = nrm((DEPTH, MOE_EXPERTS), 0.01)
    moe_w1 = nrm((DEPTH, MOE_EXPERTS, D, MOE_D_FF), D ** -0.5)
    moe_w3 = nrm((DEPTH, MOE_EXPERTS, D, MOE_D_FF), D ** -0.5)
    moe_w2 = nrm((DEPTH, MOE_EXPERTS, MOE_D_FF, D), MOE_D_FF ** -0.5)
    return {"x": x, "norm_mix_w": norm_mix_w, "norm_ffn_w": norm_ffn_w, "norm_final_w": norm_final_w,
            "w_in_e": w_in_e, "ssd_conv_w": ssd_conv_w, "ssd_conv_b": ssd_conv_b,
            "ssd_dt_bias": ssd_dt_bias, "ssd_a_log": ssd_a_log, "ssd_d": ssd_d,
            "ssd_norm_w": ssd_norm_w, "w_out_e": w_out_e,
            "rw_mu": rw_mu, "rw_wr": rw_wr, "rw_wk": rw_wk, "rw_wv": rw_wv, "rw_wo": rw_wo,
            "rw_w0": rw_w0, "rw_w1": rw_w1, "rw_w2": rw_w2, "rw_a0": rw_a0, "rw_a1": rw_a1,
            "rw_a2": rw_a2, "rw_g1": rw_g1, "rw_g2": rw_g2, "rw_kk": rw_kk, "rw_ka": rw_ka,
            "rw_rk": rw_rk, "rw_lnx_w": rw_lnx_w, "rw_lnx_b": rw_lnx_b,
            "moe_wg": moe_wg, "moe_bg": moe_bg, "moe_we": moe_we, "moe_be": moe_be,
            "moe_w1": moe_w1, "moe_w3": moe_w3, "moe_w2": moe_w2}


def reference(x, norm_mix_w, norm_ffn_w, norm_final_w,
              w_in_e, ssd_conv_w, ssd_conv_b, ssd_dt_bias, ssd_a_log, ssd_d, ssd_norm_w, w_out_e,
              rw_mu, rw_wr, rw_wk, rw_wv, rw_wo, rw_w0, rw_w1, rw_w2, rw_a0, rw_a1, rw_a2,
              rw_g1, rw_g2, rw_kk, rw_ka, rw_rk, rw_lnx_w, rw_lnx_b,
              moe_wg, moe_bg, moe_we, moe_be, moe_w1, moe_w3, moe_w2):
    pos = jnp.arange(x.shape[1], dtype=jnp.float32)
    h = x
    for layer in range(DEPTH):
        i = layer // 2
        u = rms_norm(h, norm_mix_w[layer])
        if layer % 2 == 0:
            mix = ssd_retention_layer(u, pos, w_in_e[i], ssd_conv_w[i], ssd_conv_b[i], ssd_dt_bias[i],
                                      ssd_a_log[i], ssd_d[i], ssd_norm_w[i], w_out_e[i])
        else:
            mix = rwkv7_layer(u, rw_mu[i], rw_wr[i], rw_wk[i], rw_wv[i], rw_wo[i], rw_w0[i], rw_w1[i],
                              rw_w2[i], rw_a0[i], rw_a1[i], rw_a2[i], rw_g1[i], rw_g2[i], rw_kk[i],
                              rw_ka[i], rw_rk[i], rw_lnx_w[i], rw_lnx_b[i])
        h = h + mix
        h = h + hier_moe(rms_norm(h, norm_ffn_w[layer]), moe_wg[layer], moe_bg[layer], moe_we[layer],
                         moe_be[layer], moe_w1[layer], moe_w3[layer], moe_w2[layer])
    return rms_norm(h, norm_final_w)
```

```python
import functools
import math

import jax
import jax.numpy as jnp
from jax import lax
from jax.experimental import pallas as pl
from jax.experimental.pallas import tpu as pltpu

F32 = jnp.float32
BF16 = jnp.bfloat16

D_MODEL = 1024
RMS_EPS = 1e-6
SSD_HEADS = 16
SSD_HEAD_DIM = 64
SSD_GROUPS = 2
SSD_STATE = 128
SSD_CONV = 4
SSD_CHUNK = 128
SSD_NORM_EPS = 1e-5
SSD_GROUP_WIDTH = D_MODEL // SSD_GROUPS
RET_HEADS = 4
RET_QK_HEAD = 128
RET_V_HEAD = 256
RET_CHUNK = 128
ROPE_BASE = 10000.0
RWKV_HEAD = 64
RWKV_HEADS = 16
RWKV_LN_EPS = 64e-5
RWKV_CHUNK = 64
RWKV_LORA_PAD = 128
RWKV_GATE_PAD = 256
MOE_GROUPS = 4
MOE_EPG = 4
MOE_EXPERTS = 16
MOE_D_FF = 512
ROUTER_LANES = 128

LANES = 128
VMEM_LIMIT_BYTES = 56 * 1024 * 1024


def _cparams(sem):
    return pltpu.CompilerParams(dimension_semantics=sem, vmem_limit_bytes=VMEM_LIMIT_BYTES)


def _dot(a, b):
    return jnp.dot(a, b, preferred_element_type=F32)


def _dot_nt(a, b):
    return lax.dot_general(a, b, (((1,), (1,)), ((), ())), preferred_element_type=F32)


def _split2(x):
    hi = x.astype(BF16)
    lo = (x - hi.astype(F32)).astype(BF16)
    return hi, lo


def _dot_x2(x, w_exact):
    hi, lo = _split2(x)
    return _dot(hi, w_exact) + _dot(lo, w_exact)


def _dot_2x(w_exact, x):
    hi, lo = _split2(x)
    return _dot(w_exact, hi) + _dot(w_exact, lo)


def _dot_x3(x, w):
    xh, xl = _split2(x)
    wh, wl = _split2(w)
    return _dot(xh, wh) + _dot(xl, wh) + _dot(xh, wl)


def _sigmoid(x):
    return 1.0 / (1.0 + jnp.exp(-x))


def _silu(x):
    return x * _sigmoid(x)


def _softplus(x):
    return jnp.maximum(x, 0.0) + jnp.log(1.0 + jnp.exp(-jnp.abs(x)))


def _rms(x, w, eps=RMS_EPS):
    return x * lax.rsqrt(jnp.mean(x * x, axis=-1, keepdims=True) + eps) * w


def _inproj_kernel(x_ref, nw_ref, w_ref, wdt_ref, o_ref, dt_ref, u_ref):
    @pl.when(pl.program_id(1) == 0)
    def _():
        ub = _rms(x_ref[...], nw_ref[...]).astype(BF16)
        u_ref[...] = ub
        dt_ref[...] = _dot(ub, wdt_ref[...])

    o_ref[...] = _dot(u_ref[...], w_ref[...]).astype(o_ref.dtype)


def _inproj(h2, nw, w_main, w_dt, tm, tn):
    T, D = h2.shape
    N = w_main.shape[1]
    return pl.pallas_call(
        _inproj_kernel,
        grid=(T // tm, N // tn),
        in_specs=[
            pl.BlockSpec((tm, D), lambda i, j: (i, 0)),
            pl.BlockSpec((1, D), lambda i, j: (0, 0)),
            pl.BlockSpec((D, tn), lambda i, j: (0, j)),
            pl.BlockSpec((D, LANES), lambda i, j: (0, 0)),
        ],
        out_specs=[
            pl.BlockSpec((tm, tn), lambda i, j: (i, j)),
            pl.BlockSpec((tm, LANES), lambda i, j: (i, 0)),
        ],
        out_shape=[
            jax.ShapeDtypeStruct((T, N), BF16),
            jax.ShapeDtypeStruct((T, LANES), F32),
        ],
        scratch_shapes=[pltpu.VMEM((tm, D), BF16)],
        compiler_params=_cparams(("parallel", "arbitrary")),
    )(h2, nw, w_main, w_dt)


def _ssd_kernel(z_ref, x_ref, bc_ref, dt_ref, cwx_ref, cbx_ref, cwb_ref, cbb_ref,
                dtb_ref, alog_ref, e_ref, dskip_ref, nw_ref, o_ref,
                state_ref, xbuf, bcbuf, ybuf):
    L = SSD_CHUNK
    c = pl.program_id(1)

    @pl.when(c == 0)
    def _():
        state_ref[...] = jnp.zeros_like(state_ref)
        xbuf[0:8, :] = jnp.zeros((8, xbuf.shape[1]), F32)
        bcbuf[0:8, :] = jnp.zeros((8, bcbuf.shape[1]), F32)

    def conv(cur_ref, buf, w_ref, b_ref):
        cur = cur_ref[0].astype(F32)
        buf[8:8 + L, :] = cur
        acc = b_ref[...] + w_ref[0:1, :] * buf[5:5 + L, :]
        for k in range(1, SSD_CONV):
            acc = acc + w_ref[k:k + 1, :] * buf[5 + k:5 + k + L, :]
        buf[0:8, :] = cur[L - 8:L, :]
        return _silu(acc)

    xs = conv(x_ref, xbuf, cwx_ref, cbx_ref)
    bc = conv(bc_ref, bcbuf, cwb_ref, cbb_ref)

    dt = _softplus(dt_ref[0] + dtb_ref[...])
    a = dt * (-jnp.exp(alog_ref[...]))
    row = lax.broadcasted_iota(jnp.int32, (L, L), 0)
    col = lax.broadcasted_iota(jnp.int32, (L, L), 1)
    causal = row >= col
    tri = jnp.where(causal, 1.0, 0.0).astype(BF16)
    acum = _dot_2x(tri, a)
    acum_t = acum.T
    e = e_ref[...]
    dt_full = _dot_x2(dt, e)
    acum_full = _dot_x2(acum, e)
    xdt = xs * dt_full
    alast_full = acum_full[L - 1:L, :]
    decay_in = jnp.exp(acum_full)
    xdt_end = (xdt * jnp.exp(alast_full - acum_full)).astype(BF16)
    xdt_b = xdt.astype(BF16)
    lane = lax.broadcasted_iota(jnp.int32, (L, LANES), 1)
    first_head = lane < SSD_HEAD_DIM

    GW = SSD_GROUP_WIDTH
    for g in range(SSD_GROUPS):
        bg = bc[:, g * SSD_STATE:(g + 1) * SSD_STATE]
        cg = bc[:, (SSD_GROUPS + g) * SSD_STATE:(SSD_GROUPS + g + 1) * SSD_STATE].astype(BF16)
        cb = _dot_nt(cg, bg.astype(BF16))
        st = state_ref[g]
        y_off = _dot(cg, st.astype(BF16)) * decay_in[:, g * GW:(g + 1) * GW]
        for p in range(GW // LANES):
            xp = xdt_b[:, g * GW + p * LANES:g * GW + (p + 1) * LANES]
            ys = []
            for s in range(2):
                hd = g * (SSD_HEADS // SSD_GROUPS) + 2 * p + s
                seg = acum[:, hd:hd + 1] - acum_t[hd:hd + 1, :]
                dec = jnp.where(causal, jnp.exp(jnp.minimum(seg, 0.0)), 0.0)
                ys.append(_dot((cb * dec).astype(BF16), xp))
            yd = jnp.where(first_head, ys[0], ys[1])
            lo = g * GW + p * LANES
            ybuf[:, lo:lo + LANES] = yd + y_off[:, p * LANES:(p + 1) * LANES]
        bg_t = bg.T.astype(BF16)
        state_ref[g] = st * jnp.exp(alast_full[:, g * GW:(g + 1) * GW]) + _dot(
            bg_t, xdt_end[:, g * GW:(g + 1) * GW])

    y = ybuf[...] + xs * dskip_ref[...]
    y = y * _silu(z_ref[0].astype(F32))
    for g in range(SSD_GROUPS):
        yg = y[:, g * GW:(g + 1) * GW]
        yg = yg * lax.rsqrt(jnp.mean(yg * yg, axis=-1, keepdims=True) + SSD_NORM_EPS)
        o_ref[0, :, g * GW:(g + 1) * GW] = (yg * nw_ref[:, g * GW:(g + 1) * GW]).astype(o_ref.dtype)


def _ssd(proj3, dt3, cwx, cbx, cwb, cbb, dtb, alog, expand, dskip, nw):
    B, S, _ = proj3.shape
    L = SSD_CHUNK
    D = D_MODEL
    full = lambda shape: pl.BlockSpec(shape, lambda b, c: (0,) * len(shape))
    return pl.pallas_call(
        _ssd_kernel,
        grid=(B, S // L),
        in_specs=[
            pl.BlockSpec((1, L, D), lambda b, c: (b, c, 0)),
            pl.BlockSpec((1, L, D), lambda b, c: (b, c, 1)),
            pl.BlockSpec((1, L, 512), lambda b, c: (b, c, 8)),
            pl.BlockSpec((1, L, LANES), lambda b, c: (b, c, 0)),
            full((SSD_CONV, D)), full((1, D)), full((SSD_CONV, 512)), full((1, 512)),
            full((1, LANES)), full((1, LANES)), full((LANES, D)), full((1, D)), full((1, D)),
        ],
        out_specs=pl.BlockSpec((1, L, D), lambda b, c: (b, c, 0)),
        out_shape=jax.ShapeDtypeStruct((B, S, D), BF16),
        scratch_shapes=[
            pltpu.VMEM((SSD_GROUPS, SSD_STATE, SSD_GROUP_WIDTH), F32),
            pltpu.VMEM((L + 8, D), F32),
            pltpu.VMEM((L + 8, 512), F32),
            pltpu.VMEM((L, D), F32),
        ],
        compiler_params=_cparams(("parallel", "arbitrary")),
    )(proj3, proj3, proj3, dt3, cwx, cbx, cwb, cbb, dtb, alog, expand, dskip, nw)


def _ret_kernel(q_ref, k_ref, v_ref, g_ref, cc_ref, ss_ref, dmask_ref, qdec_ref, kdec_ref,
                cdec_ref, o_ref, r_ref):
    c = pl.program_id(1)

    @pl.when(c == 0)
    def _():
        r_ref[...] = jnp.zeros_like(r_ref)

    cc = cc_ref[...]
    ss = ss_ref[...]
    dk, dv = RET_QK_HEAD, RET_V_HEAD
    half = dk // 2

    def rope(x):
        return x * cc + pltpu.roll(x, half, 1) * ss

    for hd in range(RET_HEADS):
        q = rope(q_ref[0, :, hd * dk:(hd + 1) * dk].astype(F32))
        k = rope(k_ref[0, :, hd * dk:(hd + 1) * dk].astype(F32)) * (dk ** -0.5)
        v = v_ref[0, :, hd * dv:(hd + 1) * dv]
        qb = q.astype(BF16)
        s = _dot_nt(qb, k.astype(BF16)) * dmask_ref[hd]
        r_old = r_ref[hd]
        y = _dot(s.astype(BF16), v) + _dot(qb, r_old.astype(BF16)) * qdec_ref[:, hd * dv:(hd + 1) * dv]
        kd_t = (k * kdec_ref[:, hd * dk:(hd + 1) * dk]).T.astype(BF16)
        r_ref[hd] = r_old * cdec_ref[:, hd * dv:(hd + 1) * dv] + _dot(kd_t, v)
        y = y * lax.rsqrt(jnp.mean(y * y, axis=-1, keepdims=True) + RMS_EPS)
        y = y * _silu(g_ref[0, :, hd * dv:(hd + 1) * dv].astype(F32))
        o_ref[0, :, hd * dv:(hd + 1) * dv] = y.astype(o_ref.dtype)


def _retention(proj3, cc, ss, dmask, qdec, kdec, cdec):
    B, S, _ = proj3.shape
    C = RET_CHUNK
    D = D_MODEL
    full = lambda shape: pl.BlockSpec(shape, lambda b, c: (0,) * len(shape))
    return pl.pallas_call(
        _ret_kernel,
        grid=(B, S // C),
        in_specs=[
            pl.BlockSpec((1, C, 512), lambda b, c: (b, c, 9)),
            pl.BlockSpec((1, C, 512), lambda b, c: (b, c, 10)),
            pl.BlockSpec((1, C, D), lambda b, c: (b, c, 2)),
            pl.BlockSpec((1, C, D), lambda b, c: (b, c, 3)),
            pl.BlockSpec((C, RET_QK_HEAD), lambda b, c: (c, 0)),
            pl.BlockSpec((C, RET_QK_HEAD), lambda b, c: (c, 0)),
            full((RET_HEADS, C, C)), full((C, D)), full((C, 512)), full((1, D)),
        ],
        out_specs=pl.BlockSpec((1, C, D), lambda b, c: (b, c, 0)),
        out_shape=jax.ShapeDtypeStruct((B, S, D), BF16),
        scratch_shapes=[pltpu.VMEM((RET_HEADS, RET_QK_HEAD, RET_V_HEAD), F32)],
        compiler_params=_cparams(("parallel", "arbitrary")),
    )(proj3, proj3, proj3, proj3, cc, ss, dmask, qdec, kdec, cdec)


def _outproj_kernel(ya_ref, yb_ref, wa_ref, wb_ref, h_ref, o_ref):
    o_ref[...] = h_ref[...] + _dot(ya_ref[...], wa_ref[...]) + _dot(yb_ref[...], wb_ref[...])


def _outproj(ya, yb, wa, wb, h2, tm):
    T, D = h2.shape
    return pl.pallas_call(
        _outproj_kernel,
        grid=(T // tm,),
        in_specs=[
            pl.BlockSpec((tm, D), lambda i: (i, 0)),
            pl.BlockSpec((tm, D), lambda i: (i, 0)),
            pl.BlockSpec((D, D), lambda i: (0, 0)),
            pl.BlockSpec((D, D), lambda i: (0, 0)),
            pl.BlockSpec((tm, D), lambda i: (i, 0)),
        ],
        out_specs=pl.BlockSpec((tm, D), lambda i: (i, 0)),
        out_shape=jax.ShapeDtypeStruct((T, D), F32),
        compiler_params=_cparams(("parallel",)),
    )(ya, yb, wa, wb, h2)


def _rwkv_pre_kernel(h_ref, hp_ref, nw_ref, mu_ref, wr_ref, wk_ref, wv_ref,
                     w1_ref, w2_ref, w0_ref, a1_ref, a2_ref, a0_ref, g1_ref, g2_ref,
                     kkw_ref, kaw_ref, hs_ref, hst_ref,
                     r_o, k_o, v_o, kk_o, a_o, lw_o, g_o, ubuf, *, tiles_per_seq):
    tm = h_ref.shape[0]
    i = pl.program_id(0)
    nw = nw_ref[...]
    u = _rms(h_ref[...], nw)
    up = _rms(hp_ref[...], nw)
    seq_start = (i % tiles_per_seq) == 0
    ubuf[8:8 + tm, :] = u
    ubuf[0:8, :] = jnp.where(seq_start, 0.0, up)
    xx = ubuf[7:7 + tm, :] - u

    def mix(j):
        return (u + xx * mu_ref[j:j + 1, :]).astype(BF16)

    r = _dot(mix(0), wr_ref[...])
    k = _dot(mix(2), wk_ref[...])
    v = _dot(mix(3), wv_ref[...])
    wl = w0_ref[...] + _dot(jnp.tanh(_dot(mix(1), w1_ref[...])).astype(BF16), w2_ref[...])
    w_log = -_softplus(-wl) - 0.5
    lw = -jnp.exp(w_log)
    a = _sigmoid(a0_ref[...] + _dot(_dot(mix(4), a1_ref[...]).astype(BF16), a2_ref[...]))
    g = _dot(_sigmoid(_dot(mix(5), g1_ref[...])).astype(BF16), g2_ref[...])
    kk = k * kkw_ref[...]
    ssq = _dot((kk * kk).astype(BF16), hs_ref[...])
    inv = lax.rsqrt(jnp.maximum(ssq, 1e-24))
    kk = kk * _dot_x2(inv, hst_ref[...])
    k = k * (1.0 + (a - 1.0) * kaw_ref[...])
    r_o[...] = r.astype(r_o.dtype)
    k_o[...] = k.astype(k_o.dtype)
    v_o[...] = v.astype(v_o.dtype)
    kk_o[...] = kk.astype(kk_o.dtype)
    a_o[...] = a.astype(a_o.dtype)
    lw_o[...] = lw
    g_o[...] = g.astype(g_o.dtype)


def _rwkv_pre(h2, nw, mu, wr, wk, wv, w1, w2, w0, a1, a2, a0, g1, g2, kkw, kaw, hs, hst, S, tm):
    T, D = h2.shape
    full = lambda arr: pl.BlockSpec(arr.shape, lambda i: (0,) * arr.ndim)
    row = pl.BlockSpec((tm, D), lambda i: (i, 0))
    params = (nw, mu, wr, wk, wv, w1, w2, w0, a1, a2, a0, g1, g2, kkw, kaw, hs, hst)
    bf = jax.ShapeDtypeStruct((T, D), BF16)
    return pl.pallas_call(
        functools.partial(_rwkv_pre_kernel, tiles_per_seq=S // tm),
        grid=(T // tm,),
        in_specs=[row, pl.BlockSpec((8, D), lambda i: (jnp.maximum(i * (tm // 8) - 1, 0), 0))]
        + [full(p) for p in params],
        out_specs=[row] * 7,
        out_shape=[bf, bf, bf, bf, bf, jax.ShapeDtypeStruct((T, D), F32), bf],
        scratch_shapes=[pltpu.VMEM((tm + 8, D), F32)],
        compiler_params=_cparams(("parallel",)),
    )(h2, h2, *params)


def _rwkv_scan_kernel(r_ref, k_ref, v_ref, kk_ref, a_ref, lw_ref, g_ref, lnw_ref, lnb_ref, rk_ref,
                      o_ref, state_ref, *, n_chunks, n_pairs):
    C = RWKV_CHUNK
    N = RWKV_HEAD
    t = pl.program_id(2)

    @pl.when(t == 0)
    def _():
        state_ref[...] = jnp.zeros_like(state_ref)

    row = lax.broadcasted_iota(jnp.int32, (C, C), 0)
    col = lax.broadcasted_iota(jnp.int32, (C, C), 1)
    tri_incl = jnp.where(row >= col, 1.0, 0.0).astype(BF16)
    eye = jnp.where(row == col, 1.0, 0.0).astype(F32)
    grow = lax.broadcasted_iota(jnp.int32, (2 * C, 2 * C), 0)
    gcol = lax.broadcasted_iota(jnp.int32, (2 * C, 2 * C), 1)
    gr, gc = grow % C, gcol % C
    gmask = (gr + grow // C) > gc
    blockdiag = (grow // N) == (gcol // N)
    lane = lax.broadcasted_iota(jnp.int32, (C, LANES), 1)
    head_masks = (lane < N, lane >= N)
    upper_lanes = lane >= C

    def chunk_body(ci, carry):
        rows = pl.ds(pl.multiple_of(ci * C, C), C)
        for p in range(n_pairs):
            lanes = slice(p * LANES, (p + 1) * LANES)
            r = r_ref[0, rows, lanes].astype(F32)
            k = k_ref[0, rows, lanes].astype(F32)
            v = v_ref[0, rows, lanes].astype(F32)
            kk = kk_ref[0, rows, lanes].astype(F32)
            a = a_ref[0, rows, lanes].astype(F32)
            lw = lw_ref[0, rows, lanes]
            cum = _dot_2x(tri_incl, lw)
            wc = cum[C - 1:C, :]
            e_in = jnp.exp(cum)
            e_out = jnp.exp(-cum)
            e_end = jnp.exp(wc - cum)
            b = kk * a
            at = -kk * jnp.exp(cum - lw)
            rt = r * e_in
            bh = (b * e_out).astype(BF16)
            kh = (k * e_out).astype(BF16)
            kb_end = jnp.concatenate([k * e_end, b * e_end], axis=0).astype(BF16)
            rhs = jnp.concatenate([bh, kh], axis=0)
            vb = v.astype(BF16)
            vv = jnp.concatenate([vb, vb], axis=0)

            u_loc, at_p, g_bot = [], [], []
            for s in range(2):
                m = head_masks[s]
                lhs = jnp.concatenate([jnp.where(m, at, 0.0), jnp.where(m, rt, 0.0)], axis=0).astype(BF16)
                gram = jnp.where(gmask, _dot_nt(lhs, rhs), 0.0)
                top = gram[0:C, :]
                g_bot.append(gram[C:2 * C, :].astype(BF16))
                lmat = top[:, 0:C]
                pk = eye + lmat
                lk = _dot(lmat.astype(BF16), lmat.astype(BF16))
                n_levels = int(math.log2(C))
                for lvl in range(1, n_levels):
                    lkb = lk.astype(BF16)
                    if lvl < n_levels - 1:
                        both = _dot(jnp.concatenate([pk, lk], axis=0).astype(BF16), lkb)
                        pk = pk + both[0:C, :]
                        lk = both[C:2 * C, :]
                    else:
                        pk = pk + _dot(pk.astype(BF16), lkb)
                mv = _dot(jnp.where(upper_lanes, top, 0.0).astype(BF16), vv)
                tw = _dot(pk.astype(BF16), jnp.concatenate([mv, at], axis=1).astype(BF16))
                u_loc.append(tw[:, 0:LANES])
                at_p.append(tw[:, LANES:2 * LANES])
            m0 = head_masks[0]
            u_local = jnp.where(m0, u_loc[0], u_loc[1])
            atp = jnp.where(m0, at_p[0], at_p[1])

            st = state_ref[p]
            z = _dot_nt(jnp.concatenate([atp, rt], axis=0).astype(BF16), st.astype(BF16))
            u = u_local + z[0:C, :]
            y0 = z[C:2 * C, :]
            uv = jnp.concatenate([u.astype(BF16), vb], axis=0)
            ys = [_dot(g_bot[s], uv) for s in range(2)]
            y = y0 + jnp.where(m0, ys[0], ys[1])

            vu_t = jnp.concatenate([v, u], axis=0).T.astype(BF16)
            upd = _dot(vu_t, kb_end)
            state_ref[p] = st * jnp.exp(wc) + jnp.where(blockdiag, upd, 0.0)

            def head_sum(x):
                s0 = jnp.sum(jnp.where(m0, x, 0.0), axis=-1, keepdims=True)
                s1 = jnp.sum(jnp.where(m0, 0.0, x), axis=-1, keepdims=True)
                return jnp.where(m0, s0, s1)

            mean = head_sum(y) * (1.0 / N)
            yc = y - mean
            var = head_sum(yc * yc) * (1.0 / N)
            yn = yc * lax.rsqrt(var + RWKV_LN_EPS) * lnw_ref[:, lanes] + lnb_ref[:, lanes]
            bonus = head_sum(r * k * rk_ref[:, lanes]) * v
            out = (yn + bonus) * g_ref[0, rows, lanes].astype(F32)
            o_ref[0, rows, lanes] = out.astype(o_ref.dtype)
        return carry

    lax.fori_loop(0, n_chunks, chunk_body, 0)


def _rwkv_scan(r, k, v, kk, a, lw, g, lnw, lnb, rk, tb, n_pairs):
    B, S, D = r.shape
    W = n_pairs * LANES
    blk = pl.BlockSpec((1, tb, W), lambda b, j, t: (b, t, j))
    par = pl.BlockSpec((1, W), lambda b, j, t: (0, j))
    return pl.pallas_call(
        functools.partial(_rwkv_scan_kernel, n_chunks=tb // RWKV_CHUNK, n_pairs=n_pairs),
        grid=(B, D // W, S // tb),
        in_specs=[blk] * 7 + [par] * 3,
        out_specs=blk,
        out_shape=jax.ShapeDtypeStruct((B, S, D), BF16),
        scratch_shapes=[pltpu.VMEM((n_pairs, LANES, LANES), F32)],
        compiler_params=_cparams(("parallel", "parallel", "arbitrary")),
    )(r, k, v, kk, a, lw, g, lnw, lnb, rk)


def _rwkv_out_kernel(y_ref, w_ref, h_ref, o_ref):
    o_ref[...] = h_ref[...] + _dot(y_ref[...], w_ref[...])


def _rwkv_out(y, wo, h2, tm):
    T, D = h2.shape
    return pl.pallas_call(
        _rwkv_out_kernel,
        grid=(T // tm,),
        in_specs=[pl.BlockSpec((tm, D), lambda i: (i, 0)), pl.BlockSpec((D, D), lambda i: (0, 0)),
                  pl.BlockSpec((tm, D), lambda i: (i, 0))],
        out_specs=pl.BlockSpec((tm, D), lambda i: (i, 0)),
        out_shape=jax.ShapeDtypeStruct((T, D), F32),
        compiler_params=_cparams(("parallel",)),
    )(y, wo, h2)


def _router_kernel(h_ref, nw_ref, wr_ref, br_ref, t_ref, cw_ref):
    tn = _rms(h_ref[...], nw_ref[...])
    t_ref[...] = tn.astype(t_ref.dtype)
    logits = _dot_x3(tn, wr_ref[...]) + br_ref[...]
    lane = lax.broadcasted_iota(jnp.int32, logits.shape, 1).astype(F32)
    neg = jnp.float32(-jnp.inf)
    big = jnp.float32(1 << 20)
    is_g = lane < MOE_GROUPS
    gl = jnp.where(is_g, logits, neg)
    gmax = jnp.max(gl, axis=-1, keepdims=True)
    g_idx = jnp.min(jnp.where(gl == gmax, lane, big), axis=-1, keepdims=True)
    g_gate = 1.0 / jnp.sum(jnp.where(is_g, jnp.exp(logits - gmax), 0.0), axis=-1, keepdims=True)
    lo = MOE_GROUPS + g_idx * MOE_EPG
    sel = (lane >= lo) & (lane < lo + MOE_EPG)
    el = jnp.where(sel, logits, neg)
    m1 = jnp.max(el, axis=-1, keepdims=True)
    i1 = jnp.min(jnp.where(el == m1, lane, big), axis=-1, keepdims=True)
    el2 = jnp.where(lane == i1, neg, el)
    m2 = jnp.max(el2, axis=-1, keepdims=True)
    i2 = jnp.min(jnp.where(el2 == m2, lane, big), axis=-1, keepdims=True)
    e21 = jnp.exp(m2 - m1)
    w1 = g_gate / (1.0 + e21)
    w2 = w1 * e21
    cw_ref[...] = jnp.where(lane == i1, w1, jnp.where(lane == i2, w2, 0.0))


def _router(h2, nw, wr, br, tm):
    T, D = h2.shape
    return pl.pallas_call(
        _router_kernel,
        grid=(T // tm,),
        in_specs=[pl.BlockSpec((tm, D), lambda i: (i, 0)), pl.BlockSpec((1, D), lambda i: (0, 0)),
                  pl.BlockSpec((D, ROUTER_LANES), lambda i: (0, 0)),
                  pl.BlockSpec((1, ROUTER_LANES), lambda i: (0, 0))],
        out_specs=[pl.BlockSpec((tm, D), lambda i: (i, 0)),
                   pl.BlockSpec((tm, ROUTER_LANES), lambda i: (i, 0))],
        out_shape=[jax.ShapeDtypeStruct((T, D), BF16),
                   jax.ShapeDtypeStruct((T, ROUTER_LANES), F32)],
        compiler_params=_cparams(("parallel",)),
    )(h2, nw, wr, br)


def _experts_kernel(t_ref, cw_ref, h_ref, w1_ref, w3_ref, w2_ref, fnw_ref, o_ref, *, final_norm):
    e = pl.program_id(1)

    @pl.when(e == 0)
    def _():
        o_ref[...] = h_ref[...]

    t = t_ref[...]
    a = _dot(t, w1_ref[0])
    b = _dot(t, w3_ref[0])
    hh = (_silu(a) * b).astype(BF16)
    cw = cw_ref[...]
    lane = lax.broadcasted_iota(jnp.int32, cw.shape, 1)
    cwe = jnp.sum(jnp.where(lane == e + MOE_GROUPS, cw, 0.0), axis=-1, keepdims=True)
    o_ref[...] += cwe * _dot(hh, w2_ref[0])

    if final_norm:
        @pl.when(e == MOE_EXPERTS - 1)
        def _():
            o_ref[...] = _rms(o_ref[...], fnw_ref[...])


def _experts(t, cw, h2, w1, w3, w2, fnw, tm, final_norm):
    T, D = h2.shape
    F = MOE_D_FF
    return pl.pallas_call(
        functools.partial(_experts_kernel, final_norm=final_norm),
        grid=(T // tm, MOE_EXPERTS),
        in_specs=[
            pl.BlockSpec((tm, D), lambda i, e: (i, 0)),
            pl.BlockSpec((tm, ROUTER_LANES), lambda i, e: (i, 0)),
            pl.BlockSpec((tm, D), lambda i, e: (i, 0)),
            pl.BlockSpec((1, D, F), lambda i, e: (e, 0, 0)),
            pl.BlockSpec((1, D, F), lambda i, e: (e, 0, 0)),
            pl.BlockSpec((1, F, D), lambda i, e: (e, 0, 0)),
            pl.BlockSpec((1, D), lambda i, e: (0, 0)),
        ],
        out_specs=pl.BlockSpec((tm, D), lambda i, e: (i, 0)),
        out_shape=jax.ShapeDtypeStruct((T, D), F32),
        compiler_params=_cparams(("parallel", "arbitrary")),
    )(t, cw, h2, w1, w3, w2, fnw)


def _moe(h2, nw, wg, bg, we, be, w1, w3, w2, fnw, final_norm, tm):
    D = D_MODEL
    pad = ROUTER_LANES - MOE_GROUPS - MOE_EXPERTS
    wr = jnp.concatenate([wg, we, jnp.zeros((D, pad), F32)], axis=1)
    br = jnp.concatenate([bg, be, jnp.zeros((pad,), F32)]).reshape(1, ROUTER_LANES)
    t, cw = _router(h2, nw.reshape(1, D), wr, br, tm)
    return _experts(t, cw, h2, w1.astype(BF16), w3.astype(BF16), w2.astype(BF16),
                    fnw.reshape(1, D), tm, final_norm)


def _pick(n, prefs):
    for p in prefs:
        if n % p == 0:
            return p
    return n


def _ssd_retention_layer(h2, B, S, nw, w_in, conv_w, conv_b, dt_bias, a_log, d_skip, norm_w, w_out):
    D = D_MODEL
    T = B * S
    o_x, o_bc, o_dt, o_q, o_k, o_v, o_g = 1024, 2048, 2560, 2576, 3088, 3600, 4624
    w_main = jnp.concatenate([w_in[:, 0:o_x], w_in[:, o_x:o_bc], w_in[:, o_v:o_g], w_in[:, o_g:o_g + D],
                              w_in[:, o_bc:o_dt], w_in[:, o_q:o_k], w_in[:, o_k:o_v]], axis=1).astype(BF16)
    w_dt = jnp.pad(w_in[:, o_dt:o_q], ((0, 0), (0, LANES - SSD_HEADS))).astype(BF16)
    tm = _pick(T, (1024, 512, 256, 128))
    proj, dt = _inproj(h2, nw.reshape(1, D), w_main, w_dt, tm, 1408)
    proj3 = proj.reshape(B, S, proj.shape[1])
    dt3 = dt.reshape(B, S, LANES)

    head_of_channel = jnp.arange(D) // SSD_HEAD_DIM
    expand = (jnp.arange(LANES)[:, None] == head_of_channel[None, :]).astype(BF16)
    pad16 = lambda x: jnp.pad(x, (0, LANES - SSD_HEADS)).reshape(1, LANES)
    ya = _ssd(proj3, dt3, conv_w[:, :D], conv_b[:D].reshape(1, D), conv_w[:, D:], conv_b[D:].reshape(1, 512),
              pad16(dt_bias), pad16(a_log), expand, jnp.repeat(d_skip, SSD_HEAD_DIM).reshape(1, D),
              norm_w.reshape(1, D))

    C = RET_CHUNK
    half = RET_QK_HEAD // 2
    pos = jnp.arange(S, dtype=F32)
    inv_freq = ROPE_BASE ** (-jnp.arange(half, dtype=F32) / half)
    ang = pos[:, None] * inv_freq[None, :]
    cc = jnp.concatenate([jnp.cos(ang), jnp.cos(ang)], axis=1)
    ss = jnp.concatenate([-jnp.sin(ang), jnp.sin(ang)], axis=1)
    log_gamma = jnp.log(1.0 - 2.0 ** (-5.0 - jnp.arange(RET_HEADS, dtype=F32)))
    idx = jnp.arange(C, dtype=F32)
    diff = idx[:, None] - idx[None, :]
    dmask = jnp.where(diff[None] >= 0, jnp.exp(jnp.maximum(diff, 0.0)[None] * log_gamma[:, None, None]), 0.0)
    qdec = jnp.repeat(jnp.exp((idx[:, None] + 1.0) * log_gamma[None, :]), RET_V_HEAD, axis=1)
    kdec = jnp.repeat(jnp.exp((C - 1.0 - idx[:, None]) * log_gamma[None, :]), RET_QK_HEAD, axis=1)
    cdec = jnp.repeat(jnp.exp(C * log_gamma), RET_V_HEAD).reshape(1, D)
    yb = _retention(proj3, cc, ss, dmask, qdec, kdec, cdec)

    w_out_b = w_out.astype(BF16)
    return _outproj(ya.reshape(T, D), yb.reshape(T, D), w_out_b[:D], w_out_b[D:], h2, tm)


def _rwkv_layer(h2, B, S, nw, mu, w_r, w_k, w_v, w_o, w0, w1, w2, a0, a1, a2, g1, g2, k_k, k_a, r_k, lnx_w, lnx_b):
    D = D_MODEL
    T = B * S
    padc = lambda w, n: jnp.pad(w, ((0, 0), (0, n - w.shape[1]))).astype(BF16)
    padr = lambda w, n: jnp.pad(w, ((0, n - w.shape[0]), (0, 0))).astype(BF16)
    head_of_channel = jnp.arange(D) // RWKV_HEAD
    hs = (head_of_channel[:, None] == jnp.arange(LANES)[None, :]).astype(BF16)
    row = lambda x: x.reshape(1, D)
    tm = _pick(S, (256, 128))
    r, k, v, kk, a, lw, g = _rwkv_pre(
        h2, row(nw), mu, w_r.astype(BF16), w_k.astype(BF16), w_v.astype(BF16),
        padc(w1, RWKV_LORA_PAD), padr(w2, RWKV_LORA_PAD), row(w0),
        padc(a1, RWKV_LORA_PAD), padr(a2, RWKV_LORA_PAD), row(a0),
        padc(g1, RWKV_GATE_PAD), padr(g2, RWKV_GATE_PAD), row(k_k), row(k_a), hs, hs.T, S, tm)
    sh = lambda x: x.reshape(B, S, D)
    tb = _pick(S, (512, 256, 128, 64))
    y = _rwkv_scan(sh(r), sh(k), sh(v), sh(kk), sh(a), sh(lw), sh(g),
                   row(lnx_w), row(lnx_b), r_k.reshape(1, D), tb, 2)
    return _rwkv_out(y.reshape(T, D), w_o.astype(BF16), h2, _pick(T, (1024, 512, 256, 128)))


def kernel(x, norm_mix_w, norm_ffn_w, norm_final_w, w_in_e, ssd_conv_w, ssd_conv_b, ssd_dt_bias, ssd_a_log, ssd_d, ssd_norm_w, w_out_e, rw_mu, rw_wr, rw_wk, rw_wv, rw_wo, rw_w0, rw_w1, rw_w2, rw_a0, rw_a1, rw_a2, rw_g1, rw_g2, rw_kk, rw_ka, rw_rk, rw_lnx_w, rw_lnx_b, moe_wg, moe_bg, moe_we, moe_be, moe_w1, moe_w3, moe_w2):
    B, S, D = x.shape
    T = B * S
    depth = norm_mix_w.shape[0]
    h = x.reshape(T, D)
    tm_moe = _pick(T, (1024, 512, 256, 128))
    for layer in range(depth):
        i = layer // 2
        if layer % 2 == 0:
            h = _ssd_retention_layer(h, B, S, norm_mix_w[layer], w_in_e[i], ssd_conv_w[i], ssd_conv_b[i],
                                     ssd_dt_bias[i], ssd_a_log[i], ssd_d[i], ssd_norm_w[i], w_out_e[i])
        else:
            h = _rwkv_layer(h, B, S, norm_mix_w[layer], rw_mu[i], rw_wr[i], rw_wk[i], rw_wv[i], rw_wo[i],
                            rw_w0[i], rw_w1[i], rw_w2[i], rw_a0[i], rw_a1[i], rw_a2[i], rw_g1[i], rw_g2[i],
                            rw_kk[i], rw_ka[i], rw_rk[i], rw_lnx_w[i], rw_lnx_b[i])
        h = _moe(h, norm_ffn_w[layer], moe_wg[layer], moe_bg[layer], moe_we[layer], moe_be[layer],
                 moe_w1[layer], moe_w3[layer], moe_w2[layer], norm_final_w,
                 final_norm=(layer == depth - 1), tm=tm_moe)
    return h.reshape(B, S, D)
```

```python
import functools
import math

import jax
import jax.numpy as jnp
from jax import lax
from jax.experimental import pallas as pl
from jax.experimental.pallas import tpu as pltpu

F32 = jnp.float32
BF16 = jnp.bfloat16

D_MODEL = 1024
RMS_EPS = 1e-6
SSD_HEADS = 16
SSD_HEAD_DIM = 64
SSD_GROUPS = 2
SSD_STATE = 128
SSD_CONV = 4
SSD_CHUNK = 128
SSD_NORM_EPS = 1e-5
SSD_GROUP_WIDTH = D_MODEL // SSD_GROUPS
RET_HEADS = 4
RET_QK_HEAD = 128
RET_V_HEAD = 256
RET_CHUNK = 128
ROPE_BASE = 10000.0
RWKV_HEAD = 64
RWKV_HEADS = 16
RWKV_LN_EPS = 64e-5
RWKV_CHUNK = 64
RWKV_LORA_PAD = 128
RWKV_GATE_PAD = 256
MOE_GROUPS = 4
MOE_EPG = 4
MOE_EXPERTS = 16
MOE_D_FF = 512
ROUTER_LANES = 128

LANES = 128
VMEM_LIMIT_BYTES = 56 * 1024 * 1024


def _cparams(sem):
    return pltpu.CompilerParams(dimension_semantics=sem, vmem_limit_bytes=VMEM_LIMIT_BYTES)


def _dot(a, b):
    return jnp.dot(a, b, preferred_element_type=F32)


def _dot_nt(a, b):
    return lax.dot_general(a, b, (((1,), (1,)), ((), ())), preferred_element_type=F32)


def _split2(x):
    hi = x.astype(BF16)
    lo = (x - hi.astype(F32)).astype(BF16)
    return hi, lo


def _dot_x2(x, w_exact):
    hi, lo = _split2(x)
    return _dot(hi, w_exact) + _dot(lo, w_exact)


def _dot_2x(w_exact, x):
    hi, lo = _split2(x)
    return _dot(w_exact, hi) + _dot(w_exact, lo)


def _dot_x3(x, w):
    xh, xl = _split2(x)
    wh, wl = _split2(w)
    return _dot(xh, wh) + _dot(xl, wh) + _dot(xh, wl)


def _sigmoid(x):
    return 1.0 / (1.0 + jnp.exp(-x))


def _silu(x):
    return x * _sigmoid(x)


def _softplus(x):
    return jnp.maximum(x, 0.0) + jnp.log(1.0 + jnp.exp(-jnp.abs(x)))


def _rms(x, w, eps=RMS_EPS):
    return x * lax.rsqrt(jnp.mean(x * x, axis=-1, keepdims=True) + eps) * w


def _inproj_kernel(x_ref, nw_ref, w_ref, wdt_ref, o_ref, dt_ref, u_ref):
    @pl.when(pl.program_id(1) == 0)
    def _():
        ub = _rms(x_ref[...], nw_ref[...]).astype(BF16)
        u_ref[...] = ub
        dt_ref[...] = _dot(ub, wdt_ref[...])

    o_ref[...] = _dot(u_ref[...], w_ref[...]).astype(o_ref.dtype)


def _inproj(h2, nw, w_main, w_dt, tm, tn):
    T, D = h2.shape
    N = w_main.shape[1]
    return pl.pallas_call(
        _inproj_kernel,
        grid=(T // tm, N // tn),
        in_specs=[
            pl.BlockSpec((tm, D), lambda i, j: (i, 0)),
            pl.BlockSpec((1, D), lambda i, j: (0, 0)),
            pl.BlockSpec((D, tn), lambda i, j: (0, j)),
            pl.BlockSpec((D, LANES), lambda i, j: (0, 0)),
        ],
        out_specs=[
            pl.BlockSpec((tm, tn), lambda i, j: (i, j)),
            pl.BlockSpec((tm, LANES), lambda i, j: (i, 0)),
        ],
        out_shape=[
            jax.ShapeDtypeStruct((T, N), BF16),
            jax.ShapeDtypeStruct((T, LANES), F32),
        ],
        scratch_shapes=[pltpu.VMEM((tm, D), BF16)],
        compiler_params=_cparams(("parallel", "arbitrary")),
    )(h2, nw, w_main, w_dt)


def _ssd_kernel(z_ref, x_ref, bc_ref, dt_ref, cwx_ref, cbx_ref, cwb_ref, cbb_ref,
                dtb_ref, alog_ref, e_ref, dskip_ref, nw_ref, o_ref,
                state_ref, xbuf, bcbuf, ybuf):
    L = SSD_CHUNK
    c = pl.program_id(1)

    @pl.when(c == 0)
    def _():
        state_ref[...] = jnp.zeros_like(state_ref)
        xbuf[0:8, :] = jnp.zeros((8, xbuf.shape[1]), F32)
        bcbuf[0:8, :] = jnp.zeros((8, bcbuf.shape[1]), F32)

    def conv(cur_ref, buf, w_ref, b_ref):
        cur = cur_ref[0].astype(F32)
        buf[8:8 + L, :] = cur
        acc = b_ref[...] + w_ref[0:1, :] * buf[5:5 + L, :]
        for k in range(1, SSD_CONV):
            acc = acc + w_ref[k:k + 1, :] * buf[5 + k:5 + k + L, :]
        buf[0:8, :] = cur[L - 8:L, :]
        return _silu(acc)

    xs = conv(x_ref, xbuf, cwx_ref, cbx_ref)
    bc = conv(bc_ref, bcbuf, cwb_ref, cbb_ref)

    dt = _softplus(dt_ref[0] + dtb_ref[...])
    a = dt * (-jnp.exp(alog_ref[...]))
    row = lax.broadcasted_iota(jnp.int32, (L, L), 0)
    col = lax.broadcasted_iota(jnp.int32, (L, L), 1)
    causal = row >= col
    tri = jnp.where(causal, 1.0, 0.0).astype(BF16)
    acum = _dot_2x(tri, a)
    acum_t = acum.T
    e = e_ref[...]
    dt_full = _dot_x2(dt, e)
    acum_full = _dot_x2(acum, e)
    xdt = xs * dt_full
    alast_full = acum_full[L - 1:L, :]
    decay_in = jnp.exp(acum_full)
    xdt_end = (xdt * jnp.exp(alast_full - acum_full)).astype(BF16)
    xdt_b = xdt.astype(BF16)
    lane = lax.broadcasted_iota(jnp.int32, (L, LANES), 1)
    first_head = lane < SSD_HEAD_DIM

    GW = SSD_GROUP_WIDTH
    for g in range(SSD_GROUPS):
        bg = bc[:, g * SSD_STATE:(g + 1) * SSD_STATE]
        cg = bc[:, (SSD_GROUPS + g) * SSD_STATE:(SSD_GROUPS + g + 1) * SSD_STATE].astype(BF16)
        cb = _dot_nt(cg, bg.astype(BF16))
        st = state_ref[g]
        y_off = _dot(cg, st.astype(BF16)) * decay_in[:, g * GW:(g + 1) * GW]
        for p in range(GW // LANES):
            xp = xdt_b[:, g * GW + p * LANES:g * GW + (p + 1) * LANES]
            ys = []
            for s in range(2):
                hd = g * (SSD_HEADS // SSD_GROUPS) + 2 * p + s
                seg = acum[:, hd:hd + 1] - acum_t[hd:hd + 1, :]
                dec = jnp.where(causal, jnp.exp(jnp.minimum(seg, 0.0)), 0.0)
                ys.append(_dot((cb * dec).astype(BF16), xp))
            yd = jnp.where(first_head, ys[0], ys[1])
            lo = g * GW + p * LANES
            ybuf[:, lo:lo + LANES] = yd + y_off[:, p * LANES:(p + 1) * LANES]
        bg_t = bg.T.astype(BF16)
        state_ref[g] = st * jnp.exp(alast_full[:, g * GW:(g + 1) * GW]) + _dot(
            bg_t, xdt_end[:, g * GW:(g + 1) * GW])

    y = ybuf[...] + xs * dskip_ref[...]
    y = y * _silu(z_ref[0].astype(F32))
    for g in range(SSD_GROUPS):
        yg = y[:, g * GW:(g + 1) * GW]
        yg = yg * lax.rsqrt(jnp.mean(yg * yg, axis=-1, keepdims=True) + SSD_NORM_EPS)
        o_ref[0, :, g * GW:(g + 1) * GW] = (yg * nw_ref[:, g * GW:(g + 1) * GW]).astype(o_ref.dtype)


def _ssd(proj3, dt3, cwx, cbx, cwb, cbb, dtb, alog, expand, dskip, nw):
    B, S, _ = proj3.shape
    L = SSD_CHUNK
    D = D_MODEL
    full = lambda shape: pl.BlockSpec(shape, lambda b, c: (0,) * len(shape))
    return pl.pallas_call(
        _ssd_kernel,
        grid=(B, S // L),
        in_specs=[
            pl.BlockSpec((1, L, D), lambda b, c: (b, c, 0)),
            pl.BlockSpec((1, L, D), lambda b, c: (b, c, 1)),
            pl.BlockSpec((1, L, 512), lambda b, c: (b, c, 8)),
            pl.BlockSpec((1, L, LANES), lambda b, c: (b, c, 0)),
            full((SSD_CONV, D)), full((1, D)), full((SSD_CONV, 512)), full((1, 512)),
            full((1, LANES)), full((1, LANES)), full((LANES, D)), full((1, D)), full((1, D)),
        ],
        out_specs=pl.BlockSpec((1, L, D), lambda b, c: (b, c, 0)),
        out_shape=jax.ShapeDtypeStruct((B, S, D), BF16),
        scratch_shapes=[
            pltpu.VMEM((SSD_GROUPS, SSD_STATE, SSD_GROUP_WIDTH), F32),
            pltpu.VMEM((L + 8, D), F32),
            pltpu.VMEM((L + 8, 512), F32),
            pltpu.VMEM((L, D), F32),
        ],
        compiler_params=_cparams(("parallel", "arbitrary")),
    )(proj3, proj3, proj3, dt3, cwx, cbx, cwb, cbb, dtb, alog, expand, dskip, nw)


def _ret_kernel(q_ref, k_ref, v_ref, g_ref, cc_ref, ss_ref, dmask_ref, qdec_ref, kdec_ref,
                cdec_ref, o_ref, r_ref):
    c = pl.program_id(1)

    @pl.when(c == 0)
    def _():
        r_ref[...] = jnp.zeros_like(r_ref)

    cc = cc_ref[...]
    ss = ss_ref[...]
    dk, dv = RET_QK_HEAD, RET_V_HEAD
    half = dk // 2

    def rope(x):
        return x * cc + pltpu.roll(x, half, 1) * ss

    for hd in range(RET_HEADS):
        q = rope(q_ref[0, :, hd * dk:(hd + 1) * dk].astype(F32))
        k = rope(k_ref[0, :, hd * dk:(hd + 1) * dk].astype(F32)) * (dk ** -0.5)
        v = v_ref[0, :, hd * dv:(hd + 1) * dv]
        qb = q.astype(BF16)
        s = _dot_nt(qb, k.astype(BF16)) * dmask_ref[hd]
        r_old = r_ref[hd]
        y = _dot(s.astype(BF16), v) + _dot(qb, r_old.astype(BF16)) * qdec_ref[:, hd * dv:(hd + 1) * dv]
        kd_t = (k * kdec_ref[:, hd * dk:(hd + 1) * dk]).T.astype(BF16)
        r_ref[hd] = r_old * cdec_ref[:, hd * dv:(hd + 1) * dv] + _dot(kd_t, v)
        y = y * lax.rsqrt(jnp.mean(y * y, axis=-1, keepdims=True) + RMS_EPS)
        y = y * _silu(g_ref[0, :, hd * dv:(hd + 1) * dv].astype(F32))
        o_ref[0, :, hd * dv:(hd + 1) * dv] = y.astype(o_ref.dtype)


def _retention(proj3, cc, ss, dmask, qdec, kdec, cdec):
    B, S, _ = proj3.shape
    C = RET_CHUNK
    D = D_MODEL
    full = lambda shape: pl.BlockSpec(shape, lambda b, c: (0,) * len(shape))
    return pl.pallas_call(
        _ret_kernel,
        grid=(B, S // C),
        in_specs=[
            pl.BlockSpec((1, C, 512), lambda b, c: (b, c, 9)),
            pl.BlockSpec((1, C, 512), lambda b, c: (b, c, 10)),
            pl.BlockSpec((1, C, D), lambda b, c: (b, c, 2)),
            pl.BlockSpec((1, C, D), lambda b, c: (b, c, 3)),
            pl.BlockSpec((C, RET_QK_HEAD), lambda b, c: (c, 0)),
            pl.BlockSpec((C, RET_QK_HEAD), lambda b, c: (c, 0)),
            full((RET_HEADS, C, C)), full((C, D)), full((C, 512)), full((1, D)),
        ],
        out_specs=pl.BlockSpec((1, C, D), lambda b, c: (b, c, 0)),
        out_shape=jax.ShapeDtypeStruct((B, S, D), BF16),
        scratch_shapes=[pltpu.VMEM((RET_HEADS, RET_QK_HEAD, RET_V_HEAD), F32)],
        compiler_params=_cparams(("parallel", "arbitrary")),
    )(proj3, proj3, proj3, proj3, cc, ss, dmask, qdec, kdec, cdec)


def _outproj_kernel(ya_ref, yb_ref, wa_ref, wb_ref, h_ref, o_ref):
    o_ref[...] = h_ref[...] + _dot(ya_ref[...], wa_ref[...]) + _dot(yb_ref[...], wb_ref[...])


def _outproj(ya, yb, wa, wb, h2, tm):
    T, D = h2.shape
    return pl.pallas_call(
        _outproj_kernel,
        grid=(T // tm,),
        in_specs=[
            pl.BlockSpec((tm, D), lambda i: (i, 0)),
            pl.BlockSpec((tm, D), lambda i: (i, 0)),
            pl.BlockSpec((D, D), lambda i: (0, 0)),
            pl.BlockSpec((D, D), lambda i: (0, 0)),
            pl.BlockSpec((tm, D), lambda i: (i, 0)),
        ],
        out_specs=pl.BlockSpec((tm, D), lambda i: (i, 0)),
        out_shape=jax.ShapeDtypeStruct((T, D), F32),
        compiler_params=_cparams(("parallel",)),
    )(ya, yb, wa, wb, h2)


def _rwkv_pre_kernel(h_ref, hp_ref, nw_ref, mu_ref, wr_ref, wk_ref, wv_ref,
                     w1_ref, w2_ref, w0_ref, a1_ref, a2_ref, a0_ref, g1_ref, g2_ref,
                     kkw_ref, kaw_ref, hs_ref, hst_ref,
                     r_o, k_o, v_o, kk_o, a_o, lw_o, g_o, ubuf, *, tiles_per_seq):
    tm = h_ref.shape[0]
    i = pl.program_id(0)
    nw = nw_ref[...]
    u = _rms(h_ref[...], nw)
    up = _rms(hp_ref[...], nw)
    seq_start = (i % tiles_per_seq) == 0
    ubuf[8:8 + tm, :] = u
    ubuf[0:8, :] = jnp.where(seq_start, 0.0, up)
    xx = ubuf[7:7 + tm, :] - u

    def mix(j):
        return (u + xx * mu_ref[j:j + 1, :]).astype(BF16)

    r = _dot(mix(0), wr_ref[...])
    k = _dot(mix(2), wk_ref[...])
    v = _dot(mix(3), wv_ref[...])
    wl = w0_ref[...] + _dot(jnp.tanh(_dot(mix(1), w1_ref[...])).astype(BF16), w2_ref[...])
    w_log = -_softplus(-wl) - 0.5
    lw = -jnp.exp(w_log)
    a = _sigmoid(a0_ref[...] + _dot(_dot(mix(4), a1_ref[...]).astype(BF16), a2_ref[...]))
    g = _dot(_sigmoid(_dot(mix(5), g1_ref[...])).astype(BF16), g2_ref[...])
    kk = k * kkw_ref[...]
    ssq = _dot((kk * kk).astype(BF16), hs_ref[...])
    inv = lax.rsqrt(jnp.maximum(ssq, 1e-24))
    kk = kk * _dot_x2(inv, hst_ref[...])
    k = k * (1.0 + (a - 1.0) * kaw_ref[...])
    r_o[...] = r.astype(r_o.dtype)
    k_o[...] = k.astype(k_o.dtype)
    v_o[...] = v.astype(v_o.dtype)
    kk_o[...] = kk.astype(kk_o.dtype)
    a_o[...] = a.astype(a_o.dtype)
    lw_o[...] = lw
    g_o[...] = g.astype(g_o.dtype)


def _rwkv_pre(h2, nw, mu, wr, wk, wv, w1, w2, w0, a1, a2, a0, g1, g2, kkw, kaw, hs, hst, S, tm):
    T, D = h2.shape
    full = lambda arr: pl.BlockSpec(arr.shape, lambda i: (0,) * arr.ndim)
    row = pl.BlockSpec((tm, D), lambda i: (i, 0))
    params = (nw, mu, wr, wk, wv, w1, w2, w0, a1, a2, a0, g1, g2, kkw, kaw, hs, hst)
    bf = jax.ShapeDtypeStruct((T, D), BF16)
    return pl.pallas_call(
        functools.partial(_rwkv_pre_kernel, tiles_per_seq=S // tm),
        grid=(T // tm,),
        in_specs=[row, pl.BlockSpec((8, D), lambda i: (jnp.maximum(i * (tm // 8) - 1, 0), 0))]
        + [full(p) for p in params],
        out_specs=[row] * 7,
        out_shape=[bf, bf, bf, bf, bf, jax.ShapeDtypeStruct((T, D), F32), bf],
        scratch_shapes=[pltpu.VMEM((tm + 8, D), F32)],
        compiler_params=_cparams(("parallel",)),
    )(h2, h2, *params)


def _rwkv_scan_kernel(r_ref, k_ref, v_ref, kk_ref, a_ref, lw_ref, g_ref, lnw_ref, lnb_ref, rk_ref,
                      o_ref, state_ref, *, n_chunks, n_pairs):
    C = RWKV_CHUNK
    N = RWKV_HEAD
    t = pl.program_id(2)

    @pl.when(t == 0)
    def _():
        state_ref[...] = jnp.zeros_like(state_ref)

    row = lax.broadcasted_iota(jnp.int32, (C, C), 0)
    col = lax.broadcasted_iota(jnp.int32, (C, C), 1)
    tri_incl = jnp.where(row >= col, 1.0, 0.0).astype(BF16)
    eye = jnp.where(row == col, 1.0, 0.0).astype(F32)
    grow = lax.broadcasted_iota(jnp.int32, (2 * C, 2 * C), 0)
    gcol = lax.broadcasted_iota(jnp.int32, (2 * C, 2 * C), 1)
    gr, gc = grow % C, gcol % C
    gmask = (gr + grow // C) > gc
    blockdiag = (grow // N) == (gcol // N)
    lane = lax.broadcasted_iota(jnp.int32, (C, LANES), 1)
    head_masks = (lane < N, lane >= N)
    upper_lanes = lane >= C

    m0 = head_masks[0]
    n_levels = int(math.log2(C))
    pairs = range(n_pairs)
    chains = [(p, s) for p in pairs for s in range(2)]

    def head_sum(x):
        s0 = jnp.sum(jnp.where(m0, x, 0.0), axis=-1, keepdims=True)
        s1 = jnp.sum(jnp.where(m0, 0.0, x), axis=-1, keepdims=True)
        return jnp.where(m0, s0, s1)

    def chunk_body(ci, carry):
        rows = pl.ds(pl.multiple_of(ci * C, C), C)
        cum_all = _dot_2x(tri_incl, lw_ref[0, rows, :])
        pre = []
        for p in pairs:
            lanes = slice(p * LANES, (p + 1) * LANES)
            r = r_ref[0, rows, lanes].astype(F32)
            k = k_ref[0, rows, lanes].astype(F32)
            v = v_ref[0, rows, lanes].astype(F32)
            kk = kk_ref[0, rows, lanes].astype(F32)
            a = a_ref[0, rows, lanes].astype(F32)
            lw = lw_ref[0, rows, lanes]
            cum = cum_all[:, lanes]
            wc = cum[C - 1:C, :]
            e_out = jnp.exp(-cum)
            e_end = jnp.exp(wc - cum)
            b = kk * a
            at = -kk * jnp.exp(cum - lw)
            rt = r * jnp.exp(cum)
            rhs = jnp.concatenate([b * e_out, k * e_out], axis=0).astype(BF16)
            kb_end = jnp.concatenate([k * e_end, b * e_end], axis=0).astype(BF16)
            vb = v.astype(BF16)
            pre.append(dict(lanes=lanes, r=r, k=k, v=v, wc=wc, at=at, rt=rt, rhs=rhs, kb_end=kb_end,
                            vb=vb, vv=jnp.concatenate([vb, vb], axis=0)))

        grams = []
        for p, s in chains:
            d, m = pre[p], head_masks[s]
            lhs = jnp.concatenate([jnp.where(m, d["at"], 0.0), jnp.where(m, d["rt"], 0.0)], axis=0)
            grams.append(jnp.where(gmask, _dot_nt(lhs.astype(BF16), d["rhs"]), 0.0))
        tops = [g[0:C, :] for g in grams]
        g_bot = [g[C:2 * C, :].astype(BF16) for g in grams]
        mvs = [_dot(jnp.where(upper_lanes, top, 0.0).astype(BF16), pre[p]["vv"])
               for top, (p, s) in zip(tops, chains)]
        lms = [top[:, 0:C] for top in tops]
        pks = [eye + lm for lm in lms]
        lks = [_dot(lm.astype(BF16), lm.astype(BF16)) for lm in lms]
        for lvl in range(1, n_levels):
            if lvl < n_levels - 1:
                boths = [_dot(jnp.concatenate([pk, lk], axis=0).astype(BF16), lk.astype(BF16))
                         for pk, lk in zip(pks, lks)]
                pks = [pk + bo[0:C, :] for pk, bo in zip(pks, boths)]
                lks = [bo[C:2 * C, :] for bo in boths]
            else:
                pks = [pk + _dot(pk.astype(BF16), lk.astype(BF16)) for pk, lk in zip(pks, lks)]
        tws = [_dot(pk.astype(BF16), jnp.concatenate([mv, pre[p]["at"]], axis=1).astype(BF16))
               for pk, mv, (p, s) in zip(pks, mvs, chains)]

        sts = [state_ref[p] for p in pairs]
        zs = []
        for p in pairs:
            atp = jnp.where(m0, tws[2 * p][:, LANES:2 * LANES], tws[2 * p + 1][:, LANES:2 * LANES])
            zs.append(_dot_nt(jnp.concatenate([atp, pre[p]["rt"]], axis=0).astype(BF16), sts[p].astype(BF16)))
        us = [jnp.where(m0, tws[2 * p][:, 0:LANES], tws[2 * p + 1][:, 0:LANES]) + zs[p][0:C, :] for p in pairs]
        uvs = [jnp.concatenate([us[p].astype(BF16), pre[p]["vb"]], axis=0) for p in pairs]
        yss = [_dot(g_bot[i], uvs[p]) for i, (p, s) in enumerate(chains)]
        upds = [_dot(jnp.concatenate([pre[p]["v"], us[p]], axis=0).T.astype(BF16), pre[p]["kb_end"])
                for p in pairs]
        for p in pairs:
            d = pre[p]
            lanes = d["lanes"]
            state_ref[p] = sts[p] * jnp.exp(d["wc"]) + jnp.where(blockdiag, upds[p], 0.0)
            y = zs[p][C:2 * C, :] + jnp.where(m0, yss[2 * p], yss[2 * p + 1])
            mean = head_sum(y) * (1.0 / N)
            yc = y - mean
            var = head_sum(yc * yc) * (1.0 / N)
            yn = yc * lax.rsqrt(var + RWKV_LN_EPS) * lnw_ref[:, lanes] + lnb_ref[:, lanes]
            bonus = head_sum(d["r"] * d["k"] * rk_ref[:, lanes]) * d["v"]
            out = (yn + bonus) * g_ref[0, rows, lanes].astype(F32)
            o_ref[0, rows, lanes] = out.astype(o_ref.dtype)
        return carry

    lax.fori_loop(0, n_chunks, chunk_body, 0)


def _rwkv_scan(r, k, v, kk, a, lw, g, lnw, lnb, rk, tb, n_pairs):
    B, S, D = r.shape
    W = n_pairs * LANES
    blk = pl.BlockSpec((1, tb, W), lambda b, j, t: (b, t, j))
    par = pl.BlockSpec((1, W), lambda b, j, t: (0, j))
    return pl.pallas_call(
        functools.partial(_rwkv_scan_kernel, n_chunks=tb // RWKV_CHUNK, n_pairs=n_pairs),
        grid=(B, D // W, S // tb),
        in_specs=[blk] * 7 + [par] * 3,
        out_specs=blk,
        out_shape=jax.ShapeDtypeStruct((B, S, D), BF16),
        scratch_shapes=[pltpu.VMEM((n_pairs, LANES, LANES), F32)],
        compiler_params=_cparams(("parallel", "parallel", "arbitrary")),
    )(r, k, v, kk, a, lw, g, lnw, lnb, rk)


def _rwkv_out_kernel(y_ref, w_ref, h_ref, o_ref):
    o_ref[...] = h_ref[...] + _dot(y_ref[...], w_ref[...])


def _rwkv_out(y, wo, h2, tm):
    T, D = h2.shape
    return pl.pallas_call(
        _rwkv_out_kernel,
        grid=(T // tm,),
        in_specs=[pl.BlockSpec((tm, D), lambda i: (i, 0)), pl.BlockSpec((D, D), lambda i: (0, 0)),
                  pl.BlockSpec((tm, D), lambda i: (i, 0))],
        out_specs=pl.BlockSpec((tm, D), lambda i: (i, 0)),
        out_shape=jax.ShapeDtypeStruct((T, D), F32),
        compiler_params=_cparams(("parallel",)),
    )(y, wo, h2)


def _router_kernel(h_ref, nw_ref, wr_ref, br_ref, t_ref, cw_ref):
    tn = _rms(h_ref[...], nw_ref[...])
    t_ref[...] = tn.astype(t_ref.dtype)
    logits = _dot_x3(tn, wr_ref[...]) + br_ref[...]
    lane = lax.broadcasted_iota(jnp.int32, logits.shape, 1).astype(F32)
    neg = jnp.float32(-jnp.inf)
    big = jnp.float32(1 << 20)
    is_g = lane < MOE_GROUPS
    gl = jnp.where(is_g, logits, neg)
    gmax = jnp.max(gl, axis=-1, keepdims=True)
    g_idx = jnp.min(jnp.where(gl == gmax, lane, big), axis=-1, keepdims=True)
    g_gate = 1.0 / jnp.sum(jnp.where(is_g, jnp.exp(logits - gmax), 0.0), axis=-1, keepdims=True)
    lo = MOE_GROUPS + g_idx * MOE_EPG
    sel = (lane >= lo) & (lane < lo + MOE_EPG)
    el = jnp.where(sel, logits, neg)
    m1 = jnp.max(el, axis=-1, keepdims=True)
    i1 = jnp.min(jnp.where(el == m1, lane, big), axis=-1, keepdims=True)
    el2 = jnp.where(lane == i1, neg, el)
    m2 = jnp.max(el2, axis=-1, keepdims=True)
    i2 = jnp.min(jnp.where(el2 == m2, lane, big), axis=-1, keepdims=True)
    e21 = jnp.exp(m2 - m1)
    w1 = g_gate / (1.0 + e21)
    w2 = w1 * e21
    cw_ref[...] = jnp.where(lane == i1, w1, jnp.where(lane == i2, w2, 0.0))


def _router(h2, nw, wr, br, tm):
    T, D = h2.shape
    return pl.pallas_call(
        _router_kernel,
        grid=(T // tm,),
        in_specs=[pl.BlockSpec((tm, D), lambda i: (i, 0)), pl.BlockSpec((1, D), lambda i: (0, 0)),
                  pl.BlockSpec((D, ROUTER_LANES), lambda i: (0, 0)),
                  pl.BlockSpec((1, ROUTER_LANES), lambda i: (0, 0))],
        out_specs=[pl.BlockSpec((tm, D), lambda i: (i, 0)),
                   pl.BlockSpec((tm, ROUTER_LANES), lambda i: (i, 0))],
        out_shape=[jax.ShapeDtypeStruct((T, D), BF16),
                   jax.ShapeDtypeStruct((T, ROUTER_LANES), F32)],
        compiler_params=_cparams(("parallel",)),
    )(h2, nw, wr, br)


def _experts_kernel(t_ref, cw_ref, h_ref, w1_ref, w3_ref, w2_ref, fnw_ref, o_ref, *, final_norm):
    e = pl.program_id(1)

    @pl.when(e == 0)
    def _():
        o_ref[...] = h_ref[...]

    t = t_ref[...]
    a = _dot(t, w1_ref[0])
    b = _dot(t, w3_ref[0])
    hh = (_silu(a) * b).astype(BF16)
    cw = cw_ref[...]
    lane = lax.broadcasted_iota(jnp.int32, cw.shape, 1)
    cwe = jnp.sum(jnp.where(lane == e + MOE_GROUPS, cw, 0.0), axis=-1, keepdims=True)
    o_ref[...] += cwe * _dot(hh, w2_ref[0])

    if final_norm:
        @pl.when(e == MOE_EXPERTS - 1)
        def _():
            o_ref[...] = _rms(o_ref[...], fnw_ref[...])


def _experts(t, cw, h2, w1, w3, w2, fnw, tm, final_norm):
    T, D = h2.shape
    F = MOE_D_FF
    return pl.pallas_call(
        functools.partial(_experts_kernel, final_norm=final_norm),
        grid=(T // tm, MOE_EXPERTS),
        in_specs=[
            pl.BlockSpec((tm, D), lambda i, e: (i, 0)),
            pl.BlockSpec((tm, ROUTER_LANES), lambda i, e: (i, 0)),
            pl.BlockSpec((tm, D), lambda i, e: (i, 0)),
            pl.BlockSpec((1, D, F), lambda i, e: (e, 0, 0)),
            pl.BlockSpec((1, D, F), lambda i, e: (e, 0, 0)),
            pl.BlockSpec((1, F, D), lambda i, e: (e, 0, 0)),
            pl.BlockSpec((1, D), lambda i, e: (0, 0)),
        ],
        out_specs=pl.BlockSpec((tm, D), lambda i, e: (i, 0)),
        out_shape=jax.ShapeDtypeStruct((T, D), F32),
        compiler_params=_cparams(("parallel", "arbitrary")),
    )(t, cw, h2, w1, w3, w2, fnw)


def _moe(h2, nw, wg, bg, we, be, w1, w3, w2, fnw, final_norm, tm):
    D = D_MODEL
    pad = ROUTER_LANES - MOE_GROUPS - MOE_EXPERTS
    wr = jnp.concatenate([wg, we, jnp.zeros((D, pad), F32)], axis=1)
    br = jnp.concatenate([bg, be, jnp.zeros((pad,), F32)]).reshape(1, ROUTER_LANES)
    t, cw = _router(h2, nw.reshape(1, D), wr, br, tm)
    return _experts(t, cw, h2, w1.astype(BF16), w3.astype(BF16), w2.astype(BF16),
                    fnw.reshape(1, D), tm, final_norm)


def _pick(n, prefs):
    for p in prefs:
        if n % p == 0:
            return p
    return n


def _ssd_retention_layer(h2, B, S, nw, w_in, conv_w, conv_b, dt_bias, a_log, d_skip, norm_w, w_out):
    D = D_MODEL
    T = B * S
    o_x, o_bc, o_dt, o_q, o_k, o_v, o_g = 1024, 2048, 2560, 2576, 3088, 3600, 4624
    w_main = jnp.concatenate([w_in[:, 0:o_x], w_in[:, o_x:o_bc], w_in[:, o_v:o_g], w_in[:, o_g:o_g + D],
                              w_in[:, o_bc:o_dt], w_in[:, o_q:o_k], w_in[:, o_k:o_v]], axis=1).astype(BF16)
    w_dt = jnp.pad(w_in[:, o_dt:o_q], ((0, 0), (0, LANES - SSD_HEADS))).astype(BF16)
    tm = _pick(T, (1024, 512, 256, 128))
    proj, dt = _inproj(h2, nw.reshape(1, D), w_main, w_dt, tm, 1408)
    proj3 = proj.reshape(B, S, proj.shape[1])
    dt3 = dt.reshape(B, S, LANES)

    head_of_channel = jnp.arange(D) // SSD_HEAD_DIM
    expand = (jnp.arange(LANES)[:, None] == head_of_channel[None, :]).astype(BF16)
    pad16 = lambda x: jnp.pad(x, (0, LANES - SSD_HEADS)).reshape(1, LANES)
    ya = _ssd(proj3, dt3, conv_w[:, :D], conv_b[:D].reshape(1, D), conv_w[:, D:], conv_b[D:].reshape(1, 512),
              pad16(dt_bias), pad16(a_log), expand, jnp.repeat(d_skip, SSD_HEAD_DIM).reshape(1, D),
              norm_w.reshape(1, D))

    C = RET_CHUNK
    half = RET_QK_HEAD // 2
    pos = jnp.arange(S, dtype=F32)
    inv_freq = ROPE_BASE ** (-jnp.arange(half, dtype=F32) / half)
    ang = pos[:, None] * inv_freq[None, :]
    cc = jnp.concatenate([jnp.cos(ang), jnp.cos(ang)], axis=1)
    ss = jnp.concatenate([-jnp.sin(ang), jnp.sin(ang)], axis=1)
    log_gamma = jnp.log(1.0 - 2.0 ** (-5.0 - jnp.arange(RET_HEADS, dtype=F32)))
    idx = jnp.arange(C, dtype=F32)
    diff = idx[:, None] - idx[None, :]
    dmask = jnp.where(diff[None] >= 0, jnp.exp(jnp.maximum(diff, 0.0)[None] * log_gamma[:, None, None]), 0.0)
    qdec = jnp.repeat(jnp.exp((idx[:, None] + 1.0) * log_gamma[None, :]), RET_V_HEAD, axis=1)
    kdec = jnp.repeat(jnp.exp((C - 1.0 - idx[:, None]) * log_gamma[None, :]), RET_QK_HEAD, axis=1)
    cdec = jnp.repeat(jnp.exp(C * log_gamma), RET_V_HEAD).reshape(1, D)
    yb = _retention(proj3, cc, ss, dmask, qdec, kdec, cdec)

    w_out_b = w_out.astype(BF16)
    return _outproj(ya.reshape(T, D), yb.reshape(T, D), w_out_b[:D], w_out_b[D:], h2, tm)


def _rwkv_layer(h2, B, S, nw, mu, w_r, w_k, w_v, w_o, w0, w1, w2, a0, a1, a2, g1, g2, k_k, k_a, r_k, lnx_w, lnx_b):
    D = D_MODEL
    T = B * S
    padc = lambda w, n: jnp.pad(w, ((0, 0), (0, n - w.shape[1]))).astype(BF16)
    padr = lambda w, n: jnp.pad(w, ((0, n - w.shape[0]), (0, 0))).astype(BF16)
    head_of_channel = jnp.arange(D) // RWKV_HEAD
    hs = (head_of_channel[:, None] == jnp.arange(LANES)[None, :]).astype(BF16)
    row = lambda x: x.reshape(1, D)
    tm = _pick(S, (256, 128))
    r, k, v, kk, a, lw, g = _rwkv_pre(
        h2, row(nw), mu, w_r.astype(BF16), w_k.astype(BF16), w_v.astype(BF16),
        padc(w1, RWKV_LORA_PAD), padr(w2, RWKV_LORA_PAD), row(w0),
        padc(a1, RWKV_LORA_PAD), padr(a2, RWKV_LORA_PAD), row(a0),
        padc(g1, RWKV_GATE_PAD), padr(g2, RWKV_GATE_PAD), row(k_k), row(k_a), hs, hs.T, S, tm)
    sh = lambda x: x.reshape(B, S, D)
    tb = _pick(S, (512, 256, 128, 64))
    y = _rwkv_scan(sh(r), sh(k), sh(v), sh(kk), sh(a), sh(lw), sh(g),
                   row(lnx_w), row(lnx_b), r_k.reshape(1, D), tb, 8)
    return _rwkv_out(y.reshape(T, D), w_o.astype(BF16), h2, _pick(T, (1024, 512, 256, 128)))


def kernel(x, norm_mix_w, norm_ffn_w, norm_final_w, w_in_e, ssd_conv_w, ssd_conv_b, ssd_dt_bias, ssd_a_log, ssd_d, ssd_norm_w, w_out_e, rw_mu, rw_wr, rw_wk, rw_wv, rw_wo, rw_w0, rw_w1, rw_w2, rw_a0, rw_a1, rw_a2, rw_g1, rw_g2, rw_kk, rw_ka, rw_rk, rw_lnx_w, rw_lnx_b, moe_wg, moe_bg, moe_we, moe_be, moe_w1, moe_w3, moe_w2):
    B, S, D = x.shape
    T = B * S
    depth = norm_mix_w.shape[0]
    h = x.reshape(T, D)
    tm_moe = _pick(T, (1024, 512, 256, 128))
    for layer in range(depth):
        i = layer // 2
        if layer % 2 == 0:
            h = _ssd_retention_layer(h, B, S, norm_mix_w[layer], w_in_e[i], ssd_conv_w[i], ssd_conv_b[i],
                                     ssd_dt_bias[i], ssd_a_log[i], ssd_d[i], ssd_norm_w[i], w_out_e[i])
        else:
            h = _rwkv_layer(h, B, S, norm_mix_w[layer], rw_mu[i], rw_wr[i], rw_wk[i], rw_wv[i], rw_wo[i],
                            rw_w0[i], rw_w1[i], rw_w2[i], rw_a0[i], rw_a1[i], rw_a2[i], rw_g1[i], rw_g2[i],
                            rw_kk[i], rw_ka[i], rw_rk[i], rw_lnx_w[i], rw_lnx_b[i])
        h = _moe(h, norm_ffn_w[layer], moe_wg[layer], moe_bg[layer], moe_we[layer], moe_be[layer],
                 moe_w1[layer], moe_w3[layer], moe_w2[layer], norm_final_w,
                 final_norm=(layer == depth - 1), tm=tm_moe)
    return h.reshape(B, S, D)
```

```python
import functools
import math

import jax
import jax.numpy as jnp
from jax import lax
from jax.experimental import pallas as pl
from jax.experimental.pallas import tpu as pltpu

F32 = jnp.float32
BF16 = jnp.bfloat16

D_MODEL = 1024
RMS_EPS = 1e-6
SSD_HEADS = 16
SSD_HEAD_DIM = 64
SSD_GROUPS = 2
SSD_STATE = 128
SSD_CONV = 4
SSD_CHUNK = 128
SSD_NORM_EPS = 1e-5
SSD_GROUP_WIDTH = D_MODEL // SSD_GROUPS
RET_HEADS = 4
RET_QK_HEAD = 128
RET_V_HEAD = 256
RET_CHUNK = 128
ROPE_BASE = 10000.0
RWKV_HEAD = 64
RWKV_HEADS = 16
RWKV_LN_EPS = 64e-5
RWKV_CHUNK = 64
RWKV_LORA_PAD = 128
RWKV_GATE_PAD = 256
MOE_GROUPS = 4
MOE_EPG = 4
MOE_EXPERTS = 16
MOE_D_FF = 512
ROUTER_LANES = 128
MOE_SUB = 128
MOE_EXPERTS_PER_STEP = 2

LANES = 128
VMEM_LIMIT_BYTES = 56 * 1024 * 1024


def _cparams(sem):
    return pltpu.CompilerParams(dimension_semantics=sem, vmem_limit_bytes=VMEM_LIMIT_BYTES)


def _dot(a, b):
    return jnp.dot(a, b, preferred_element_type=F32)


def _dot_nt(a, b):
    return lax.dot_general(a, b, (((1,), (1,)), ((), ())), preferred_element_type=F32)


def _split2(x):
    hi = x.astype(BF16)
    lo = (x - hi.astype(F32)).astype(BF16)
    return hi, lo


def _dot_x2(x, w_exact):
    hi, lo = _split2(x)
    return _dot(hi, w_exact) + _dot(lo, w_exact)


def _dot_2x(w_exact, x):
    hi, lo = _split2(x)
    return _dot(w_exact, hi) + _dot(w_exact, lo)


def _dot_x3(x, w):
    xh, xl = _split2(x)
    wh, wl = _split2(w)
    return _dot(xh, wh) + _dot(xl, wh) + _dot(xh, wl)


def _sigmoid(x):
    return 1.0 / (1.0 + jnp.exp(-x))


def _silu(x):
    return x * _sigmoid(x)


def _softplus(x):
    return jnp.maximum(x, 0.0) + jnp.log(1.0 + jnp.exp(-jnp.abs(x)))


def _rms(x, w, eps=RMS_EPS):
    return x * lax.rsqrt(jnp.mean(x * x, axis=-1, keepdims=True) + eps) * w


def _inproj_kernel(x_ref, nw_ref, w_ref, wdt_ref, o_ref, dt_ref, u_ref):
    @pl.when(pl.program_id(1) == 0)
    def _():
        ub = _rms(x_ref[...], nw_ref[...]).astype(BF16)
        u_ref[...] = ub
        dt_ref[...] = _dot(ub, wdt_ref[...])

    o_ref[...] = _dot(u_ref[...], w_ref[...]).astype(o_ref.dtype)


def _inproj(h2, nw, w_main, w_dt, tm, tn):
    T, D = h2.shape
    N = w_main.shape[1]
    return pl.pallas_call(
        _inproj_kernel,
        grid=(T // tm, N // tn),
        in_specs=[
            pl.BlockSpec((tm, D), lambda i, j: (i, 0)),
            pl.BlockSpec((1, D), lambda i, j: (0, 0)),
            pl.BlockSpec((D, tn), lambda i, j: (0, j)),
            pl.BlockSpec((D, LANES), lambda i, j: (0, 0)),
        ],
        out_specs=[
            pl.BlockSpec((tm, tn), lambda i, j: (i, j)),
            pl.BlockSpec((tm, LANES), lambda i, j: (i, 0)),
        ],
        out_shape=[
            jax.ShapeDtypeStruct((T, N), BF16),
            jax.ShapeDtypeStruct((T, LANES), F32),
        ],
        scratch_shapes=[pltpu.VMEM((tm, D), BF16)],
        compiler_params=_cparams(("parallel", "arbitrary")),
    )(h2, nw, w_main, w_dt)


def _ssd_kernel(z_ref, x_ref, bc_ref, dt_ref, cwx_ref, cbx_ref, cwb_ref, cbb_ref,
                dtb_ref, alog_ref, e_ref, dskip_ref, nw_ref, o_ref,
                state_ref, xbuf, bcbuf, ybuf):
    L = SSD_CHUNK
    c = pl.program_id(1)

    @pl.when(c == 0)
    def _():
        state_ref[...] = jnp.zeros_like(state_ref)
        xbuf[0:8, :] = jnp.zeros((8, xbuf.shape[1]), F32)
        bcbuf[0:8, :] = jnp.zeros((8, bcbuf.shape[1]), F32)

    def conv(cur_ref, buf, w_ref, b_ref):
        cur = cur_ref[0].astype(F32)
        buf[8:8 + L, :] = cur
        acc = b_ref[...] + w_ref[0:1, :] * buf[5:5 + L, :]
        for k in range(1, SSD_CONV):
            acc = acc + w_ref[k:k + 1, :] * buf[5 + k:5 + k + L, :]
        buf[0:8, :] = cur[L - 8:L, :]
        return _silu(acc)

    xs = conv(x_ref, xbuf, cwx_ref, cbx_ref)
    bc = conv(bc_ref, bcbuf, cwb_ref, cbb_ref)

    dt = _softplus(dt_ref[0] + dtb_ref[...])
    a = dt * (-jnp.exp(alog_ref[...]))
    row = lax.broadcasted_iota(jnp.int32, (L, L), 0)
    col = lax.broadcasted_iota(jnp.int32, (L, L), 1)
    causal = row >= col
    tri = jnp.where(causal, 1.0, 0.0).astype(BF16)
    acum = _dot_2x(tri, a)
    acum_t = acum.T
    e = e_ref[...]
    dt_full = _dot_x2(dt, e)
    acum_full = _dot_x2(acum, e)
    xdt = xs * dt_full
    alast_full = acum_full[L - 1:L, :]
    decay_in = jnp.exp(acum_full)
    xdt_end = (xdt * jnp.exp(alast_full - acum_full)).astype(BF16)
    xdt_b = xdt.astype(BF16)
    lane = lax.broadcasted_iota(jnp.int32, (L, LANES), 1)
    first_head = lane < SSD_HEAD_DIM

    GW = SSD_GROUP_WIDTH
    for g in range(SSD_GROUPS):
        bg = bc[:, g * SSD_STATE:(g + 1) * SSD_STATE]
        cg = bc[:, (SSD_GROUPS + g) * SSD_STATE:(SSD_GROUPS + g + 1) * SSD_STATE].astype(BF16)
        cb = _dot_nt(cg, bg.astype(BF16))
        st = state_ref[g]
        y_off = _dot(cg, st.astype(BF16)) * decay_in[:, g * GW:(g + 1) * GW]
        for p in range(GW // LANES):
            xp = xdt_b[:, g * GW + p * LANES:g * GW + (p + 1) * LANES]
            ys = []
            for s in range(2):
                hd = g * (SSD_HEADS // SSD_GROUPS) + 2 * p + s
                seg = acum[:, hd:hd + 1] - acum_t[hd:hd + 1, :]
                dec = jnp.where(causal, jnp.exp(jnp.minimum(seg, 0.0)), 0.0)
                ys.append(_dot((cb * dec).astype(BF16), xp))
            yd = jnp.where(first_head, ys[0], ys[1])
            lo = g * GW + p * LANES
            ybuf[:, lo:lo + LANES] = yd + y_off[:, p * LANES:(p + 1) * LANES]
        bg_t = bg.T.astype(BF16)
        state_ref[g] = st * jnp.exp(alast_full[:, g * GW:(g + 1) * GW]) + _dot(
            bg_t, xdt_end[:, g * GW:(g + 1) * GW])

    y = ybuf[...] + xs * dskip_ref[...]
    y = y * _silu(z_ref[0].astype(F32))
    for g in range(SSD_GROUPS):
        yg = y[:, g * GW:(g + 1) * GW]
        yg = yg * lax.rsqrt(jnp.mean(yg * yg, axis=-1, keepdims=True) + SSD_NORM_EPS)
        o_ref[0, :, g * GW:(g + 1) * GW] = (yg * nw_ref[:, g * GW:(g + 1) * GW]).astype(o_ref.dtype)


def _ssd(proj3, dt3, cwx, cbx, cwb, cbb, dtb, alog, expand, dskip, nw):
    B, S, _ = proj3.shape
    L = SSD_CHUNK
    D = D_MODEL
    full = lambda shape: pl.BlockSpec(shape, lambda b, c: (0,) * len(shape))
    return pl.pallas_call(
        _ssd_kernel,
        grid=(B, S // L),
        in_specs=[
            pl.BlockSpec((1, L, D), lambda b, c: (b, c, 0)),
            pl.BlockSpec((1, L, D), lambda b, c: (b, c, 1)),
            pl.BlockSpec((1, L, 512), lambda b, c: (b, c, 8)),
            pl.BlockSpec((1, L, LANES), lambda b, c: (b, c, 0)),
            full((SSD_CONV, D)), full((1, D)), full((SSD_CONV, 512)), full((1, 512)),
            full((1, LANES)), full((1, LANES)), full((LANES, D)), full((1, D)), full((1, D)),
        ],
        out_specs=pl.BlockSpec((1, L, D), lambda b, c: (b, c, 0)),
        out_shape=jax.ShapeDtypeStruct((B, S, D), BF16),
        scratch_shapes=[
            pltpu.VMEM((SSD_GROUPS, SSD_STATE, SSD_GROUP_WIDTH), F32),
            pltpu.VMEM((L + 8, D), F32),
            pltpu.VMEM((L + 8, 512), F32),
            pltpu.VMEM((L, D), F32),
        ],
        compiler_params=_cparams(("parallel", "arbitrary")),
    )(proj3, proj3, proj3, dt3, cwx, cbx, cwb, cbb, dtb, alog, expand, dskip, nw)


def _ret_kernel(q_ref, k_ref, v_ref, g_ref, cc_ref, ss_ref, dmask_ref, qdec_ref, kdec_ref,
                cdec_ref, o_ref, r_ref):
    c = pl.program_id(1)

    @pl.when(c == 0)
    def _():
        r_ref[...] = jnp.zeros_like(r_ref)

    cc = cc_ref[...]
    ss = ss_ref[...]
    dk, dv = RET_QK_HEAD, RET_V_HEAD
    half = dk // 2

    def rope(x):
        return x * cc + pltpu.roll(x, half, 1) * ss

    for hd in range(RET_HEADS):
        q = rope(q_ref[0, :, hd * dk:(hd + 1) * dk].astype(F32))
        k = rope(k_ref[0, :, hd * dk:(hd + 1) * dk].astype(F32)) * (dk ** -0.5)
        v = v_ref[0, :, hd * dv:(hd + 1) * dv]
        qb = q.astype(BF16)
        s = _dot_nt(qb, k.astype(BF16)) * dmask_ref[hd]
        r_old = r_ref[hd]
        y = _dot(s.astype(BF16), v) + _dot(qb, r_old.astype(BF16)) * qdec_ref[:, hd * dv:(hd + 1) * dv]
        kd_t = (k * kdec_ref[:, hd * dk:(hd + 1) * dk]).T.astype(BF16)
        r_ref[hd] = r_old * cdec_ref[:, hd * dv:(hd + 1) * dv] + _dot(kd_t, v)
        y = y * lax.rsqrt(jnp.mean(y * y, axis=-1, keepdims=True) + RMS_EPS)
        y = y * _silu(g_ref[0, :, hd * dv:(hd + 1) * dv].astype(F32))
        o_ref[0, :, hd * dv:(hd + 1) * dv] = y.astype(o_ref.dtype)


def _retention(proj3, cc, ss, dmask, qdec, kdec, cdec):
    B, S, _ = proj3.shape
    C = RET_CHUNK
    D = D_MODEL
    full = lambda shape: pl.BlockSpec(shape, lambda b, c: (0,) * len(shape))
    return pl.pallas_call(
        _ret_kernel,
        grid=(B, S // C),
        in_specs=[
            pl.BlockSpec((1, C, 512), lambda b, c: (b, c, 9)),
            pl.BlockSpec((1, C, 512), lambda b, c: (b, c, 10)),
            pl.BlockSpec((1, C, D), lambda b, c: (b, c, 2)),
            pl.BlockSpec((1, C, D), lambda b, c: (b, c, 3)),
            pl.BlockSpec((C, RET_QK_HEAD), lambda b, c: (c, 0)),
            pl.BlockSpec((C, RET_QK_HEAD), lambda b, c: (c, 0)),
            full((RET_HEADS, C, C)), full((C, D)), full((C, 512)), full((1, D)),
        ],
        out_specs=pl.BlockSpec((1, C, D), lambda b, c: (b, c, 0)),
        out_shape=jax.ShapeDtypeStruct((B, S, D), BF16),
        scratch_shapes=[pltpu.VMEM((RET_HEADS, RET_QK_HEAD, RET_V_HEAD), F32)],
        compiler_params=_cparams(("parallel", "arbitrary")),
    )(proj3, proj3, proj3, proj3, cc, ss, dmask, qdec, kdec, cdec)


def _outproj_kernel(ya_ref, yb_ref, wa_ref, wb_ref, h_ref, o_ref):
    o_ref[...] = h_ref[...] + _dot(ya_ref[...], wa_ref[...]) + _dot(yb_ref[...], wb_ref[...])


def _outproj(ya, yb, wa, wb, h2, tm):
    T, D = h2.shape
    return pl.pallas_call(
        _outproj_kernel,
        grid=(T // tm,),
        in_specs=[
            pl.BlockSpec((tm, D), lambda i: (i, 0)),
            pl.BlockSpec((tm, D), lambda i: (i, 0)),
            pl.BlockSpec((D, D), lambda i: (0, 0)),
            pl.BlockSpec((D, D), lambda i: (0, 0)),
            pl.BlockSpec((tm, D), lambda i: (i, 0)),
        ],
        out_specs=pl.BlockSpec((tm, D), lambda i: (i, 0)),
        out_shape=jax.ShapeDtypeStruct((T, D), F32),
        compiler_params=_cparams(("parallel",)),
    )(ya, yb, wa, wb, h2)


def _rwkv_pre_kernel(h_ref, hp_ref, nw_ref, mu_ref, wr_ref, wk_ref, wv_ref,
                     w1_ref, w2_ref, w0_ref, a1_ref, a2_ref, a0_ref, g1_ref, g2_ref,
                     kkw_ref, kaw_ref, hs_ref, hst_ref,
                     r_o, k_o, v_o, kk_o, a_o, lw_o, g_o, ubuf, *, tiles_per_seq):
    tm = h_ref.shape[0]
    i = pl.program_id(0)
    nw = nw_ref[...]
    u = _rms(h_ref[...], nw)
    up = _rms(hp_ref[...], nw)
    seq_start = (i % tiles_per_seq) == 0
    ubuf[8:8 + tm, :] = u
    ubuf[0:8, :] = jnp.where(seq_start, 0.0, up)
    xx = ubuf[7:7 + tm, :] - u

    def mix(j):
        return (u + xx * mu_ref[j:j + 1, :]).astype(BF16)

    r = _dot(mix(0), wr_ref[...])
    k = _dot(mix(2), wk_ref[...])
    v = _dot(mix(3), wv_ref[...])
    wl = w0_ref[...] + _dot(jnp.tanh(_dot(mix(1), w1_ref[...])).astype(BF16), w2_ref[...])
    w_log = -_softplus(-wl) - 0.5
    lw = -jnp.exp(w_log)
    a = _sigmoid(a0_ref[...] + _dot(_dot(mix(4), a1_ref[...]).astype(BF16), a2_ref[...]))
    g = _dot(_sigmoid(_dot(mix(5), g1_ref[...])).astype(BF16), g2_ref[...])
    kk = k * kkw_ref[...]
    ssq = _dot((kk * kk).astype(BF16), hs_ref[...])
    inv = lax.rsqrt(jnp.maximum(ssq, 1e-24))
    kk = kk * _dot_x2(inv, hst_ref[...])
    k = k * (1.0 + (a - 1.0) * kaw_ref[...])
    r_o[...] = r.astype(r_o.dtype)
    k_o[...] = k.astype(k_o.dtype)
    v_o[...] = v.astype(v_o.dtype)
    kk_o[...] = kk.astype(kk_o.dtype)
    a_o[...] = a.astype(a_o.dtype)
    lw_o[...] = lw
    g_o[...] = g.astype(g_o.dtype)


def _rwkv_pre(h2, nw, mu, wr, wk, wv, w1, w2, w0, a1, a2, a0, g1, g2, kkw, kaw, hs, hst, S, tm):
    T, D = h2.shape
    full = lambda arr: pl.BlockSpec(arr.shape, lambda i: (0,) * arr.ndim)
    row = pl.BlockSpec((tm, D), lambda i: (i, 0))
    params = (nw, mu, wr, wk, wv, w1, w2, w0, a1, a2, a0, g1, g2, kkw, kaw, hs, hst)
    bf = jax.ShapeDtypeStruct((T, D), BF16)
    return pl.pallas_call(
        functools.partial(_rwkv_pre_kernel, tiles_per_seq=S // tm),
        grid=(T // tm,),
        in_specs=[row, pl.BlockSpec((8, D), lambda i: (jnp.maximum(i * (tm // 8) - 1, 0), 0))]
        + [full(p) for p in params],
        out_specs=[row] * 7,
        out_shape=[bf, bf, bf, bf, bf, jax.ShapeDtypeStruct((T, D), F32), bf],
        scratch_shapes=[pltpu.VMEM((tm + 8, D), F32)],
        compiler_params=_cparams(("parallel",)),
    )(h2, h2, *params)


def _rwkv_scan_kernel(r_ref, k_ref, v_ref, kk_ref, a_ref, lw_ref, g_ref, lnw_ref, lnb_ref, rk_ref,
                      o_ref, state_ref, *, n_chunks, n_pairs):
    C = RWKV_CHUNK
    N = RWKV_HEAD
    t = pl.program_id(2)

    @pl.when(t == 0)
    def _():
        state_ref[...] = jnp.zeros_like(state_ref)

    row = lax.broadcasted_iota(jnp.int32, (C, C), 0)
    col = lax.broadcasted_iota(jnp.int32, (C, C), 1)
    tri_incl = jnp.where(row >= col, 1.0, 0.0).astype(BF16)
    eye = jnp.where(row == col, 1.0, 0.0).astype(F32)
    grow = lax.broadcasted_iota(jnp.int32, (2 * C, 2 * C), 0)
    gcol = lax.broadcasted_iota(jnp.int32, (2 * C, 2 * C), 1)
    gr, gc = grow % C, gcol % C
    gmask = (gr + grow // C) > gc
    blockdiag = (grow // N) == (gcol // N)
    lane = lax.broadcasted_iota(jnp.int32, (C, LANES), 1)
    head_masks = (lane < N, lane >= N)
    upper_lanes = lane >= C

    m0 = head_masks[0]
    n_levels = int(math.log2(C))
    pairs = range(n_pairs)
    chains = [(p, s) for p in pairs for s in range(2)]

    def head_sum(x):
        s0 = jnp.sum(jnp.where(m0, x, 0.0), axis=-1, keepdims=True)
        s1 = jnp.sum(jnp.where(m0, 0.0, x), axis=-1, keepdims=True)
        return jnp.where(m0, s0, s1)

    def chunk_body(ci, carry):
        rows = pl.ds(pl.multiple_of(ci * C, C), C)
        cum_all = _dot_2x(tri_incl, lw_ref[0, rows, :])
        pre = []
        for p in pairs:
            lanes = slice(p * LANES, (p + 1) * LANES)
            r = r_ref[0, rows, lanes].astype(F32)
            k = k_ref[0, rows, lanes].astype(F32)
            v = v_ref[0, rows, lanes].astype(F32)
            kk = kk_ref[0, rows, lanes].astype(F32)
            a = a_ref[0, rows, lanes].astype(F32)
            lw = lw_ref[0, rows, lanes]
            cum = cum_all[:, lanes]
            wc = cum[C - 1:C, :]
            e_out = jnp.exp(-cum)
            e_end = jnp.exp(wc - cum)
            b = kk * a
            at = -kk * jnp.exp(cum - lw)
            rt = r * jnp.exp(cum)
            rhs = jnp.concatenate([b * e_out, k * e_out], axis=0).astype(BF16)
            kb_end = jnp.concatenate([k * e_end, b * e_end], axis=0).astype(BF16)
            vb = v.astype(BF16)
            pre.append(dict(lanes=lanes, r=r, k=k, v=v, wc=wc, at=at, rt=rt, rhs=rhs, kb_end=kb_end,
                            vb=vb, vv=jnp.concatenate([vb, vb], axis=0)))

        grams = []
        for p, s in chains:
            d, m = pre[p], head_masks[s]
            lhs = jnp.concatenate([jnp.where(m, d["at"], 0.0), jnp.where(m, d["rt"], 0.0)], axis=0)
            grams.append(jnp.where(gmask, _dot_nt(lhs.astype(BF16), d["rhs"]), 0.0))
        tops = [g[0:C, :] for g in grams]
        g_bot = [g[C:2 * C, :].astype(BF16) for g in grams]
        mvs = [_dot(jnp.where(upper_lanes, top, 0.0).astype(BF16), pre[p]["vv"])
               for top, (p, s) in zip(tops, chains)]
        lms = [top[:, 0:C] for top in tops]
        pks = [eye + lm for lm in lms]
        lks = [_dot(lm.astype(BF16), lm.astype(BF16)) for lm in lms]
        for lvl in range(1, n_levels):
            if lvl < n_levels - 1:
                boths = [_dot(jnp.concatenate([pk, lk], axis=0).astype(BF16), lk.astype(BF16))
                         for pk, lk in zip(pks, lks)]
                pks = [pk + bo[0:C, :] for pk, bo in zip(pks, boths)]
                lks = [bo[C:2 * C, :] for bo in boths]
            else:
                pks = [pk + _dot(pk.astype(BF16), lk.astype(BF16)) for pk, lk in zip(pks, lks)]
        tws = [_dot(pk.astype(BF16), jnp.concatenate([mv, pre[p]["at"]], axis=1).astype(BF16))
               for pk, mv, (p, s) in zip(pks, mvs, chains)]

        sts = [state_ref[p] for p in pairs]
        zs = []
        for p in pairs:
            atp = jnp.where(m0, tws[2 * p][:, LANES:2 * LANES], tws[2 * p + 1][:, LANES:2 * LANES])
            zs.append(_dot_nt(jnp.concatenate([atp, pre[p]["rt"]], axis=0).astype(BF16), sts[p].astype(BF16)))
        us = [jnp.where(m0, tws[2 * p][:, 0:LANES], tws[2 * p + 1][:, 0:LANES]) + zs[p][0:C, :] for p in pairs]
        uvs = [jnp.concatenate([us[p].astype(BF16), pre[p]["vb"]], axis=0) for p in pairs]
        yss = [_dot(g_bot[i], uvs[p]) for i, (p, s) in enumerate(chains)]
        upds = [_dot(jnp.concatenate([pre[p]["v"], us[p]], axis=0).T.astype(BF16), pre[p]["kb_end"])
                for p in pairs]
        for p in pairs:
            d = pre[p]
            lanes = d["lanes"]
            state_ref[p] = sts[p] * jnp.exp(d["wc"]) + jnp.where(blockdiag, upds[p], 0.0)
            y = zs[p][C:2 * C, :] + jnp.where(m0, yss[2 * p], yss[2 * p + 1])
            mean = head_sum(y) * (1.0 / N)
            yc = y - mean
            var = head_sum(yc * yc) * (1.0 / N)
            yn = yc * lax.rsqrt(var + RWKV_LN_EPS) * lnw_ref[:, lanes] + lnb_ref[:, lanes]
            bonus = head_sum(d["r"] * d["k"] * rk_ref[:, lanes]) * d["v"]
            out = (yn + bonus) * g_ref[0, rows, lanes].astype(F32)
            o_ref[0, rows, lanes] = out.astype(o_ref.dtype)
        return carry

    lax.fori_loop(0, n_chunks, chunk_body, 0)


def _rwkv_scan(r, k, v, kk, a, lw, g, lnw, lnb, rk, tb, n_pairs):
    B, S, D = r.shape
    W = n_pairs * LANES
    blk = pl.BlockSpec((1, tb, W), lambda b, j, t: (b, t, j))
    par = pl.BlockSpec((1, W), lambda b, j, t: (0, j))
    return pl.pallas_call(
        functools.partial(_rwkv_scan_kernel, n_chunks=tb // RWKV_CHUNK, n_pairs=n_pairs),
        grid=(B, D // W, S // tb),
        in_specs=[blk] * 7 + [par] * 3,
        out_specs=blk,
        out_shape=jax.ShapeDtypeStruct((B, S, D), BF16),
        scratch_shapes=[pltpu.VMEM((n_pairs, LANES, LANES), F32)],
        compiler_params=_cparams(("parallel", "parallel", "arbitrary")),
    )(r, k, v, kk, a, lw, g, lnw, lnb, rk)


def _rwkv_out_kernel(y_ref, w_ref, h_ref, o_ref):
    o_ref[...] = h_ref[...] + _dot(y_ref[...], w_ref[...])


def _rwkv_out(y, wo, h2, tm):
    T, D = h2.shape
    return pl.pallas_call(
        _rwkv_out_kernel,
        grid=(T // tm,),
        in_specs=[pl.BlockSpec((tm, D), lambda i: (i, 0)), pl.BlockSpec((D, D), lambda i: (0, 0)),
                  pl.BlockSpec((tm, D), lambda i: (i, 0))],
        out_specs=pl.BlockSpec((tm, D), lambda i: (i, 0)),
        out_shape=jax.ShapeDtypeStruct((T, D), F32),
        compiler_params=_cparams(("parallel",)),
    )(y, wo, h2)


def _router_kernel(h_ref, nw_ref, wr_ref, br_ref, t_ref, cw_ref):
    tn = _rms(h_ref[...], nw_ref[...])
    t_ref[...] = tn.astype(t_ref.dtype)
    logits = _dot_x3(tn, wr_ref[...]) + br_ref[...]
    lane = lax.broadcasted_iota(jnp.int32, logits.shape, 1).astype(F32)
    neg = jnp.float32(-jnp.inf)
    big = jnp.float32(1 << 20)
    is_g = lane < MOE_GROUPS
    gl = jnp.where(is_g, logits, neg)
    gmax = jnp.max(gl, axis=-1, keepdims=True)
    g_idx = jnp.min(jnp.where(gl == gmax, lane, big), axis=-1, keepdims=True)
    g_gate = 1.0 / jnp.sum(jnp.where(is_g, jnp.exp(logits - gmax), 0.0), axis=-1, keepdims=True)
    lo = MOE_GROUPS + g_idx * MOE_EPG
    sel = (lane >= lo) & (lane < lo + MOE_EPG)
    el = jnp.where(sel, logits, neg)
    m1 = jnp.max(el, axis=-1, keepdims=True)
    i1 = jnp.min(jnp.where(el == m1, lane, big), axis=-1, keepdims=True)
    el2 = jnp.where(lane == i1, neg, el)
    m2 = jnp.max(el2, axis=-1, keepdims=True)
    i2 = jnp.min(jnp.where(el2 == m2, lane, big), axis=-1, keepdims=True)
    e21 = jnp.exp(m2 - m1)
    w1 = g_gate / (1.0 + e21)
    w2 = w1 * e21
    shift = g_idx * MOE_EPG
    cw_ref[...] = jnp.where(lane == 0.0, g_idx,
                            jnp.where(lane == i1 - shift, w1, jnp.where(lane == i2 - shift, w2, 0.0)))


def _router(h2, nw, wr, br, tm):
    T, D = h2.shape
    return pl.pallas_call(
        _router_kernel,
        grid=(T // tm,),
        in_specs=[pl.BlockSpec((tm, D), lambda i: (i, 0)), pl.BlockSpec((1, D), lambda i: (0, 0)),
                  pl.BlockSpec((D, ROUTER_LANES), lambda i: (0, 0)),
                  pl.BlockSpec((1, ROUTER_LANES), lambda i: (0, 0))],
        out_specs=[pl.BlockSpec((tm, D), lambda i: (i, 0)),
                   pl.BlockSpec((tm, ROUTER_LANES), lambda i: (i, 0))],
        out_shape=[jax.ShapeDtypeStruct((T, D), BF16),
                   jax.ShapeDtypeStruct((T, ROUTER_LANES), F32)],
        compiler_params=_cparams(("parallel",)),
    )(h2, nw, wr, br)


def _dot_tn(a, b):
    return lax.dot_general(a, b, (((0,), (0,)), ((), ())), preferred_element_type=F32)


def _experts_kernel(t_ref, cw_ref, h_ref, w13_ref, w2_ref, fnw_ref, o_ref,
                    pt_ref, xs_ref, cws_ref, oacc_ref, meta_ref, *, final_norm):
    tm = t_ref.shape[0]
    cap = pt_ref.shape[1]
    R = MOE_SUB
    F = MOE_D_FF
    g = pl.program_id(1)
    part = pl.program_id(2)
    n_parts = MOE_EPG // MOE_EXPERTS_PER_STEP

    @pl.when((g == 0) & (part == 0))
    def _():
        cw = cw_ref[...]
        lane_t = lax.broadcasted_iota(jnp.int32, (tm, ROUTER_LANES), 1).astype(F32)
        memb = (cw[:, 0:1] == lane_t) & (lane_t < MOE_GROUPS)
        membf = jnp.where(memb, 1.0, 0.0)
        tr = lax.broadcasted_iota(jnp.int32, (tm, tm), 0)
        tc = lax.broadcasted_iota(jnp.int32, (tm, tm), 1)
        before = jnp.where(tr > tc, 1.0, 0.0).astype(BF16)
        ranks = _dot(before, membf.astype(BF16))
        cnt = jnp.sum(membf, axis=0, keepdims=True)
        nsub = jnp.floor((cnt + (R - 1.0)) * (1.0 / R))
        ur = lax.broadcasted_iota(jnp.int32, (ROUTER_LANES, ROUTER_LANES), 0)
        uc = lax.broadcasted_iota(jnp.int32, (ROUTER_LANES, ROUTER_LANES), 1)
        prefix = jnp.where(ur < uc, 1.0, 0.0).astype(BF16)
        base = _dot_x2(jnp.broadcast_to(nsub * R, (8, ROUTER_LANES)), prefix)[0:1, :]
        pos = jnp.sum(jnp.where(memb, ranks + base, 0.0), axis=-1, keepdims=True)
        col = lax.broadcasted_iota(jnp.int32, (tm, cap), 1).astype(F32)
        pt = jnp.where(pos == col, 1.0, 0.0).astype(BF16)
        pt_ref[...] = pt
        xs_ref[...] = _dot_tn(pt, t_ref[...]).astype(BF16)
        cw_hi, cw_lo = _split2(cw)
        cws_ref[...] = _dot_tn(pt, cw_hi) + _dot_tn(pt, cw_lo)
        oacc_ref[...] = jnp.zeros_like(oacc_ref)
        meta_ref[0:1, :] = nsub
        meta_ref[1:2, :] = base

    lane1 = lax.broadcasted_iota(jnp.int32, (1, ROUTER_LANES), 1)
    n_sub = jnp.sum(jnp.where(lane1 == g, meta_ref[0:1, :], 0.0)).astype(jnp.int32)
    row0 = jnp.sum(jnp.where(lane1 == g, meta_ref[1:2, :], 0.0)).astype(jnp.int32)
    lane_r = lax.broadcasted_iota(jnp.int32, (R, ROUTER_LANES), 1)

    def sub_body(s, carry):
        rows = pl.ds(pl.multiple_of(row0 + s * R, R), R)
        xs = xs_ref[rows, :]
        cws = cws_ref[rows, :]
        acc = oacc_ref[rows, :]
        for j in range(MOE_EXPERTS_PER_STEP):
            h13 = _dot(xs, w13_ref[j])
            ce = jnp.sum(jnp.where(lane_r == MOE_GROUPS + part * MOE_EXPERTS_PER_STEP + j, cws, 0.0),
                         axis=-1, keepdims=True)
            he = (_silu(h13[:, 0:F]) * h13[:, F:2 * F] * ce).astype(BF16)
            acc = acc + _dot(he, w2_ref[j])
        oacc_ref[rows, :] = acc
        return carry

    lax.fori_loop(0, n_sub, sub_body, 0)

    @pl.when((g == MOE_GROUPS - 1) & (part == n_parts - 1))
    def _():
        out = h_ref[...] + _dot(pt_ref[...], oacc_ref[...].astype(BF16))
        if final_norm:
            out = _rms(out, fnw_ref[...])
        o_ref[...] = out


def _experts(t, cw, h2, w13, w2, fnw, tm, final_norm):
    T, D = h2.shape
    F = MOE_D_FF
    E = MOE_EXPERTS_PER_STEP
    n_parts = MOE_EPG // E
    cap = tm + MOE_GROUPS * MOE_SUB
    return pl.pallas_call(
        functools.partial(_experts_kernel, final_norm=final_norm),
        grid=(T // tm, MOE_GROUPS, n_parts),
        in_specs=[
            pl.BlockSpec((tm, D), lambda i, g, p: (i, 0)),
            pl.BlockSpec((tm, ROUTER_LANES), lambda i, g, p: (i, 0)),
            pl.BlockSpec((tm, D), lambda i, g, p: (i, 0)),
            pl.BlockSpec((E, D, 2 * F), lambda i, g, p: (g * n_parts + p, 0, 0)),
            pl.BlockSpec((E, F, D), lambda i, g, p: (g * n_parts + p, 0, 0)),
            pl.BlockSpec((1, D), lambda i, g, p: (0, 0)),
        ],
        out_specs=pl.BlockSpec((tm, D), lambda i, g, p: (i, 0)),
        out_shape=jax.ShapeDtypeStruct((T, D), F32),
        scratch_shapes=[
            pltpu.VMEM((tm, cap), BF16),
            pltpu.VMEM((cap, D), BF16),
            pltpu.VMEM((cap, ROUTER_LANES), F32),
            pltpu.VMEM((cap, D), F32),
            pltpu.VMEM((8, ROUTER_LANES), F32),
        ],
        compiler_params=_cparams(("parallel", "arbitrary", "arbitrary")),
    )(t, cw, h2, w13, w2, fnw)


def _moe(h2, nw, wg, bg, we, be, w1, w3, w2, fnw, final_norm, tm):
    D = D_MODEL
    pad = ROUTER_LANES - MOE_GROUPS - MOE_EXPERTS
    wr = jnp.concatenate([wg, we, jnp.zeros((D, pad), F32)], axis=1)
    br = jnp.concatenate([bg, be, jnp.zeros((pad,), F32)]).reshape(1, ROUTER_LANES)
    t, cw = _router(h2, nw.reshape(1, D), wr, br, tm)
    w13 = jnp.concatenate([w1, w3], axis=-1).astype(BF16)
    return _experts(t, cw, h2, w13, w2.astype(BF16), fnw.reshape(1, D), tm, final_norm)


def _pick(n, prefs):
    for p in prefs:
        if n % p == 0:
            return p
    return n


def _ssd_retention_layer(h2, B, S, nw, w_in, conv_w, conv_b, dt_bias, a_log, d_skip, norm_w, w_out):
    D = D_MODEL
    T = B * S
    o_x, o_bc, o_dt, o_q, o_k, o_v, o_g = 1024, 2048, 2560, 2576, 3088, 3600, 4624
    w_main = jnp.concatenate([w_in[:, 0:o_x], w_in[:, o_x:o_bc], w_in[:, o_v:o_g], w_in[:, o_g:o_g + D],
                              w_in[:, o_bc:o_dt], w_in[:, o_q:o_k], w_in[:, o_k:o_v]], axis=1).astype(BF16)
    w_dt = jnp.pad(w_in[:, o_dt:o_q], ((0, 0), (0, LANES - SSD_HEADS))).astype(BF16)
    tm = _pick(T, (1024, 512, 256, 128))
    proj, dt = _inproj(h2, nw.reshape(1, D), w_main, w_dt, tm, 1408)
    proj3 = proj.reshape(B, S, proj.shape[1])
    dt3 = dt.reshape(B, S, LANES)

    head_of_channel = jnp.arange(D) // SSD_HEAD_DIM
    expand = (jnp.arange(LANES)[:, None] == head_of_channel[None, :]).astype(BF16)
    pad16 = lambda x: jnp.pad(x, (0, LANES - SSD_HEADS)).reshape(1, LANES)
    ya = _ssd(proj3, dt3, conv_w[:, :D], conv_b[:D].reshape(1, D), conv_w[:, D:], conv_b[D:].reshape(1, 512),
              pad16(dt_bias), pad16(a_log), expand, jnp.repeat(d_skip, SSD_HEAD_DIM).reshape(1, D),
              norm_w.reshape(1, D))

    C = RET_CHUNK
    half = RET_QK_HEAD // 2
    pos = jnp.arange(S, dtype=F32)
    inv_freq = ROPE_BASE ** (-jnp.arange(half, dtype=F32) / half)
    ang = pos[:, None] * inv_freq[None, :]
    cc = jnp.concatenate([jnp.cos(ang), jnp.cos(ang)], axis=1)
    ss = jnp.concatenate([-jnp.sin(ang), jnp.sin(ang)], axis=1)
    log_gamma = jnp.log(1.0 - 2.0 ** (-5.0 - jnp.arange(RET_HEADS, dtype=F32)))
    idx = jnp.arange(C, dtype=F32)
    diff = idx[:, None] - idx[None, :]
    dmask = jnp.where(diff[None] >= 0, jnp.exp(jnp.maximum(diff, 0.0)[None] * log_gamma[:, None, None]), 0.0)
    qdec = jnp.repeat(jnp.exp((idx[:, None] + 1.0) * log_gamma[None, :]), RET_V_HEAD, axis=1)
    kdec = jnp.repeat(jnp.exp((C - 1.0 - idx[:, None]) * log_gamma[None, :]), RET_QK_HEAD, axis=1)
    cdec = jnp.repeat(jnp.exp(C * log_gamma), RET_V_HEAD).reshape(1, D)
    yb = _retention(proj3, cc, ss, dmask, qdec, kdec, cdec)

    w_out_b = w_out.astype(BF16)
    return _outproj(ya.reshape(T, D), yb.reshape(T, D), w_out_b[:D], w_out_b[D:], h2, tm)


def _rwkv_layer(h2, B, S, nw, mu, w_r, w_k, w_v, w_o, w0, w1, w2, a0, a1, a2, g1, g2, k_k, k_a, r_k, lnx_w, lnx_b):
    D = D_MODEL
    T = B * S
    padc = lambda w, n: jnp.pad(w, ((0, 0), (0, n - w.shape[1]))).astype(BF16)
    padr = lambda w, n: jnp.pad(w, ((0, n - w.shape[0]), (0, 0))).astype(BF16)
    head_of_channel = jnp.arange(D) // RWKV_HEAD
    hs = (head_of_channel[:, None] == jnp.arange(LANES)[None, :]).astype(BF16)
    row = lambda x: x.reshape(1, D)
    tm = _pick(S, (256, 128))
    r, k, v, kk, a, lw, g = _rwkv_pre(
        h2, row(nw), mu, w_r.astype(BF16), w_k.astype(BF16), w_v.astype(BF16),
        padc(w1, RWKV_LORA_PAD), padr(w2, RWKV_LORA_PAD), row(w0),
        padc(a1, RWKV_LORA_PAD), padr(a2, RWKV_LORA_PAD), row(a0),
        padc(g1, RWKV_GATE_PAD), padr(g2, RWKV_GATE_PAD), row(k_k), row(k_a), hs, hs.T, S, tm)
    sh = lambda x: x.reshape(B, S, D)
    tb = _pick(S, (512, 256, 128, 64))
    y = _rwkv_scan(sh(r), sh(k), sh(v), sh(kk), sh(a), sh(lw), sh(g),
                   row(lnx_w), row(lnx_b), r_k.reshape(1, D), tb, 8)
    return _rwkv_out(y.reshape(T, D), w_o.astype(BF16), h2, _pick(T, (1024, 512, 256, 128)))


def kernel(x, norm_mix_w, norm_ffn_w, norm_final_w, w_in_e, ssd_conv_w, ssd_conv_b, ssd_dt_bias, ssd_a_log, ssd_d, ssd_norm_w, w_out_e, rw_mu, rw_wr, rw_wk, rw_wv, rw_wo, rw_w0, rw_w1, rw_w2, rw_a0, rw_a1, rw_a2, rw_g1, rw_g2, rw_kk, rw_ka, rw_rk, rw_lnx_w, rw_lnx_b, moe_wg, moe_bg, moe_we, moe_be, moe_w1, moe_w3, moe_w2):
    B, S, D = x.shape
    T = B * S
    depth = norm_mix_w.shape[0]
    h = x.reshape(T, D)
    tm_moe = _pick(T, (1024, 512, 256, 128))
    for layer in range(depth):
        i = layer // 2
        if layer % 2 == 0:
            h = _ssd_retention_layer(h, B, S, norm_mix_w[layer], w_in_e[i], ssd_conv_w[i], ssd_conv_b[i],
                                     ssd_dt_bias[i], ssd_a_log[i], ssd_d[i], ssd_norm_w[i], w_out_e[i])
        else:
            h = _rwkv_layer(h, B, S, norm_mix_w[layer], rw_mu[i], rw_wr[i], rw_wk[i], rw_wv[i], rw_wo[i],
                            rw_w0[i], rw_w1[i], rw_w2[i], rw_a0[i], rw_a1[i], rw_a2[i], rw_g1[i], rw_g2[i],
                            rw_kk[i], rw_ka[i], rw_rk[i], rw_lnx_w[i], rw_lnx_b[i])
        h = _moe(h, norm_ffn_w[layer], moe_wg[layer], moe_bg[layer], moe_we[layer], moe_be[layer],
                 moe_w1[layer], moe_w3[layer], moe_w2[layer], norm_final_w,
                 final_norm=(layer == depth - 1), tm=tm_moe)
    return h.reshape(B, S, D)
```

```python
import functools
import math

import jax
import jax.numpy as jnp
from jax import lax
from jax.experimental import pallas as pl
from jax.experimental.pallas import tpu as pltpu

F32 = jnp.float32
BF16 = jnp.bfloat16

D_MODEL = 1024
RMS_EPS = 1e-6
SSD_HEADS = 16
SSD_HEAD_DIM = 64
SSD_GROUPS = 2
SSD_STATE = 128
SSD_CONV = 4
SSD_CHUNK = 128
SSD_CONV_TAIL = 16
SSD_NORM_EPS = 1e-5
SSD_GROUP_WIDTH = D_MODEL // SSD_GROUPS
RET_HEADS = 4
RET_QK_HEAD = 128
RET_V_HEAD = 256
RET_CHUNK = 128
ROPE_BASE = 10000.0
RWKV_HEAD = 64
RWKV_HEADS = 16
RWKV_LN_EPS = 64e-5
RWKV_CHUNK = 64
RWKV_LORA_PAD = 128
RWKV_GATE_PAD = 256
RWKV_PRE_SPLIT = 2
MOE_GROUPS = 4
MOE_EPG = 4
MOE_EXPERTS = 16
MOE_D_FF = 512
ROUTER_LANES = 128
MOE_SUB = 128
MOE_EXPERTS_PER_STEP = 2

LANES = 128
VMEM_LIMIT_BYTES = 56 * 1024 * 1024


def _cparams(sem):
    return pltpu.CompilerParams(dimension_semantics=sem, vmem_limit_bytes=VMEM_LIMIT_BYTES)


def _dot(a, b):
    return jnp.dot(a, b, preferred_element_type=F32)


def _dot_nt(a, b):
    return lax.dot_general(a, b, (((1,), (1,)), ((), ())), preferred_element_type=F32)


def _split2(x):
    hi = x.astype(BF16)
    lo = (x - hi.astype(F32)).astype(BF16)
    return hi, lo


def _dot_x2(x, w_exact):
    hi, lo = _split2(x)
    return _dot(hi, w_exact) + _dot(lo, w_exact)


def _dot_2x(w_exact, x):
    hi, lo = _split2(x)
    return _dot(w_exact, hi) + _dot(w_exact, lo)


def _dot_x3(x, w):
    xh, xl = _split2(x)
    wh, wl = _split2(w)
    return _dot(xh, wh) + _dot(xl, wh) + _dot(xh, wl)


def _sigmoid(x):
    return 0.5 * jnp.tanh(0.5 * x) + 0.5


def _silu(x):
    hx = 0.5 * x
    return hx * jnp.tanh(hx) + hx


def _softplus(x):
    return jnp.maximum(x, 0.0) + jnp.log(1.0 + jnp.exp(-jnp.abs(x)))


def _rms(x, w, eps=RMS_EPS):
    return x * lax.rsqrt(jnp.mean(x * x, axis=-1, keepdims=True) + eps) * w


def _inproj_kernel(x_ref, nw_ref, w_ref, wdt_ref, o_ref, dt_ref, u_ref):
    @pl.when(pl.program_id(1) == 0)
    def _():
        ub = _rms(x_ref[...], nw_ref[...]).astype(BF16)
        u_ref[...] = ub
        dt_ref[...] = _dot(ub, wdt_ref[...])

    o_ref[...] = _dot(u_ref[...], w_ref[...]).astype(o_ref.dtype)


def _inproj(h2, nw, w_main, w_dt, tm, tn):
    T, D = h2.shape
    N = w_main.shape[1]
    return pl.pallas_call(
        _inproj_kernel,
        grid=(T // tm, N // tn),
        in_specs=[
            pl.BlockSpec((tm, D), lambda i, j: (i, 0)),
            pl.BlockSpec((1, D), lambda i, j: (0, 0)),
            pl.BlockSpec((D, tn), lambda i, j: (0, j)),
            pl.BlockSpec((D, LANES), lambda i, j: (0, 0)),
        ],
        out_specs=[
            pl.BlockSpec((tm, tn), lambda i, j: (i, j)),
            pl.BlockSpec((tm, LANES), lambda i, j: (i, 0)),
        ],
        out_shape=[
            jax.ShapeDtypeStruct((T, N), BF16),
            jax.ShapeDtypeStruct((T, LANES), F32),
        ],
        scratch_shapes=[pltpu.VMEM((tm, D), BF16)],
        compiler_params=_cparams(("parallel", "arbitrary")),
    )(h2, nw, w_main, w_dt)


def _ssd_kernel(z_ref, x_ref, bc_ref, dt_ref, cwx_ref, cbx_ref, cwb_ref, cbb_ref,
                dtb_ref, alog_ref, e_ref, dskip_ref, nw_ref, o_ref,
                state_ref, xtail, bctail, ybuf):
    L = SSD_CHUNK
    c = pl.program_id(1)

    @pl.when(c == 0)
    def _():
        state_ref[...] = jnp.zeros_like(state_ref)
        xtail[...] = jnp.zeros_like(xtail)
        bctail[...] = jnp.zeros_like(bctail)

    TAIL = xtail.shape[0]
    srow = lax.broadcasted_iota(jnp.int32, (SSD_CONV * L, TAIL + L), 0)
    scol = lax.broadcasted_iota(jnp.int32, (SSD_CONV * L, TAIL + L), 1)
    shift = jnp.where(scol == (srow % L) + (srow // L) + (TAIL - SSD_CONV + 1), 1.0, 0.0).astype(BF16)

    def conv(cur_ref, tail, w_ref, b_ref):
        cur = cur_ref[0]
        taps = _dot(shift, jnp.concatenate([tail[...], cur], axis=0))
        acc = b_ref[...] + w_ref[0:1, :] * taps[0:L, :]
        for k in range(1, SSD_CONV):
            acc = acc + w_ref[k:k + 1, :] * taps[k * L:(k + 1) * L, :]
        tail[...] = cur[L - TAIL:L, :]
        return _silu(acc)

    xs = conv(x_ref, xtail, cwx_ref, cbx_ref)
    bc = conv(bc_ref, bctail, cwb_ref, cbb_ref)

    dt = _softplus(dt_ref[0] + dtb_ref[...])
    a = dt * (-jnp.exp(alog_ref[...]))
    row = lax.broadcasted_iota(jnp.int32, (L, L), 0)
    col = lax.broadcasted_iota(jnp.int32, (L, L), 1)
    causal = row >= col
    tri = jnp.where(causal, 1.0, 0.0).astype(BF16)
    acum = _dot_2x(tri, a)
    acum_t = acum.T
    e = e_ref[...]
    dt_full = _dot_x2(dt, e)
    acum_full = _dot_x2(acum, e)
    xdt = xs * dt_full
    alast_full = acum_full[L - 1:L, :]
    decay_in = jnp.exp(acum_full)
    xdt_end = (xdt * jnp.exp(alast_full - acum_full)).astype(BF16)
    xdt_b = xdt.astype(BF16)
    lane = lax.broadcasted_iota(jnp.int32, (L, LANES), 1)
    first_head = lane < SSD_HEAD_DIM

    GW = SSD_GROUP_WIDTH
    for g in range(SSD_GROUPS):
        bg = bc[:, g * SSD_STATE:(g + 1) * SSD_STATE]
        cg = bc[:, (SSD_GROUPS + g) * SSD_STATE:(SSD_GROUPS + g + 1) * SSD_STATE].astype(BF16)
        cb = _dot_nt(cg, bg.astype(BF16))
        st = state_ref[g]
        y_off = _dot(cg, st.astype(BF16)) * decay_in[:, g * GW:(g + 1) * GW]
        for p in range(GW // LANES):
            xp = xdt_b[:, g * GW + p * LANES:g * GW + (p + 1) * LANES]
            ys = []
            for s in range(2):
                hd = g * (SSD_HEADS // SSD_GROUPS) + 2 * p + s
                seg = acum[:, hd:hd + 1] - acum_t[hd:hd + 1, :]
                dec = jnp.where(causal, jnp.exp(seg), 0.0)
                ys.append(_dot((cb * dec).astype(BF16), xp))
            yd = jnp.where(first_head, ys[0], ys[1])
            lo = g * GW + p * LANES
            ybuf[:, lo:lo + LANES] = yd + y_off[:, p * LANES:(p + 1) * LANES]
        bg_t = bg.T.astype(BF16)
        state_ref[g] = st * jnp.exp(alast_full[:, g * GW:(g + 1) * GW]) + _dot(
            bg_t, xdt_end[:, g * GW:(g + 1) * GW])

    y = ybuf[...] + xs * dskip_ref[...]
    y = y * _silu(z_ref[0].astype(F32))
    for g in range(SSD_GROUPS):
        yg = y[:, g * GW:(g + 1) * GW]
        yg = yg * lax.rsqrt(jnp.mean(yg * yg, axis=-1, keepdims=True) + SSD_NORM_EPS)
        o_ref[0, :, g * GW:(g + 1) * GW] = (yg * nw_ref[:, g * GW:(g + 1) * GW]).astype(o_ref.dtype)


def _ssd(proj3, dt3, cwx, cbx, cwb, cbb, dtb, alog, expand, dskip, nw):
    B, S, _ = proj3.shape
    L = SSD_CHUNK
    D = D_MODEL
    full = lambda shape: pl.BlockSpec(shape, lambda b, c: (0,) * len(shape))
    return pl.pallas_call(
        _ssd_kernel,
        grid=(B, S // L),
        in_specs=[
            pl.BlockSpec((1, L, D), lambda b, c: (b, c, 0)),
            pl.BlockSpec((1, L, D), lambda b, c: (b, c, 1)),
            pl.BlockSpec((1, L, 512), lambda b, c: (b, c, 8)),
            pl.BlockSpec((1, L, LANES), lambda b, c: (b, c, 0)),
            full((SSD_CONV, D)), full((1, D)), full((SSD_CONV, 512)), full((1, 512)),
            full((1, LANES)), full((1, LANES)), full((LANES, D)), full((1, D)), full((1, D)),
        ],
        out_specs=pl.BlockSpec((1, L, D), lambda b, c: (b, c, 0)),
        out_shape=jax.ShapeDtypeStruct((B, S, D), BF16),
        scratch_shapes=[
            pltpu.VMEM((SSD_GROUPS, SSD_STATE, SSD_GROUP_WIDTH), F32),
            pltpu.VMEM((SSD_CONV_TAIL, D), BF16),
            pltpu.VMEM((SSD_CONV_TAIL, 512), BF16),
            pltpu.VMEM((L, D), F32),
        ],
        compiler_params=_cparams(("parallel", "arbitrary")),
    )(proj3, proj3, proj3, dt3, cwx, cbx, cwb, cbb, dtb, alog, expand, dskip, nw)


def _ret_kernel(q_ref, k_ref, v_ref, g_ref, cc_ref, ss_ref, dmask_ref, qdec_ref, kdec_ref,
                cdec_ref, o_ref, r_ref):
    c = pl.program_id(1)

    @pl.when(c == 0)
    def _():
        r_ref[...] = jnp.zeros_like(r_ref)

    cc = cc_ref[...]
    ss = ss_ref[...]
    dk, dv = RET_QK_HEAD, RET_V_HEAD
    half = dk // 2

    def rope(x):
        return x * cc + pltpu.roll(x, half, 1) * ss

    heads = range(RET_HEADS)
    qs = [rope(q_ref[0, :, hd * dk:(hd + 1) * dk].astype(F32)) for hd in heads]
    ks = [rope(k_ref[0, :, hd * dk:(hd + 1) * dk].astype(F32)) * (dk ** -0.5) for hd in heads]
    vs = [v_ref[0, :, hd * dv:(hd + 1) * dv] for hd in heads]
    qbs = [q.astype(BF16) for q in qs]
    ss = [_dot_nt(qbs[hd], ks[hd].astype(BF16)) * dmask_ref[hd] for hd in heads]
    r_olds = [r_ref[hd] for hd in heads]
    cross = [_dot(qbs[hd], r_olds[hd].astype(BF16)) for hd in heads]
    kd_ts = [(ks[hd] * kdec_ref[:, hd * dk:(hd + 1) * dk]).T.astype(BF16) for hd in heads]
    upds = [_dot(kd_ts[hd], vs[hd]) for hd in heads]
    inner = [_dot(ss[hd].astype(BF16), vs[hd]) for hd in heads]
    for hd in heads:
        r_ref[hd] = r_olds[hd] * cdec_ref[:, hd * dv:(hd + 1) * dv] + upds[hd]
        y = inner[hd] + cross[hd] * qdec_ref[:, hd * dv:(hd + 1) * dv]
        y = y * lax.rsqrt(jnp.mean(y * y, axis=-1, keepdims=True) + RMS_EPS)
        y = y * _silu(g_ref[0, :, hd * dv:(hd + 1) * dv].astype(F32))
        o_ref[0, :, hd * dv:(hd + 1) * dv] = y.astype(o_ref.dtype)


def _retention(proj3, cc, ss, dmask, qdec, kdec, cdec):
    B, S, _ = proj3.shape
    C = RET_CHUNK
    D = D_MODEL
    full = lambda shape: pl.BlockSpec(shape, lambda b, c: (0,) * len(shape))
    return pl.pallas_call(
        _ret_kernel,
        grid=(B, S // C),
        in_specs=[
            pl.BlockSpec((1, C, 512), lambda b, c: (b, c, 9)),
            pl.BlockSpec((1, C, 512), lambda b, c: (b, c, 10)),
            pl.BlockSpec((1, C, D), lambda b, c: (b, c, 2)),
            pl.BlockSpec((1, C, D), lambda b, c: (b, c, 3)),
            pl.BlockSpec((C, RET_QK_HEAD), lambda b, c: (c, 0)),
            pl.BlockSpec((C, RET_QK_HEAD), lambda b, c: (c, 0)),
            full((RET_HEADS, C, C)), full((C, D)), full((C, 512)), full((1, D)),
        ],
        out_specs=pl.BlockSpec((1, C, D), lambda b, c: (b, c, 0)),
        out_shape=jax.ShapeDtypeStruct((B, S, D), BF16),
        scratch_shapes=[pltpu.VMEM((RET_HEADS, RET_QK_HEAD, RET_V_HEAD), F32)],
        compiler_params=_cparams(("parallel", "arbitrary")),
    )(proj3, proj3, proj3, proj3, cc, ss, dmask, qdec, kdec, cdec)


def _outproj_kernel(ya_ref, yb_ref, wa_ref, wb_ref, h_ref, o_ref):
    o_ref[...] = h_ref[...] + _dot(ya_ref[...], wa_ref[...]) + _dot(yb_ref[...], wb_ref[...])


def _outproj(ya, yb, wa, wb, h2, tm):
    T, D = h2.shape
    return pl.pallas_call(
        _outproj_kernel,
        grid=(T // tm,),
        in_specs=[
            pl.BlockSpec((tm, D), lambda i: (i, 0)),
            pl.BlockSpec((tm, D), lambda i: (i, 0)),
            pl.BlockSpec((D, D), lambda i: (0, 0)),
            pl.BlockSpec((D, D), lambda i: (0, 0)),
            pl.BlockSpec((tm, D), lambda i: (i, 0)),
        ],
        out_specs=pl.BlockSpec((tm, D), lambda i: (i, 0)),
        out_shape=jax.ShapeDtypeStruct((T, D), F32),
        compiler_params=_cparams(("parallel",)),
    )(ya, yb, wa, wb, h2)


def _rwkv_pre_kernel(h_ref, hp_ref, nw_ref, mu_ref, wr_ref, wk_ref, wv_ref,
                     w1_ref, w2_ref, w0_ref, a1_ref, a2_ref, a0_ref, g1_ref, g2_ref,
                     kkw_ref, kaw_ref, hs_ref, hst_ref,
                     r_o, k_o, v_o, kk_o, a_o, lw_o, g_o, ubuf, xxbuf, *, tiles_per_seq):
    tm = h_ref.shape[0]
    i = pl.program_id(0)
    nw = nw_ref[...]
    u = _rms(h_ref[...], nw)
    up = _rms(hp_ref[...], nw)
    seq_start = (i % tiles_per_seq) == 0
    ubuf[8:8 + tm, :] = u
    ubuf[0:8, :] = jnp.where(seq_start, 0.0, up)
    xxbuf[...] = ubuf[7:7 + tm, :] - u

    def mix(j, lo, n):
        return (ubuf[8 + lo:8 + lo + n, :] + xxbuf[lo:lo + n, :] * mu_ref[j:j + 1, :]).astype(BF16)

    n = tm // RWKV_PRE_SPLIT
    parts = [s * n for s in range(RWKV_PRE_SPLIT)]
    first = []
    for lo in parts:
        first.append(dict(
            r=_dot(mix(0, lo, n), wr_ref[...]), k=_dot(mix(2, lo, n), wk_ref[...]),
            v=_dot(mix(3, lo, n), wv_ref[...]), w1=_dot(mix(1, lo, n), w1_ref[...]),
            a1=_dot(mix(4, lo, n), a1_ref[...]), g1=_dot(mix(5, lo, n), g1_ref[...])))
    second = []
    for f in first:
        kk = f["k"] * kkw_ref[...]
        second.append(dict(
            wl=w0_ref[...] + _dot(jnp.tanh(f["w1"]).astype(BF16), w2_ref[...]),
            al=a0_ref[...] + _dot(f["a1"].astype(BF16), a2_ref[...]),
            g=_dot(_sigmoid(f["g1"]).astype(BF16), g2_ref[...]),
            kk=kk, ssq=_dot((kk * kk).astype(BF16), hs_ref[...])))
    for lo, f, s in zip(parts, first, second):
        rows = slice(lo, lo + n)
        w_log = -_softplus(-s["wl"]) - 0.5
        a = _sigmoid(s["al"])
        inv = lax.rsqrt(jnp.maximum(s["ssq"], 1e-24))
        r_o[rows, :] = f["r"].astype(r_o.dtype)
        k_o[rows, :] = (f["k"] * (1.0 + (a - 1.0) * kaw_ref[...])).astype(k_o.dtype)
        v_o[rows, :] = f["v"].astype(v_o.dtype)
        kk_o[rows, :] = (s["kk"] * _dot_x2(inv, hst_ref[...])).astype(kk_o.dtype)
        a_o[rows, :] = a.astype(a_o.dtype)
        lw_o[rows, :] = -jnp.exp(w_log)
        g_o[rows, :] = s["g"].astype(g_o.dtype)


def _rwkv_pre(h2, nw, mu, wr, wk, wv, w1, w2, w0, a1, a2, a0, g1, g2, kkw, kaw, hs, hst, S, tm):
    T, D = h2.shape
    full = lambda arr: pl.BlockSpec(arr.shape, lambda i: (0,) * arr.ndim)
    row = pl.BlockSpec((tm, D), lambda i: (i, 0))
    params = (nw, mu, wr, wk, wv, w1, w2, w0, a1, a2, a0, g1, g2, kkw, kaw, hs, hst)
    bf = jax.ShapeDtypeStruct((T, D), BF16)
    return pl.pallas_call(
        functools.partial(_rwkv_pre_kernel, tiles_per_seq=S // tm),
        grid=(T // tm,),
        in_specs=[row, pl.BlockSpec((8, D), lambda i: (jnp.maximum(i * (tm // 8) - 1, 0), 0))]
        + [full(p) for p in params],
        out_specs=[row] * 7,
        out_shape=[bf, bf, bf, bf, bf, jax.ShapeDtypeStruct((T, D), F32), bf],
        scratch_shapes=[pltpu.VMEM((tm + 8, D), F32), pltpu.VMEM((tm, D), F32)],
        compiler_params=_cparams(("parallel",)),
    )(h2, h2, *params)


def _rwkv_scan_kernel(r_ref, k_ref, v_ref, kk_ref, a_ref, lw_ref, g_ref, lnw_ref, lnb_ref, rk_ref,
                      o_ref, state_ref, *, n_chunks, n_pairs):
    C = RWKV_CHUNK
    N = RWKV_HEAD
    t = pl.program_id(2)

    @pl.when(t == 0)
    def _():
        state_ref[...] = jnp.zeros_like(state_ref)

    row = lax.broadcasted_iota(jnp.int32, (C, C), 0)
    col = lax.broadcasted_iota(jnp.int32, (C, C), 1)
    tri_incl = jnp.where(row >= col, 1.0, 0.0).astype(BF16)
    eye = jnp.where(row == col, 1.0, 0.0).astype(F32)
    grow = lax.broadcasted_iota(jnp.int32, (2 * C, 2 * C), 0)
    gcol = lax.broadcasted_iota(jnp.int32, (2 * C, 2 * C), 1)
    gr, gc = grow % C, gcol % C
    gmask = (gr + grow // C) > gc
    blockdiag = (grow // N) == (gcol // N)
    lane = lax.broadcasted_iota(jnp.int32, (C, LANES), 1)
    head_masks = (lane < N, lane >= N)
    upper_lanes = lane >= C

    m0 = head_masks[0]
    n_levels = int(math.log2(C))
    pairs = range(n_pairs)
    chains = [(p, s) for p in pairs for s in range(2)]

    def head_sum(x):
        s0 = jnp.sum(jnp.where(m0, x, 0.0), axis=-1, keepdims=True)
        s1 = jnp.sum(jnp.where(m0, 0.0, x), axis=-1, keepdims=True)
        return jnp.where(m0, s0, s1)

    def chunk_body(ci, carry):
        rows = pl.ds(pl.multiple_of(ci * C, C), C)
        cum_all = _dot_2x(tri_incl, lw_ref[0, rows, :])
        pre = []
        for p in pairs:
            lanes = slice(p * LANES, (p + 1) * LANES)
            r = r_ref[0, rows, lanes].astype(F32)
            k = k_ref[0, rows, lanes].astype(F32)
            v = v_ref[0, rows, lanes].astype(F32)
            kk = kk_ref[0, rows, lanes].astype(F32)
            a = a_ref[0, rows, lanes].astype(F32)
            lw = lw_ref[0, rows, lanes]
            cum = cum_all[:, lanes]
            wc = cum[C - 1:C, :]
            e_out = jnp.exp(-cum)
            e_end = jnp.exp(wc - cum)
            b = kk * a
            at = -kk * jnp.exp(cum - lw)
            rt = r * jnp.exp(cum)
            rhs = jnp.concatenate([b * e_out, k * e_out], axis=0).astype(BF16)
            kb_end = jnp.concatenate([k * e_end, b * e_end], axis=0).astype(BF16)
            vb = v.astype(BF16)
            pre.append(dict(lanes=lanes, r=r, k=k, v=v, wc=wc, at=at, rt=rt, rhs=rhs, kb_end=kb_end,
                            vb=vb, vv=jnp.concatenate([vb, vb], axis=0)))

        grams = []
        for p, s in chains:
            d, m = pre[p], head_masks[s]
            lhs = jnp.concatenate([jnp.where(m, d["at"], 0.0), jnp.where(m, d["rt"], 0.0)], axis=0)
            grams.append(jnp.where(gmask, _dot_nt(lhs.astype(BF16), d["rhs"]), 0.0))
        tops = [g[0:C, :] for g in grams]
        g_bot = [g[C:2 * C, :].astype(BF16) for g in grams]
        mvs = [_dot(jnp.where(upper_lanes, top, 0.0).astype(BF16), pre[p]["vv"])
               for top, (p, s) in zip(tops, chains)]
        lms = [top[:, 0:C] for top in tops]
        pks = [eye + lm for lm in lms]
        lks = [_dot(lm.astype(BF16), lm.astype(BF16)) for lm in lms]
        for lvl in range(1, n_levels):
            if lvl < n_levels - 1:
                boths = [_dot(jnp.concatenate([pk, lk], axis=0).astype(BF16), lk.astype(BF16))
                         for pk, lk in zip(pks, lks)]
                pks = [pk + bo[0:C, :] for pk, bo in zip(pks, boths)]
                lks = [bo[C:2 * C, :] for bo in boths]
            else:
                pks = [pk + _dot(pk.astype(BF16), lk.astype(BF16)) for pk, lk in zip(pks, lks)]
        tws = [_dot(pk.astype(BF16), jnp.concatenate([mv, pre[p]["at"]], axis=1).astype(BF16))
               for pk, mv, (p, s) in zip(pks, mvs, chains)]

        sts = [state_ref[p] for p in pairs]
        zs = []
        for p in pairs:
            atp = jnp.where(m0, tws[2 * p][:, LANES:2 * LANES], tws[2 * p + 1][:, LANES:2 * LANES])
            zs.append(_dot_nt(jnp.concatenate([atp, pre[p]["rt"]], axis=0).astype(BF16), sts[p].astype(BF16)))
        us = [jnp.where(m0, tws[2 * p][:, 0:LANES], tws[2 * p + 1][:, 0:LANES]) + zs[p][0:C, :] for p in pairs]
        uvs = [jnp.concatenate([us[p].astype(BF16), pre[p]["vb"]], axis=0) for p in pairs]
        yss = [_dot(g_bot[i], uvs[p]) for i, (p, s) in enumerate(chains)]
        upds = [_dot(jnp.concatenate([pre[p]["v"], us[p]], axis=0).T.astype(BF16), pre[p]["kb_end"])
                for p in pairs]
        for p in pairs:
            d = pre[p]
            lanes = d["lanes"]
            state_ref[p] = sts[p] * jnp.exp(d["wc"]) + jnp.where(blockdiag, upds[p], 0.0)
            y = zs[p][C:2 * C, :] + jnp.where(m0, yss[2 * p], yss[2 * p + 1])
            mean = head_sum(y) * (1.0 / N)
            yc = y - mean
            var = head_sum(yc * yc) * (1.0 / N)
            yn = yc * lax.rsqrt(var + RWKV_LN_EPS) * lnw_ref[:, lanes] + lnb_ref[:, lanes]
            bonus = head_sum(d["r"] * d["k"] * rk_ref[:, lanes]) * d["v"]
            out = (yn + bonus) * g_ref[0, rows, lanes].astype(F32)
            o_ref[0, rows, lanes] = out.astype(o_ref.dtype)
        return carry

    lax.fori_loop(0, n_chunks, chunk_body, 0)


def _rwkv_scan(r, k, v, kk, a, lw, g, lnw, lnb, rk, tb, n_pairs):
    B, S, D = r.shape
    W = n_pairs * LANES
    blk = pl.BlockSpec((1, tb, W), lambda b, j, t: (b, t, j))
    par = pl.BlockSpec((1, W), lambda b, j, t: (0, j))
    return pl.pallas_call(
        functools.partial(_rwkv_scan_kernel, n_chunks=tb // RWKV_CHUNK, n_pairs=n_pairs),
        grid=(B, D // W, S // tb),
        in_specs=[blk] * 7 + [par] * 3,
        out_specs=blk,
        out_shape=jax.ShapeDtypeStruct((B, S, D), BF16),
        scratch_shapes=[pltpu.VMEM((n_pairs, LANES, LANES), F32)],
        compiler_params=_cparams(("parallel", "parallel", "arbitrary")),
    )(r, k, v, kk, a, lw, g, lnw, lnb, rk)


def _rwkv_out_kernel(y_ref, w_ref, h_ref, o_ref):
    o_ref[...] = h_ref[...] + _dot(y_ref[...], w_ref[...])


def _rwkv_out(y, wo, h2, tm):
    T, D = h2.shape
    return pl.pallas_call(
        _rwkv_out_kernel,
        grid=(T // tm,),
        in_specs=[pl.BlockSpec((tm, D), lambda i: (i, 0)), pl.BlockSpec((D, D), lambda i: (0, 0)),
                  pl.BlockSpec((tm, D), lambda i: (i, 0))],
        out_specs=pl.BlockSpec((tm, D), lambda i: (i, 0)),
        out_shape=jax.ShapeDtypeStruct((T, D), F32),
        compiler_params=_cparams(("parallel",)),
    )(y, wo, h2)


def _router_kernel(h_ref, nw_ref, wr_ref, br_ref, t_ref, cw_ref):
    tn = _rms(h_ref[...], nw_ref[...])
    t_ref[...] = tn.astype(t_ref.dtype)
    logits = _dot_x3(tn, wr_ref[...]) + br_ref[...]
    lane = lax.broadcasted_iota(jnp.int32, logits.shape, 1).astype(F32)
    neg = jnp.float32(-jnp.inf)
    big = jnp.float32(1 << 20)
    is_g = lane < MOE_GROUPS
    gl = jnp.where(is_g, logits, neg)
    gmax = jnp.max(gl, axis=-1, keepdims=True)
    g_idx = jnp.min(jnp.where(gl == gmax, lane, big), axis=-1, keepdims=True)
    g_gate = 1.0 / jnp.sum(jnp.where(is_g, jnp.exp(logits - gmax), 0.0), axis=-1, keepdims=True)
    lo = MOE_GROUPS + g_idx * MOE_EPG
    sel = (lane >= lo) & (lane < lo + MOE_EPG)
    el = jnp.where(sel, logits, neg)
    m1 = jnp.max(el, axis=-1, keepdims=True)
    i1 = jnp.min(jnp.where(el == m1, lane, big), axis=-1, keepdims=True)
    el2 = jnp.where(lane == i1, neg, el)
    m2 = jnp.max(el2, axis=-1, keepdims=True)
    i2 = jnp.min(jnp.where(el2 == m2, lane, big), axis=-1, keepdims=True)
    e21 = jnp.exp(m2 - m1)
    w1 = g_gate / (1.0 + e21)
    w2 = w1 * e21
    shift = g_idx * MOE_EPG
    cw_ref[...] = jnp.where(lane == 0.0, g_idx,
                            jnp.where(lane == i1 - shift, w1, jnp.where(lane == i2 - shift, w2, 0.0)))


def _router(h2, nw, wr, br, tm):
    T, D = h2.shape
    return pl.pallas_call(
        _router_kernel,
        grid=(T // tm,),
        in_specs=[pl.BlockSpec((tm, D), lambda i: (i, 0)), pl.BlockSpec((1, D), lambda i: (0, 0)),
                  pl.BlockSpec((D, ROUTER_LANES), lambda i: (0, 0)),
                  pl.BlockSpec((1, ROUTER_LANES), lambda i: (0, 0))],
        out_specs=[pl.BlockSpec((tm, D), lambda i: (i, 0)),
                   pl.BlockSpec((tm, ROUTER_LANES), lambda i: (i, 0))],
        out_shape=[jax.ShapeDtypeStruct((T, D), BF16),
                   jax.ShapeDtypeStruct((T, ROUTER_LANES), F32)],
        compiler_params=_cparams(("parallel",)),
    )(h2, nw, wr, br)


def _dot_tn(a, b):
    return lax.dot_general(a, b, (((0,), (0,)), ((), ())), preferred_element_type=F32)


def _experts_kernel(t_ref, cw_ref, h_ref, w13_ref, w2_ref, fnw_ref, o_ref,
                    pt_ref, xs_ref, cws_ref, oacc_ref, meta_ref, *, final_norm):
    tm = t_ref.shape[0]
    cap = pt_ref.shape[1]
    R = MOE_SUB
    F = MOE_D_FF
    g = pl.program_id(1)
    part = pl.program_id(2)
    n_parts = MOE_EPG // MOE_EXPERTS_PER_STEP

    @pl.when((g == 0) & (part == 0))
    def _():
        cw = cw_ref[...]
        lane_t = lax.broadcasted_iota(jnp.int32, (tm, ROUTER_LANES), 1).astype(F32)
        memb = (cw[:, 0:1] == lane_t) & (lane_t < MOE_GROUPS)
        membf = jnp.where(memb, 1.0, 0.0)
        tr = lax.broadcasted_iota(jnp.int32, (tm, tm), 0)
        tc = lax.broadcasted_iota(jnp.int32, (tm, tm), 1)
        before = jnp.where(tr > tc, 1.0, 0.0).astype(BF16)
        ranks = _dot(before, membf.astype(BF16))
        cnt = jnp.sum(membf, axis=0, keepdims=True)
        nsub = jnp.floor((cnt + (R - 1.0)) * (1.0 / R))
        ur = lax.broadcasted_iota(jnp.int32, (ROUTER_LANES, ROUTER_LANES), 0)
        uc = lax.broadcasted_iota(jnp.int32, (ROUTER_LANES, ROUTER_LANES), 1)
        prefix = jnp.where(ur < uc, 1.0, 0.0).astype(BF16)
        base = _dot_x2(jnp.broadcast_to(nsub * R, (8, ROUTER_LANES)), prefix)[0:1, :]
        pos = jnp.sum(jnp.where(memb, ranks + base, 0.0), axis=-1, keepdims=True)
        col = lax.broadcasted_iota(jnp.int32, (tm, cap), 1).astype(F32)
        pt = jnp.where(pos == col, 1.0, 0.0).astype(BF16)
        pt_ref[...] = pt
        cw_hi, cw_lo = _split2(cw)
        D = t_ref.shape[1]
        gathered = _dot_tn(pt, jnp.concatenate([t_ref[...], cw_hi, cw_lo], axis=1))
        xs_ref[...] = gathered[:, 0:D].astype(BF16)
        cws_ref[...] = gathered[:, D:D + ROUTER_LANES] + gathered[:, D + ROUTER_LANES:D + 2 * ROUTER_LANES]
        oacc_ref[...] = jnp.zeros_like(oacc_ref)
        meta_ref[0:1, :] = nsub
        meta_ref[1:2, :] = base

    lane1 = lax.broadcasted_iota(jnp.int32, (1, ROUTER_LANES), 1)
    n_sub = jnp.sum(jnp.where(lane1 == g, meta_ref[0:1, :], 0.0)).astype(jnp.int32)
    row0 = jnp.sum(jnp.where(lane1 == g, meta_ref[1:2, :], 0.0)).astype(jnp.int32)
    lane_r = lax.broadcasted_iota(jnp.int32, (R, ROUTER_LANES), 1)

    def sub_body(s, carry):
        rows = pl.ds(pl.multiple_of(row0 + s * R, R), R)
        xs = xs_ref[rows, :]
        cws = cws_ref[rows, :]
        acc = oacc_ref[rows, :]
        for j in range(MOE_EXPERTS_PER_STEP):
            h13 = _dot(xs, w13_ref[j])
            ce = jnp.sum(jnp.where(lane_r == MOE_GROUPS + part * MOE_EXPERTS_PER_STEP + j, cws, 0.0),
                         axis=-1, keepdims=True)
            he = (_silu(h13[:, 0:F]) * h13[:, F:2 * F] * ce).astype(BF16)
            acc = acc + _dot(he, w2_ref[j])
        oacc_ref[rows, :] = acc
        return carry

    lax.fori_loop(0, n_sub, sub_body, 0)

    @pl.when((g == MOE_GROUPS - 1) & (part == n_parts - 1))
    def _():
        out = h_ref[...] + _dot(pt_ref[...], oacc_ref[...].astype(BF16))
        if final_norm:
            out = _rms(out, fnw_ref[...])
        o_ref[...] = out


def _experts(t, cw, h2, w13, w2, fnw, tm, final_norm):
    T, D = h2.shape
    F = MOE_D_FF
    E = MOE_EXPERTS_PER_STEP
    n_parts = MOE_EPG // E
    cap = tm + MOE_GROUPS * MOE_SUB
    return pl.pallas_call(
        functools.partial(_experts_kernel, final_norm=final_norm),
        grid=(T // tm, MOE_GROUPS, n_parts),
        in_specs=[
            pl.BlockSpec((tm, D), lambda i, g, p: (i, 0)),
            pl.BlockSpec((tm, ROUTER_LANES), lambda i, g, p: (i, 0)),
            pl.BlockSpec((tm, D), lambda i, g, p: (i, 0)),
            pl.BlockSpec((E, D, 2 * F), lambda i, g, p: (g * n_parts + p, 0, 0)),
            pl.BlockSpec((E, F, D), lambda i, g, p: (g * n_parts + p, 0, 0)),
            pl.BlockSpec((1, D), lambda i, g, p: (0, 0)),
        ],
        out_specs=pl.BlockSpec((tm, D), lambda i, g, p: (i, 0)),
        out_shape=jax.ShapeDtypeStruct((T, D), F32),
        scratch_shapes=[
            pltpu.VMEM((tm, cap), BF16),
            pltpu.VMEM((cap, D), BF16),
            pltpu.VMEM((cap, ROUTER_LANES), F32),
            pltpu.VMEM((cap, D), F32),
            pltpu.VMEM((8, ROUTER_LANES), F32),
        ],
        compiler_params=_cparams(("parallel", "arbitrary", "arbitrary")),
    )(t, cw, h2, w13, w2, fnw)


def _moe(h2, nw, wg, bg, we, be, w1, w3, w2, fnw, final_norm, tm):
    D = D_MODEL
    pad = ROUTER_LANES - MOE_GROUPS - MOE_EXPERTS
    wr = jnp.concatenate([wg, we, jnp.zeros((D, pad), F32)], axis=1)
    br = jnp.concatenate([bg, be, jnp.zeros((pad,), F32)]).reshape(1, ROUTER_LANES)
    t, cw = _router(h2, nw.reshape(1, D), wr, br, tm)
    w13 = jnp.concatenate([w1, w3], axis=-1).astype(BF16)
    return _experts(t, cw, h2, w13, w2.astype(BF16), fnw.reshape(1, D), tm, final_norm)


def _pick(n, prefs):
    for p in prefs:
        if n % p == 0:
            return p
    return n


def _ssd_retention_layer(h2, B, S, nw, w_in, conv_w, conv_b, dt_bias, a_log, d_skip, norm_w, w_out):
    D = D_MODEL
    T = B * S
    o_x, o_bc, o_dt, o_q, o_k, o_v, o_g = 1024, 2048, 2560, 2576, 3088, 3600, 4624
    w_main = jnp.concatenate([w_in[:, 0:o_x], w_in[:, o_x:o_bc], w_in[:, o_v:o_g], w_in[:, o_g:o_g + D],
                              w_in[:, o_bc:o_dt], w_in[:, o_q:o_k], w_in[:, o_k:o_v]], axis=1).astype(BF16)
    w_dt = jnp.pad(w_in[:, o_dt:o_q], ((0, 0), (0, LANES - SSD_HEADS))).astype(BF16)
    tm = _pick(T, (1024, 512, 256, 128))
    proj, dt = _inproj(h2, nw.reshape(1, D), w_main, w_dt, tm, 1408)
    proj3 = proj.reshape(B, S, proj.shape[1])
    dt3 = dt.reshape(B, S, LANES)

    head_of_channel = jnp.arange(D) // SSD_HEAD_DIM
    expand = (jnp.arange(LANES)[:, None] == head_of_channel[None, :]).astype(BF16)
    pad16 = lambda x: jnp.pad(x, (0, LANES - SSD_HEADS)).reshape(1, LANES)
    ya = _ssd(proj3, dt3, conv_w[:, :D], conv_b[:D].reshape(1, D), conv_w[:, D:], conv_b[D:].reshape(1, 512),
              pad16(dt_bias), pad16(a_log), expand, jnp.repeat(d_skip, SSD_HEAD_DIM).reshape(1, D),
              norm_w.reshape(1, D))

    C = RET_CHUNK
    half = RET_QK_HEAD // 2
    pos = jnp.arange(S, dtype=F32)
    inv_freq = ROPE_BASE ** (-jnp.arange(half, dtype=F32) / half)
    ang = pos[:, None] * inv_freq[None, :]
    cc = jnp.concatenate([jnp.cos(ang), jnp.cos(ang)], axis=1)
    ss = jnp.concatenate([-jnp.sin(ang), jnp.sin(ang)], axis=1)
    log_gamma = jnp.log(1.0 - 2.0 ** (-5.0 - jnp.arange(RET_HEADS, dtype=F32)))
    idx = jnp.arange(C, dtype=F32)
    diff = idx[:, None] - idx[None, :]
    dmask = jnp.where(diff[None] >= 0, jnp.exp(jnp.maximum(diff, 0.0)[None] * log_gamma[:, None, None]), 0.0)
    qdec = jnp.repeat(jnp.exp((idx[:, None] + 1.0) * log_gamma[None, :]), RET_V_HEAD, axis=1)
    kdec = jnp.repeat(jnp.exp((C - 1.0 - idx[:, None]) * log_gamma[None, :]), RET_QK_HEAD, axis=1)
    cdec = jnp.repeat(jnp.exp(C * log_gamma), RET_V_HEAD).reshape(1, D)
    yb = _retention(proj3, cc, ss, dmask, qdec, kdec, cdec)

    w_out_b = w_out.astype(BF16)
    return _outproj(ya.reshape(T, D), yb.reshape(T, D), w_out_b[:D], w_out_b[D:], h2, tm)


def _rwkv_layer(h2, B, S, nw, mu, w_r, w_k, w_v, w_o, w0, w1, w2, a0, a1, a2, g1, g2, k_k, k_a, r_k, lnx_w, lnx_b):
    D = D_MODEL
    T = B * S
    padc = lambda w, n: jnp.pad(w, ((0, 0), (0, n - w.shape[1]))).astype(BF16)
    padr = lambda w, n: jnp.pad(w, ((0, n - w.shape[0]), (0, 0))).astype(BF16)
    head_of_channel = jnp.arange(D) // RWKV_HEAD
    hs = (head_of_channel[:, None] == jnp.arange(LANES)[None, :]).astype(BF16)
    row = lambda x: x.reshape(1, D)
    tm = _pick(S, (512, 256, 128))
    r, k, v, kk, a, lw, g = _rwkv_pre(
        h2, row(nw), mu, w_r.astype(BF16), w_k.astype(BF16), w_v.astype(BF16),
        padc(w1, RWKV_LORA_PAD), padr(w2, RWKV_LORA_PAD), row(w0),
        padc(a1, RWKV_LORA_PAD), padr(a2, RWKV_LORA_PAD), row(a0),
        padc(g1, RWKV_GATE_PAD), padr(g2, RWKV_GATE_PAD), row(k_k), row(k_a), hs, hs.T, S, tm)
    sh = lambda x: x.reshape(B, S, D)
    tb = _pick(S, (512, 256, 128, 64))
    y = _rwkv_scan(sh(r), sh(k), sh(v), sh(kk), sh(a), sh(lw), sh(g),
                   row(lnx_w), row(lnx_b), r_k.reshape(1, D), tb, 8)
    return _rwkv_out(y.reshape(T, D), w_o.astype(BF16), h2, _pick(T, (1024, 512, 256, 128)))


def kernel(x, norm_mix_w, norm_ffn_w, norm_final_w, w_in_e, ssd_conv_w, ssd_conv_b, ssd_dt_bias, ssd_a_log, ssd_d, ssd_norm_w, w_out_e, rw_mu, rw_wr, rw_wk, rw_wv, rw_wo, rw_w0, rw_w1, rw_w2, rw_a0, rw_a1, rw_a2, rw_g1, rw_g2, rw_kk, rw_ka, rw_rk, rw_lnx_w, rw_lnx_b, moe_wg, moe_bg, moe_we, moe_be, moe_w1, moe_w3, moe_w2):
    B, S, D = x.shape
    T = B * S
    depth = norm_mix_w.shape[0]
    h = x.reshape(T, D)
    tm_moe = _pick(T, (1024, 512, 256, 128))
    for layer in range(depth):
        i = layer // 2
        if layer % 2 == 0:
            h = _ssd_retention_layer(h, B, S, norm_mix_w[layer], w_in_e[i], ssd_conv_w[i], ssd_conv_b[i],
                                     ssd_dt_bias[i], ssd_a_log[i], ssd_d[i], ssd_norm_w[i], w_out_e[i])
        else:
            h = _rwkv_layer(h, B, S, norm_mix_w[layer], rw_mu[i], rw_wr[i], rw_wk[i], rw_wv[i], rw_wo[i],
                            rw_w0[i], rw_w1[i], rw_w2[i], rw_a0[i], rw_a1[i], rw_a2[i], rw_g1[i], rw_g2[i],
                            rw_kk[i], rw_ka[i], rw_rk[i], rw_lnx_w[i], rw_lnx_b[i])
        h = _moe(h, norm_ffn_w[layer], moe_wg[layer], moe_bg[layer], moe_we[layer], moe_be[layer],
                 moe_w1[layer], moe_w3[layer], moe_w2[layer], norm_final_w,
                 final_norm=(layer == depth - 1), tm=tm_moe)
    return h.reshape(B, S, D)
```

```python
import functools
import math

import jax
import jax.numpy as jnp
from jax import lax
from jax.experimental import pallas as pl
from jax.experimental.pallas import tpu as pltpu

F32 = jnp.float32
BF16 = jnp.bfloat16

D_MODEL = 1024
RMS_EPS = 1e-6
SSD_HEADS = 16
SSD_HEAD_DIM = 64
SSD_GROUPS = 2
SSD_STATE = 128
SSD_CONV = 4
SSD_CHUNK = 128
SSD_CONV_TAIL = 16
SSD_NORM_EPS = 1e-5
SSD_GROUP_WIDTH = D_MODEL // SSD_GROUPS
RET_HEADS = 4
RET_QK_HEAD = 128
RET_V_HEAD = 256
RET_CHUNK = 128
ROPE_BASE = 10000.0
RWKV_HEAD = 64
RWKV_HEADS = 16
RWKV_LN_EPS = 64e-5
RWKV_CHUNK = 64
RWKV_QUAD = 4
RWKV_SEQS_PER_STEP = 2
RWKV_LORA_PAD = 128
RWKV_GATE_PAD = 256
RWKV_PRE_SPLIT = 2
MOE_GROUPS = 4
MOE_EPG = 4
MOE_EXPERTS = 16
MOE_D_FF = 512
ROUTER_LANES = 128
MOE_SUB = 128
MOE_EXPERTS_PER_STEP = 2

LANES = 128
VMEM_LIMIT_BYTES = 56 * 1024 * 1024


def _cparams(sem):
    return pltpu.CompilerParams(dimension_semantics=sem, vmem_limit_bytes=VMEM_LIMIT_BYTES)


def _dot(a, b):
    return jnp.dot(a, b, preferred_element_type=F32)


def _dot_nt(a, b):
    return lax.dot_general(a, b, (((1,), (1,)), ((), ())), preferred_element_type=F32)


def _split2(x):
    hi = x.astype(BF16)
    lo = (x - hi.astype(F32)).astype(BF16)
    return hi, lo


def _dot_x2(x, w_exact):
    hi, lo = _split2(x)
    return _dot(hi, w_exact) + _dot(lo, w_exact)


def _dot_2x(w_exact, x):
    hi, lo = _split2(x)
    return _dot(w_exact, hi) + _dot(w_exact, lo)


def _dot_x3(x, w):
    xh, xl = _split2(x)
    wh, wl = _split2(w)
    return _dot(xh, wh) + _dot(xl, wh) + _dot(xh, wl)


def _sigmoid(x):
    return 0.5 * jnp.tanh(0.5 * x) + 0.5


def _silu(x):
    hx = 0.5 * x
    return hx * jnp.tanh(hx) + hx


def _softplus(x):
    return jnp.maximum(x, 0.0) + jnp.log(1.0 + jnp.exp(-jnp.abs(x)))


def _rms(x, w, eps=RMS_EPS):
    return x * lax.rsqrt(jnp.mean(x * x, axis=-1, keepdims=True) + eps) * w


def _inproj_kernel(x_ref, nw_ref, w_ref, wdt_ref, o_ref, dt_ref, u_ref):
    @pl.when(pl.program_id(1) == 0)
    def _():
        ub = _rms(x_ref[...], nw_ref[...]).astype(BF16)
        u_ref[...] = ub
        dt_ref[...] = _dot(ub, wdt_ref[...])

    o_ref[...] = _dot(u_ref[...], w_ref[...]).astype(o_ref.dtype)


def _inproj(h2, nw, w_main, w_dt, tm, tn):
    T, D = h2.shape
    N = w_main.shape[1]
    return pl.pallas_call(
        _inproj_kernel,
        grid=(T // tm, N // tn),
        in_specs=[
            pl.BlockSpec((tm, D), lambda i, j: (i, 0)),
            pl.BlockSpec((1, D), lambda i, j: (0, 0)),
            pl.BlockSpec((D, tn), lambda i, j: (0, j)),
            pl.BlockSpec((D, LANES), lambda i, j: (0, 0)),
        ],
        out_specs=[
            pl.BlockSpec((tm, tn), lambda i, j: (i, j)),
            pl.BlockSpec((tm, LANES), lambda i, j: (i, 0)),
        ],
        out_shape=[
            jax.ShapeDtypeStruct((T, N), BF16),
            jax.ShapeDtypeStruct((T, LANES), F32),
        ],
        scratch_shapes=[pltpu.VMEM((tm, D), BF16)],
        compiler_params=_cparams(("parallel", "arbitrary")),
    )(h2, nw, w_main, w_dt)


def _ssd_kernel(z_ref, x_ref, bc_ref, dt_ref, cwx_ref, cbx_ref, cwb_ref, cbb_ref,
                dtb_ref, alog_ref, e_ref, dskip_ref, nw_ref, o_ref,
                state_ref, xtail, bctail, ybuf):
    L = SSD_CHUNK
    c = pl.program_id(1)

    @pl.when(c == 0)
    def _():
        state_ref[...] = jnp.zeros_like(state_ref)
        xtail[...] = jnp.zeros_like(xtail)
        bctail[...] = jnp.zeros_like(bctail)

    TAIL = xtail.shape[0]
    srow = lax.broadcasted_iota(jnp.int32, (SSD_CONV * L, TAIL + L), 0)
    scol = lax.broadcasted_iota(jnp.int32, (SSD_CONV * L, TAIL + L), 1)
    shift = jnp.where(scol == (srow % L) + (srow // L) + (TAIL - SSD_CONV + 1), 1.0, 0.0).astype(BF16)

    def conv(cur_ref, tail, w_ref, b_ref):
        cur = cur_ref[0]
        taps = _dot(shift, jnp.concatenate([tail[...], cur], axis=0))
        acc = b_ref[...] + w_ref[0:1, :] * taps[0:L, :]
        for k in range(1, SSD_CONV):
            acc = acc + w_ref[k:k + 1, :] * taps[k * L:(k + 1) * L, :]
        tail[...] = cur[L - TAIL:L, :]
        return _silu(acc)

    xs = conv(x_ref, xtail, cwx_ref, cbx_ref)
    bc = conv(bc_ref, bctail, cwb_ref, cbb_ref)

    dt = _softplus(dt_ref[0] + dtb_ref[...])
    a = dt * (-jnp.exp(alog_ref[...]))
    row = lax.broadcasted_iota(jnp.int32, (L, L), 0)
    col = lax.broadcasted_iota(jnp.int32, (L, L), 1)
    causal = row >= col
    tri = jnp.where(causal, 1.0, 0.0).astype(BF16)
    acum = _dot_2x(tri, a)
    acum_t = acum.T
    e = e_ref[...]
    dt_full = _dot_x2(dt, e)
    acum_full = _dot_x2(acum, e)
    xdt = xs * dt_full
    alast_full = acum_full[L - 1:L, :]
    decay_in = jnp.exp(acum_full)
    xdt_end = (xdt * jnp.exp(alast_full - acum_full)).astype(BF16)
    xdt_b = xdt.astype(BF16)
    lane = lax.broadcasted_iota(jnp.int32, (L, LANES), 1)
    first_head = lane < SSD_HEAD_DIM

    GW = SSD_GROUP_WIDTH
    for g in range(SSD_GROUPS):
        bg = bc[:, g * SSD_STATE:(g + 1) * SSD_STATE]
        cg = bc[:, (SSD_GROUPS + g) * SSD_STATE:(SSD_GROUPS + g + 1) * SSD_STATE].astype(BF16)
        cb = _dot_nt(cg, bg.astype(BF16))
        st = state_ref[g]
        y_off = _dot(cg, st.astype(BF16)) * decay_in[:, g * GW:(g + 1) * GW]
        for p in range(GW // LANES):
            xp = xdt_b[:, g * GW + p * LANES:g * GW + (p + 1) * LANES]
            ys = []
            for s in range(2):
                hd = g * (SSD_HEADS // SSD_GROUPS) + 2 * p + s
                seg = acum[:, hd:hd + 1] - acum_t[hd:hd + 1, :]
                dec = jnp.where(causal, jnp.exp(seg), 0.0)
                ys.append(_dot((cb * dec).astype(BF16), xp))
            yd = jnp.where(first_head, ys[0], ys[1])
            lo = g * GW + p * LANES
            ybuf[:, lo:lo + LANES] = yd + y_off[:, p * LANES:(p + 1) * LANES]
        bg_t = bg.T.astype(BF16)
        state_ref[g] = st * jnp.exp(alast_full[:, g * GW:(g + 1) * GW]) + _dot(
            bg_t, xdt_end[:, g * GW:(g + 1) * GW])

    y = ybuf[...] + xs * dskip_ref[...]
    y = y * _silu(z_ref[0].astype(F32))
    for g in range(SSD_GROUPS):
        yg = y[:, g * GW:(g + 1) * GW]
        yg = yg * lax.rsqrt(jnp.mean(yg * yg, axis=-1, keepdims=True) + SSD_NORM_EPS)
        o_ref[0, :, g * GW:(g + 1) * GW] = (yg * nw_ref[:, g * GW:(g + 1) * GW]).astype(o_ref.dtype)


def _ssd(proj3, dt3, cwx, cbx, cwb, cbb, dtb, alog, expand, dskip, nw):
    B, S, _ = proj3.shape
    L = SSD_CHUNK
    D = D_MODEL
    full = lambda shape: pl.BlockSpec(shape, lambda b, c: (0,) * len(shape))
    return pl.pallas_call(
        _ssd_kernel,
        grid=(B, S // L),
        in_specs=[
            pl.BlockSpec((1, L, D), lambda b, c: (b, c, 0)),
            pl.BlockSpec((1, L, D), lambda b, c: (b, c, 1)),
            pl.BlockSpec((1, L, 512), lambda b, c: (b, c, 8)),
            pl.BlockSpec((1, L, LANES), lambda b, c: (b, c, 0)),
            full((SSD_CONV, D)), full((1, D)), full((SSD_CONV, 512)), full((1, 512)),
            full((1, LANES)), full((1, LANES)), full((LANES, D)), full((1, D)), full((1, D)),
        ],
        out_specs=pl.BlockSpec((1, L, D), lambda b, c: (b, c, 0)),
        out_shape=jax.ShapeDtypeStruct((B, S, D), BF16),
        scratch_shapes=[
            pltpu.VMEM((SSD_GROUPS, SSD_STATE, SSD_GROUP_WIDTH), F32),
            pltpu.VMEM((SSD_CONV_TAIL, D), BF16),
            pltpu.VMEM((SSD_CONV_TAIL, 512), BF16),
            pltpu.VMEM((L, D), F32),
        ],
        compiler_params=_cparams(("parallel", "arbitrary")),
    )(proj3, proj3, proj3, dt3, cwx, cbx, cwb, cbb, dtb, alog, expand, dskip, nw)


def _ret_kernel(q_ref, k_ref, v_ref, g_ref, cc_ref, ss_ref, dmask_ref, qdec_ref, kdec_ref,
                cdec_ref, o_ref, r_ref):
    c = pl.program_id(1)

    @pl.when(c == 0)
    def _():
        r_ref[...] = jnp.zeros_like(r_ref)

    cc = cc_ref[...]
    ss = ss_ref[...]
    dk, dv = RET_QK_HEAD, RET_V_HEAD
    half = dk // 2

    def rope(x):
        return x * cc + pltpu.roll(x, half, 1) * ss

    heads = range(RET_HEADS)
    qs = [rope(q_ref[0, :, hd * dk:(hd + 1) * dk].astype(F32)) for hd in heads]
    ks = [rope(k_ref[0, :, hd * dk:(hd + 1) * dk].astype(F32)) * (dk ** -0.5) for hd in heads]
    vs = [v_ref[0, :, hd * dv:(hd + 1) * dv] for hd in heads]
    qbs = [q.astype(BF16) for q in qs]
    ss = [_dot_nt(qbs[hd], ks[hd].astype(BF16)) * dmask_ref[hd] for hd in heads]
    r_olds = [r_ref[hd] for hd in heads]
    cross = [_dot(qbs[hd], r_olds[hd].astype(BF16)) for hd in heads]
    kd_ts = [(ks[hd] * kdec_ref[:, hd * dk:(hd + 1) * dk]).T.astype(BF16) for hd in heads]
    upds = [_dot(kd_ts[hd], vs[hd]) for hd in heads]
    inner = [_dot(ss[hd].astype(BF16), vs[hd]) for hd in heads]
    for hd in heads:
        r_ref[hd] = r_olds[hd] * cdec_ref[:, hd * dv:(hd + 1) * dv] + upds[hd]
        y = inner[hd] + cross[hd] * qdec_ref[:, hd * dv:(hd + 1) * dv]
        y = y * lax.rsqrt(jnp.mean(y * y, axis=-1, keepdims=True) + RMS_EPS)
        y = y * _silu(g_ref[0, :, hd * dv:(hd + 1) * dv].astype(F32))
        o_ref[0, :, hd * dv:(hd + 1) * dv] = y.astype(o_ref.dtype)


def _retention(proj3, cc, ss, dmask, qdec, kdec, cdec):
    B, S, _ = proj3.shape
    C = RET_CHUNK
    D = D_MODEL
    full = lambda shape: pl.BlockSpec(shape, lambda b, c: (0,) * len(shape))
    return pl.pallas_call(
        _ret_kernel,
        grid=(B, S // C),
        in_specs=[
            pl.BlockSpec((1, C, 512), lambda b, c: (b, c, 9)),
            pl.BlockSpec((1, C, 512), lambda b, c: (b, c, 10)),
            pl.BlockSpec((1, C, D), lambda b, c: (b, c, 2)),
            pl.BlockSpec((1, C, D), lambda b, c: (b, c, 3)),
            pl.BlockSpec((C, RET_QK_HEAD), lambda b, c: (c, 0)),
            pl.BlockSpec((C, RET_QK_HEAD), lambda b, c: (c, 0)),
            full((RET_HEADS, C, C)), full((C, D)), full((C, 512)), full((1, D)),
        ],
        out_specs=pl.BlockSpec((1, C, D), lambda b, c: (b, c, 0)),
        out_shape=jax.ShapeDtypeStruct((B, S, D), BF16),
        scratch_shapes=[pltpu.VMEM((RET_HEADS, RET_QK_HEAD, RET_V_HEAD), F32)],
        compiler_params=_cparams(("parallel", "arbitrary")),
    )(proj3, proj3, proj3, proj3, cc, ss, dmask, qdec, kdec, cdec)


def _outproj_kernel(ya_ref, yb_ref, wa_ref, wb_ref, h_ref, o_ref):
    o_ref[...] = h_ref[...] + _dot(ya_ref[...], wa_ref[...]) + _dot(yb_ref[...], wb_ref[...])


def _outproj(ya, yb, wa, wb, h2, tm):
    T, D = h2.shape
    return pl.pallas_call(
        _outproj_kernel,
        grid=(T // tm,),
        in_specs=[
            pl.BlockSpec((tm, D), lambda i: (i, 0)),
            pl.BlockSpec((tm, D), lambda i: (i, 0)),
            pl.BlockSpec((D, D), lambda i: (0, 0)),
            pl.BlockSpec((D, D), lambda i: (0, 0)),
            pl.BlockSpec((tm, D), lambda i: (i, 0)),
        ],
        out_specs=pl.BlockSpec((tm, D), lambda i: (i, 0)),
        out_shape=jax.ShapeDtypeStruct((T, D), F32),
        compiler_params=_cparams(("parallel",)),
    )(ya, yb, wa, wb, h2)


def _rwkv_pre_kernel(h_ref, hp_ref, nw_ref, mu_ref, wr_ref, wk_ref, wv_ref,
                     w1_ref, w2_ref, w0_ref, a1_ref, a2_ref, a0_ref, g1_ref, g2_ref,
                     kkw_ref, kaw_ref, hs_ref, hst_ref,
                     r_o, k_o, v_o, kk_o, a_o, lw_o, g_o, ubuf, xxbuf, *, tiles_per_seq):
    tm = h_ref.shape[0]
    i = pl.program_id(0)
    nw = nw_ref[...]
    u = _rms(h_ref[...], nw)
    up = _rms(hp_ref[...], nw)
    seq_start = (i % tiles_per_seq) == 0
    ubuf[8:8 + tm, :] = u
    ubuf[0:8, :] = jnp.where(seq_start, 0.0, up)
    xxbuf[...] = ubuf[7:7 + tm, :] - u

    def mix(j, lo, n):
        return (ubuf[8 + lo:8 + lo + n, :] + xxbuf[lo:lo + n, :] * mu_ref[j:j + 1, :]).astype(BF16)

    n = tm // RWKV_PRE_SPLIT
    parts = [s * n for s in range(RWKV_PRE_SPLIT)]
    first = []
    for lo in parts:
        first.append(dict(
            r=_dot(mix(0, lo, n), wr_ref[...]), k=_dot(mix(2, lo, n), wk_ref[...]),
            v=_dot(mix(3, lo, n), wv_ref[...]), w1=_dot(mix(1, lo, n), w1_ref[...]),
            a1=_dot(mix(4, lo, n), a1_ref[...]), g1=_dot(mix(5, lo, n), g1_ref[...])))
    second = []
    for f in first:
        kk = f["k"] * kkw_ref[...]
        second.append(dict(
            wl=w0_ref[...] + _dot(jnp.tanh(f["w1"]).astype(BF16), w2_ref[...]),
            al=a0_ref[...] + _dot(f["a1"].astype(BF16), a2_ref[...]),
            g=_dot(_sigmoid(f["g1"]).astype(BF16), g2_ref[...]),
            kk=kk, ssq=_dot((kk * kk).astype(BF16), hs_ref[...])))
    for lo, f, s in zip(parts, first, second):
        rows = slice(lo, lo + n)
        w_log = -_softplus(-s["wl"]) - 0.5
        a = _sigmoid(s["al"])
        inv = lax.rsqrt(jnp.maximum(s["ssq"], 1e-24))
        r_o[rows, :] = f["r"].astype(r_o.dtype)
        k_o[rows, :] = (f["k"] * (1.0 + (a - 1.0) * kaw_ref[...])).astype(k_o.dtype)
        v_o[rows, :] = f["v"].astype(v_o.dtype)
        kk_o[rows, :] = (s["kk"] * _dot_x2(inv, hst_ref[...])).astype(kk_o.dtype)
        a_o[rows, :] = a.astype(a_o.dtype)
        lw_o[rows, :] = -jnp.exp(w_log)
        g_o[rows, :] = s["g"].astype(g_o.dtype)


def _rwkv_pre(h2, nw, mu, wr, wk, wv, w1, w2, w0, a1, a2, a0, g1, g2, kkw, kaw, hs, hst, S, tm):
    T, D = h2.shape
    full = lambda arr: pl.BlockSpec(arr.shape, lambda i: (0,) * arr.ndim)
    row = pl.BlockSpec((tm, D), lambda i: (i, 0))
    params = (nw, mu, wr, wk, wv, w1, w2, w0, a1, a2, a0, g1, g2, kkw, kaw, hs, hst)
    bf = jax.ShapeDtypeStruct((T, D), BF16)
    return pl.pallas_call(
        functools.partial(_rwkv_pre_kernel, tiles_per_seq=S // tm),
        grid=(T // tm,),
        in_specs=[row, pl.BlockSpec((8, D), lambda i: (jnp.maximum(i * (tm // 8) - 1, 0), 0))]
        + [full(p) for p in params],
        out_specs=[row] * 7,
        out_shape=[bf, bf, bf, bf, bf, jax.ShapeDtypeStruct((T, D), F32), bf],
        scratch_shapes=[pltpu.VMEM((tm + 8, D), F32), pltpu.VMEM((tm, D), F32)],
        compiler_params=_cparams(("parallel",)),
    )(h2, h2, *params)


def _rwkv_scan_kernel(r_ref, k_ref, v_ref, kk_ref, a_ref, lw_ref, g_ref, lnw_ref, lnb_ref, rk_ref,
                      o_ref, state_ref, *, n_chunks, n_pairs):
    C = RWKV_CHUNK
    N = RWKV_HEAD
    t = pl.program_id(2)

    @pl.when(t == 0)
    def _():
        state_ref[...] = jnp.zeros_like(state_ref)

    row = lax.broadcasted_iota(jnp.int32, (C, C), 0)
    col = lax.broadcasted_iota(jnp.int32, (C, C), 1)
    tri_incl = jnp.where(row >= col, 1.0, 0.0).astype(BF16)
    eye = jnp.where(row == col, 1.0, 0.0).astype(F32)
    grow = lax.broadcasted_iota(jnp.int32, (2 * C, 2 * C), 0)
    gcol = lax.broadcasted_iota(jnp.int32, (2 * C, 2 * C), 1)
    gr, gc = grow % C, gcol % C
    gmask = (gr + grow // C) > gc
    blockdiag = (grow // N) == (gcol // N)
    lane = lax.broadcasted_iota(jnp.int32, (C, LANES), 1)
    head_masks = (lane < N, lane >= N)
    upper_lanes = lane >= C

    m0 = head_masks[0]
    n_levels = int(math.log2(C))
    pairs = range(n_pairs)
    chains = [(p, s) for p in pairs for s in range(2)]

    def head_sum(x):
        s0 = jnp.sum(jnp.where(m0, x, 0.0), axis=-1, keepdims=True)
        s1 = jnp.sum(jnp.where(m0, 0.0, x), axis=-1, keepdims=True)
        return jnp.where(m0, s0, s1)

    def chunk_body(ci, carry):
        rows = pl.ds(pl.multiple_of(ci * C, C), C)
        cum_all = _dot_2x(tri_incl, lw_ref[0, rows, :])
        pre = []
        for p in pairs:
            lanes = slice(p * LANES, (p + 1) * LANES)
            r = r_ref[0, rows, lanes].astype(F32)
            k = k_ref[0, rows, lanes].astype(F32)
            v = v_ref[0, rows, lanes].astype(F32)
            kk = kk_ref[0, rows, lanes].astype(F32)
            a = a_ref[0, rows, lanes].astype(F32)
            lw = lw_ref[0, rows, lanes]
            cum = cum_all[:, lanes]
            wc = cum[C - 1:C, :]
            e_out = jnp.exp(-cum)
            e_end = jnp.exp(wc - cum)
            b = kk * a
            at = -kk * jnp.exp(cum - lw)
            rt = r * jnp.exp(cum)
            rhs = jnp.concatenate([b * e_out, k * e_out], axis=0).astype(BF16)
            kb_end = jnp.concatenate([k * e_end, b * e_end], axis=0).astype(BF16)
            vb = v.astype(BF16)
            pre.append(dict(lanes=lanes, r=r, k=k, v=v, wc=wc, at=at, rt=rt, rhs=rhs, kb_end=kb_end,
                            vb=vb, vv=jnp.concatenate([vb, vb], axis=0)))

        grams = []
        for p, s in chains:
            d, m = pre[p], head_masks[s]
            lhs = jnp.concatenate([jnp.where(m, d["at"], 0.0), jnp.where(m, d["rt"], 0.0)], axis=0)
            grams.append(jnp.where(gmask, _dot_nt(lhs.astype(BF16), d["rhs"]), 0.0))
        tops = [g[0:C, :] for g in grams]
        g_bot = [g[C:2 * C, :].astype(BF16) for g in grams]
        mvs = [_dot(jnp.where(upper_lanes, top, 0.0).astype(BF16), pre[p]["vv"])
               for top, (p, s) in zip(tops, chains)]
        lms = [top[:, 0:C] for top in tops]
        pks = [eye + lm for lm in lms]
        lks = [_dot(lm.astype(BF16), lm.astype(BF16)) for lm in lms]
        for lvl in range(1, n_levels):
            if lvl < n_levels - 1:
                boths = [_dot(jnp.concatenate([pk, lk], axis=0).astype(BF16), lk.astype(BF16))
                         for pk, lk in zip(pks, lks)]
                pks = [pk + bo[0:C, :] for pk, bo in zip(pks, boths)]
                lks = [bo[C:2 * C, :] for bo in boths]
            else:
                pks = [pk + _dot(pk.astype(BF16), lk.astype(BF16)) for pk, lk in zip(pks, lks)]
        tws = [_dot(pk.astype(BF16), jnp.concatenate([mv, pre[p]["at"]], axis=1).astype(BF16))
               for pk, mv, (p, s) in zip(pks, mvs, chains)]

        sts = [state_ref[p] for p in pairs]
        zs = []
        for p in pairs:
            atp = jnp.where(m0, tws[2 * p][:, LANES:2 * LANES], tws[2 * p + 1][:, LANES:2 * LANES])
            zs.append(_dot_nt(jnp.concatenate([atp, pre[p]["rt"]], axis=0).astype(BF16), sts[p].astype(BF16)))
        us = [jnp.where(m0, tws[2 * p][:, 0:LANES], tws[2 * p + 1][:, 0:LANES]) + zs[p][0:C, :] for p in pairs]
        uvs = [jnp.concatenate([us[p].astype(BF16), pre[p]["vb"]], axis=0) for p in pairs]
        yss = [_dot(g_bot[i], uvs[p]) for i, (p, s) in enumerate(chains)]
        upds = [_dot(jnp.concatenate([pre[p]["v"], us[p]], axis=0).T.astype(BF16), pre[p]["kb_end"])
                for p in pairs]
        for p in pairs:
            d = pre[p]
            lanes = d["lanes"]
            state_ref[p] = sts[p] * jnp.exp(d["wc"]) + jnp.where(blockdiag, upds[p], 0.0)
            y = zs[p][C:2 * C, :] + jnp.where(m0, yss[2 * p], yss[2 * p + 1])
            mean = head_sum(y) * (1.0 / N)
            yc = y - mean
            var = head_sum(yc * yc) * (1.0 / N)
            yn = yc * lax.rsqrt(var + RWKV_LN_EPS) * lnw_ref[:, lanes] + lnb_ref[:, lanes]
            bonus = head_sum(d["r"] * d["k"] * rk_ref[:, lanes]) * d["v"]
            out = (yn + bonus) * g_ref[0, rows, lanes].astype(F32)
            o_ref[0, rows, lanes] = out.astype(o_ref.dtype)
        return carry

    lax.fori_loop(0, n_chunks, chunk_body, 0)


def _rwkv_scan4_kernel(r_ref, k_ref, v_ref, kk_ref, a_ref, lw_ref, g_ref, lnw_ref, lnb_ref, rk_ref,
                       o_ref, state_ref, y_ref, *, n_chunks, n_quads):
    C = RWKV_CHUNK
    N = RWKV_HEAD
    HQ = RWKV_QUAD
    QW = HQ * N
    t = pl.program_id(2)

    @pl.when(t == 0)
    def _():
        state_ref[...] = jnp.zeros_like(state_ref)

    row = lax.broadcasted_iota(jnp.int32, (C, C), 0)
    col = lax.broadcasted_iota(jnp.int32, (C, C), 1)
    tri_incl = jnp.where(row >= col, 1.0, 0.0).astype(BF16)
    prow = lax.broadcasted_iota(jnp.int32, (C, QW), 0)
    plane = lax.broadcasted_iota(jnp.int32, (C, QW), 1)
    eye_p = jnp.where(prow == plane % C, 1.0, 0.0).astype(F32)
    head_masks = [plane // N == h for h in range(HQ)]
    grow = lax.broadcasted_iota(jnp.int32, (2 * C, 2 * QW), 0)
    gcol = lax.broadcasted_iota(jnp.int32, (2 * C, 2 * QW), 1)
    gmask = (grow % C + grow // C) > (gcol % C)
    srow = lax.broadcasted_iota(jnp.int32, (QW, QW), 0)
    scol = lax.broadcasted_iota(jnp.int32, (QW, QW), 1)
    state_blocks = (srow // N) == (scol // N)
    n_levels = int(math.log2(C))
    n_batch = r_ref.shape[0]
    chains = [(bi, q) for bi in range(n_batch) for q in range(n_quads)]

    head_ones = [jnp.where(m, 1.0, 0.0).astype(BF16) for m in head_masks]

    def bdiag(x):
        xb = x.astype(BF16)
        return jnp.concatenate([xb * one for one in head_ones], axis=0)

    def head_sum(x):
        out = jnp.zeros_like(x)
        for m in head_masks:
            out = jnp.where(m, jnp.sum(jnp.where(m, x, 0.0), axis=-1, keepdims=True), out)
        return out

    def chunk_body(ci, carry):
        rows = pl.ds(pl.multiple_of(ci * C, C), C)
        cum_all = [_dot_2x(tri_incl, lw_ref[bi, rows, :]) for bi in range(n_batch)]
        pre = []
        for bi, q in chains:
            lanes = slice(q * QW, (q + 1) * QW)
            r = r_ref[bi, rows, lanes].astype(F32)
            k = k_ref[bi, rows, lanes].astype(F32)
            v = v_ref[bi, rows, lanes].astype(F32)
            kk = kk_ref[bi, rows, lanes].astype(F32)
            a = a_ref[bi, rows, lanes].astype(F32)
            lw = lw_ref[bi, rows, lanes]
            cum = cum_all[bi][:, lanes]
            wc = cum[C - 1:C, :]
            e_out = jnp.exp(-cum)
            e_end = jnp.exp(wc - cum)
            b = kk * a
            at = -kk * jnp.exp(cum - lw)
            rt = r * jnp.exp(cum)
            pre.append(dict(lanes=lanes, r=r, k=k, v=v, wc=wc, at=at, rt=rt,
                            bh=b * e_out, kh=k * e_out,
                            kb_end=jnp.concatenate([k * e_end, b * e_end], axis=0).astype(BF16)))

        grams = []
        for d in pre:
            lhs = jnp.concatenate([d["at"], d["rt"]], axis=0).astype(BF16)
            rhs = jnp.concatenate([bdiag(d["bh"]), bdiag(d["kh"])], axis=0)
            grams.append(jnp.where(gmask, _dot_nt(lhs, rhs), 0.0))
        lps = [g[0:C, 0:QW] for g in grams]
        bd_vs = [bdiag(d["v"]) for d in pre]
        mvs = [_dot(g[0:C, QW:2 * QW].astype(BF16), bd_v) for g, bd_v in zip(grams, bd_vs)]
        pks = [eye_p + lp for lp in lps]
        lks = [_dot(lp.astype(BF16), bdiag(lp)) for lp in lps]
        for lvl in range(1, n_levels):
            if lvl < n_levels - 1:
                boths = [_dot(jnp.concatenate([pk, lk], axis=0).astype(BF16), bdiag(lk))
                         for pk, lk in zip(pks, lks)]
                pks = [pk + bo[0:C, :] for pk, bo in zip(pks, boths)]
                lks = [bo[C:2 * C, :] for bo in boths]
            else:
                pks = [pk + _dot(pk.astype(BF16), bdiag(lk)) for pk, lk in zip(pks, lks)]
        tws = [_dot(pk.astype(BF16), jnp.concatenate([bdiag(mv), bdiag(d["at"])], axis=1))
               for pk, mv, d in zip(pks, mvs, pre)]

        sts = [state_ref[bi, q] for bi, q in chains]
        zs = [_dot_nt(jnp.concatenate([tw[:, QW:2 * QW], d["rt"]], axis=0).astype(BF16), st.astype(BF16))
              for tw, d, st in zip(tws, pre, sts)]
        us = [tw[:, 0:QW] + z[0:C, :] for tw, z in zip(tws, zs)]
        yss = [_dot(g[C:2 * C, :].astype(BF16), jnp.concatenate([bdiag(u), bd_v], axis=0))
               for g, u, bd_v in zip(grams, us, bd_vs)]
        upds = [_dot(jnp.concatenate([d["v"], u], axis=0).T.astype(BF16), d["kb_end"])
                for d, u in zip(pre, us)]
        for i, (bi, q) in enumerate(chains):
            d = pre[i]
            state_ref[bi, q] = sts[i] * jnp.exp(d["wc"]) + jnp.where(state_blocks, upds[i], 0.0)
            y_ref[bi, :, d["lanes"]] = zs[i][C:2 * C, :] + yss[i]
        return carry

    def finish(ci):
        rows = pl.ds(pl.multiple_of(ci * C, C), C)
        for bi, q in chains:
            lanes = slice(q * QW, (q + 1) * QW)
            y = y_ref[bi, :, lanes]
            mean = head_sum(y) * (1.0 / N)
            yc = y - mean
            var = head_sum(yc * yc) * (1.0 / N)
            yn = yc * lax.rsqrt(var + RWKV_LN_EPS) * lnw_ref[:, lanes] + lnb_ref[:, lanes]
            r = r_ref[bi, rows, lanes].astype(F32)
            k = k_ref[bi, rows, lanes].astype(F32)
            v = v_ref[bi, rows, lanes].astype(F32)
            bonus = head_sum(r * k * rk_ref[:, lanes]) * v
            out = (yn + bonus) * g_ref[bi, rows, lanes].astype(F32)
            o_ref[bi, rows, lanes] = out.astype(o_ref.dtype)

    y_ref[...] = jnp.zeros_like(y_ref)

    def loop_body(ci, carry):
        finish(jnp.maximum(ci - 1, 0))
        return chunk_body(ci, carry)

    lax.fori_loop(0, n_chunks, loop_body, 0)
    finish(n_chunks - 1)


def _rwkv_scan(r, k, v, kk, a, lw, g, lnw, lnb, rk, tb, n_quads, nb):
    B, S, D = r.shape
    QW = RWKV_QUAD * RWKV_HEAD
    W = n_quads * QW
    blk = pl.BlockSpec((nb, tb, W), lambda b, j, t: (b, t, j))
    par = pl.BlockSpec((1, W), lambda b, j, t: (0, j))
    return pl.pallas_call(
        functools.partial(_rwkv_scan4_kernel, n_chunks=tb // RWKV_CHUNK, n_quads=n_quads),
        grid=(B // nb, D // W, S // tb),
        in_specs=[blk] * 7 + [par] * 3,
        out_specs=blk,
        out_shape=jax.ShapeDtypeStruct((B, S, D), BF16),
        scratch_shapes=[pltpu.VMEM((nb, n_quads, QW, QW), F32), pltpu.VMEM((nb, RWKV_CHUNK, W), F32)],
        compiler_params=_cparams(("parallel", "parallel", "arbitrary")),
    )(r, k, v, kk, a, lw, g, lnw, lnb, rk)


def _rwkv_out_kernel(y_ref, w_ref, h_ref, o_ref):
    o_ref[...] = h_ref[...] + _dot(y_ref[...], w_ref[...])


def _rwkv_out(y, wo, h2, tm):
    T, D = h2.shape
    return pl.pallas_call(
        _rwkv_out_kernel,
        grid=(T // tm,),
        in_specs=[pl.BlockSpec((tm, D), lambda i: (i, 0)), pl.BlockSpec((D, D), lambda i: (0, 0)),
                  pl.BlockSpec((tm, D), lambda i: (i, 0))],
        out_specs=pl.BlockSpec((tm, D), lambda i: (i, 0)),
        out_shape=jax.ShapeDtypeStruct((T, D), F32),
        compiler_params=_cparams(("parallel",)),
    )(y, wo, h2)


def _router_kernel(h_ref, nw_ref, wr_ref, br_ref, t_ref, cw_ref):
    tn = _rms(h_ref[...], nw_ref[...])
    t_ref[...] = tn.astype(t_ref.dtype)
    logits = _dot_x3(tn, wr_ref[...]) + br_ref[...]
    lane = lax.broadcasted_iota(jnp.int32, logits.shape, 1).astype(F32)
    neg = jnp.float32(-jnp.inf)
    big = jnp.float32(1 << 20)
    is_g = lane < MOE_GROUPS
    gl = jnp.where(is_g, logits, neg)
    gmax = jnp.max(gl, axis=-1, keepdims=True)
    g_idx = jnp.min(jnp.where(gl == gmax, lane, big), axis=-1, keepdims=True)
    g_gate = 1.0 / jnp.sum(jnp.where(is_g, jnp.exp(logits - gmax), 0.0), axis=-1, keepdims=True)
    lo = MOE_GROUPS + g_idx * MOE_EPG
    sel = (lane >= lo) & (lane < lo + MOE_EPG)
    el = jnp.where(sel, logits, neg)
    m1 = jnp.max(el, axis=-1, keepdims=True)
    i1 = jnp.min(jnp.where(el == m1, lane, big), axis=-1, keepdims=True)
    el2 = jnp.where(lane == i1, neg, el)
    m2 = jnp.max(el2, axis=-1, keepdims=True)
    i2 = jnp.min(jnp.where(el2 == m2, lane, big), axis=-1, keepdims=True)
    e21 = jnp.exp(m2 - m1)
    w1 = g_gate / (1.0 + e21)
    w2 = w1 * e21
    shift = g_idx * MOE_EPG
    cw_ref[...] = jnp.where(lane == 0.0, g_idx,
                            jnp.where(lane == i1 - shift, w1, jnp.where(lane == i2 - shift, w2, 0.0)))


def _router(h2, nw, wr, br, tm):
    T, D = h2.shape
    return pl.pallas_call(
        _router_kernel,
        grid=(T // tm,),
        in_specs=[pl.BlockSpec((tm, D), lambda i: (i, 0)), pl.BlockSpec((1, D), lambda i: (0, 0)),
                  pl.BlockSpec((D, ROUTER_LANES), lambda i: (0, 0)),
                  pl.BlockSpec((1, ROUTER_LANES), lambda i: (0, 0))],
        out_specs=[pl.BlockSpec((tm, D), lambda i: (i, 0)),
                   pl.BlockSpec((tm, ROUTER_LANES), lambda i: (i, 0))],
        out_shape=[jax.ShapeDtypeStruct((T, D), BF16),
                   jax.ShapeDtypeStruct((T, ROUTER_LANES), F32)],
        compiler_params=_cparams(("parallel",)),
    )(h2, nw, wr, br)


def _dot_tn(a, b):
    return lax.dot_general(a, b, (((0,), (0,)), ((), ())), preferred_element_type=F32)


def _experts_kernel(t_ref, cw_ref, h_ref, w13_ref, w2_ref, fnw_ref, o_ref,
                    pt_ref, xs_ref, cws_ref, oacc_ref, meta_ref, *, final_norm):
    tm = t_ref.shape[0]
    cap = pt_ref.shape[1]
    R = MOE_SUB
    F = MOE_D_FF
    g = pl.program_id(1)
    part = pl.program_id(2)
    n_parts = MOE_EPG // MOE_EXPERTS_PER_STEP

    @pl.when((g == 0) & (part == 0))
    def _():
        cw = cw_ref[...]
        lane_t = lax.broadcasted_iota(jnp.int32, (tm, ROUTER_LANES), 1).astype(F32)
        memb = (cw[:, 0:1] == lane_t) & (lane_t < MOE_GROUPS)
        membf = jnp.where(memb, 1.0, 0.0)
        tr = lax.broadcasted_iota(jnp.int32, (tm, tm), 0)
        tc = lax.broadcasted_iota(jnp.int32, (tm, tm), 1)
        before = jnp.where(tr > tc, 1.0, 0.0).astype(BF16)
        ranks = _dot(before, membf.astype(BF16))
        cnt = jnp.sum(membf, axis=0, keepdims=True)
        nsub = jnp.floor((cnt + (R - 1.0)) * (1.0 / R))
        ur = lax.broadcasted_iota(jnp.int32, (ROUTER_LANES, ROUTER_LANES), 0)
        uc = lax.broadcasted_iota(jnp.int32, (ROUTER_LANES, ROUTER_LANES), 1)
        prefix = jnp.where(ur < uc, 1.0, 0.0).astype(BF16)
        base = _dot_x2(jnp.broadcast_to(nsub * R, (8, ROUTER_LANES)), prefix)[0:1, :]
        pos = jnp.sum(jnp.where(memb, ranks + base, 0.0), axis=-1, keepdims=True)
        col = lax.broadcasted_iota(jnp.int32, (tm, cap), 1).astype(F32)
        pt = jnp.where(pos == col, 1.0, 0.0).astype(BF16)
        pt_ref[...] = pt
        cw_hi, cw_lo = _split2(cw)
        D = t_ref.shape[1]
        gathered = _dot_tn(pt, jnp.concatenate([t_ref[...], cw_hi, cw_lo], axis=1))
        xs_ref[...] = gathered[:, 0:D].astype(BF16)
        cws_ref[...] = gathered[:, D:D + ROUTER_LANES] + gathered[:, D + ROUTER_LANES:D + 2 * ROUTER_LANES]
        oacc_ref[...] = jnp.zeros_like(oacc_ref)
        meta_ref[0:1, :] = nsub
        meta_ref[1:2, :] = base

    lane1 = lax.broadcasted_iota(jnp.int32, (1, ROUTER_LANES), 1)
    n_sub = jnp.sum(jnp.where(lane1 == g, meta_ref[0:1, :], 0.0)).astype(jnp.int32)
    row0 = jnp.sum(jnp.where(lane1 == g, meta_ref[1:2, :], 0.0)).astype(jnp.int32)
    lane_r = lax.broadcasted_iota(jnp.int32, (R, ROUTER_LANES), 1)

    def sub_body(s, carry):
        rows = pl.ds(pl.multiple_of(row0 + s * R, R), R)
        xs = xs_ref[rows, :]
        cws = cws_ref[rows, :]
        acc = oacc_ref[rows, :]
        for j in range(MOE_EXPERTS_PER_STEP):
            h13 = _dot(xs, w13_ref[j])
            ce = jnp.sum(jnp.where(lane_r == MOE_GROUPS + part * MOE_EXPERTS_PER_STEP + j, cws, 0.0),
                         axis=-1, keepdims=True)
            he = (_silu(h13[:, 0:F]) * h13[:, F:2 * F] * ce).astype(BF16)
            acc = acc + _dot(he, w2_ref[j])
        oacc_ref[rows, :] = acc
        return carry

    lax.fori_loop(0, n_sub, sub_body, 0)

    @pl.when((g == MOE_GROUPS - 1) & (part == n_parts - 1))
    def _():
        out = h_ref[...] + _dot(pt_ref[...], oacc_ref[...].astype(BF16))
        if final_norm:
            out = _rms(out, fnw_ref[...])
        o_ref[...] = out


def _experts(t, cw, h2, w13, w2, fnw, tm, final_norm):
    T, D = h2.shape
    F = MOE_D_FF
    E = MOE_EXPERTS_PER_STEP
    n_parts = MOE_EPG // E
    cap = tm + MOE_GROUPS * MOE_SUB
    return pl.pallas_call(
        functools.partial(_experts_kernel, final_norm=final_norm),
        grid=(T // tm, MOE_GROUPS, n_parts),
        in_specs=[
            pl.BlockSpec((tm, D), lambda i, g, p: (i, 0)),
            pl.BlockSpec((tm, ROUTER_LANES), lambda i, g, p: (i, 0)),
            pl.BlockSpec((tm, D), lambda i, g, p: (i, 0)),
            pl.BlockSpec((E, D, 2 * F), lambda i, g, p: (g * n_parts + p, 0, 0)),
            pl.BlockSpec((E, F, D), lambda i, g, p: (g * n_parts + p, 0, 0)),
            pl.BlockSpec((1, D), lambda i, g, p: (0, 0)),
        ],
        out_specs=pl.BlockSpec((tm, D), lambda i, g, p: (i, 0)),
        out_shape=jax.ShapeDtypeStruct((T, D), F32),
        scratch_shapes=[
            pltpu.VMEM((tm, cap), BF16),
            pltpu.VMEM((cap, D), BF16),
            pltpu.VMEM((cap, ROUTER_LANES), F32),
            pltpu.VMEM((cap, D), F32),
            pltpu.VMEM((8, ROUTER_LANES), F32),
        ],
        compiler_params=_cparams(("parallel", "arbitrary", "arbitrary")),
    )(t, cw, h2, w13, w2, fnw)


def _moe(h2, nw, wg, bg, we, be, w1, w3, w2, fnw, final_norm, tm):
    D = D_MODEL
    pad = ROUTER_LANES - MOE_GROUPS - MOE_EXPERTS
    wr = jnp.concatenate([wg, we, jnp.zeros((D, pad), F32)], axis=1)
    br = jnp.concatenate([bg, be, jnp.zeros((pad,), F32)]).reshape(1, ROUTER_LANES)
    t, cw = _router(h2, nw.reshape(1, D), wr, br, tm)
    w13 = jnp.concatenate([w1, w3], axis=-1).astype(BF16)
    return _experts(t, cw, h2, w13, w2.astype(BF16), fnw.reshape(1, D), tm, final_norm)


def _pick(n, prefs):
    for p in prefs:
        if n % p == 0:
            return p
    return n


def _ssd_retention_layer(h2, B, S, nw, w_in, conv_w, conv_b, dt_bias, a_log, d_skip, norm_w, w_out):
    D = D_MODEL
    T = B * S
    o_x, o_bc, o_dt, o_q, o_k, o_v, o_g = 1024, 2048, 2560, 2576, 3088, 3600, 4624
    w_main = jnp.concatenate([w_in[:, 0:o_x], w_in[:, o_x:o_bc], w_in[:, o_v:o_g], w_in[:, o_g:o_g + D],
                              w_in[:, o_bc:o_dt], w_in[:, o_q:o_k], w_in[:, o_k:o_v]], axis=1).astype(BF16)
    w_dt = jnp.pad(w_in[:, o_dt:o_q], ((0, 0), (0, LANES - SSD_HEADS))).astype(BF16)
    tm = _pick(T, (1024, 512, 256, 128))
    proj, dt = _inproj(h2, nw.reshape(1, D), w_main, w_dt, tm, 1408)
    proj3 = proj.reshape(B, S, proj.shape[1])
    dt3 = dt.reshape(B, S, LANES)

    head_of_channel = jnp.arange(D) // SSD_HEAD_DIM
    expand = (jnp.arange(LANES)[:, None] == head_of_channel[None, :]).astype(BF16)
    pad16 = lambda x: jnp.pad(x, (0, LANES - SSD_HEADS)).reshape(1, LANES)
    ya = _ssd(proj3, dt3, conv_w[:, :D], conv_b[:D].reshape(1, D), conv_w[:, D:], conv_b[D:].reshape(1, 512),
              pad16(dt_bias), pad16(a_log), expand, jnp.repeat(d_skip, SSD_HEAD_DIM).reshape(1, D),
              norm_w.reshape(1, D))

    C = RET_CHUNK
    half = RET_QK_HEAD // 2
    pos = jnp.arange(S, dtype=F32)
    inv_freq = ROPE_BASE ** (-jnp.arange(half, dtype=F32) / half)
    ang = pos[:, None] * inv_freq[None, :]
    cc = jnp.concatenate([jnp.cos(ang), jnp.cos(ang)], axis=1)
    ss = jnp.concatenate([-jnp.sin(ang), jnp.sin(ang)], axis=1)
    log_gamma = jnp.log(1.0 - 2.0 ** (-5.0 - jnp.arange(RET_HEADS, dtype=F32)))
    idx = jnp.arange(C, dtype=F32)
    diff = idx[:, None] - idx[None, :]
    dmask = jnp.where(diff[None] >= 0, jnp.exp(jnp.maximum(diff, 0.0)[None] * log_gamma[:, None, None]), 0.0)
    qdec = jnp.repeat(jnp.exp((idx[:, None] + 1.0) * log_gamma[None, :]), RET_V_HEAD, axis=1)
    kdec = jnp.repeat(jnp.exp((C - 1.0 - idx[:, None]) * log_gamma[None, :]), RET_QK_HEAD, axis=1)
    cdec = jnp.repeat(jnp.exp(C * log_gamma), RET_V_HEAD).reshape(1, D)
    yb = _retention(proj3, cc, ss, dmask, qdec, kdec, cdec)

    w_out_b = w_out.astype(BF16)
    return _outproj(ya.reshape(T, D), yb.reshape(T, D), w_out_b[:D], w_out_b[D:], h2, tm)


def _rwkv_layer(h2, B, S, nw, mu, w_r, w_k, w_v, w_o, w0, w1, w2, a0, a1, a2, g1, g2, k_k, k_a, r_k, lnx_w, lnx_b):
    D = D_MODEL
    T = B * S
    padc = lambda w, n: jnp.pad(w, ((0, 0), (0, n - w.shape[1]))).astype(BF16)
    padr = lambda w, n: jnp.pad(w, ((0, n - w.shape[0]), (0, 0))).astype(BF16)
    head_of_channel = jnp.arange(D) // RWKV_HEAD
    hs = (head_of_channel[:, None] == jnp.arange(LANES)[None, :]).astype(BF16)
    row = lambda x: x.reshape(1, D)
    tm = _pick(S, (512, 256, 128))
    r, k, v, kk, a, lw, g = _rwkv_pre(
        h2, row(nw), mu, w_r.astype(BF16), w_k.astype(BF16), w_v.astype(BF16),
        padc(w1, RWKV_LORA_PAD), padr(w2, RWKV_LORA_PAD), row(w0),
        padc(a1, RWKV_LORA_PAD), padr(a2, RWKV_LORA_PAD), row(a0),
        padc(g1, RWKV_GATE_PAD), padr(g2, RWKV_GATE_PAD), row(k_k), row(k_a), hs, hs.T, S, tm)
    sh = lambda x: x.reshape(B, S, D)
    tb = _pick(S, (512, 256, 128, 64))
    y = _rwkv_scan(sh(r), sh(k), sh(v), sh(kk), sh(a), sh(lw), sh(g),
                   row(lnx_w), row(lnx_b), r_k.reshape(1, D), tb, RWKV_HEADS // RWKV_QUAD,
                   _pick(B, (RWKV_SEQS_PER_STEP, 1)))
    return _rwkv_out(y.reshape(T, D), w_o.astype(BF16), h2, _pick(T, (1024, 512, 256, 128)))


def kernel(x, norm_mix_w, norm_ffn_w, norm_final_w, w_in_e, ssd_conv_w, ssd_conv_b, ssd_dt_bias, ssd_a_log, ssd_d, ssd_norm_w, w_out_e, rw_mu, rw_wr, rw_wk, rw_wv, rw_wo, rw_w0, rw_w1, rw_w2, rw_a0, rw_a1, rw_a2, rw_g1, rw_g2, rw_kk, rw_ka, rw_rk, rw_lnx_w, rw_lnx_b, moe_wg, moe_bg, moe_we, moe_be, moe_w1, moe_w3, moe_w2):
    B, S, D = x.shape
    T = B * S
    depth = norm_mix_w.shape[0]
    h = x.reshape(T, D)
    tm_moe = _pick(T, (1024, 512, 256, 128))
    for layer in range(depth):
        i = layer // 2
        if layer % 2 == 0:
            h = _ssd_retention_layer(h, B, S, norm_mix_w[layer], w_in_e[i], ssd_conv_w[i], ssd_conv_b[i],
                                     ssd_dt_bias[i], ssd_a_log[i], ssd_d[i], ssd_norm_w[i], w_out_e[i])
        else:
            h = _rwkv_layer(h, B, S, norm_mix_w[layer], rw_mu[i], rw_wr[i], rw_wk[i], rw_wv[i], rw_wo[i],
                            rw_w0[i], rw_w1[i], rw_w2[i], rw_a0[i], rw_a1[i], rw_a2[i], rw_g1[i], rw_g2[i],
                            rw_kk[i], rw_ka[i], rw_rk[i], rw_lnx_w[i], rw_lnx_b[i])
        h = _moe(h, norm_ffn_w[layer], moe_wg[layer], moe_bg[layer], moe_we[layer], moe_be[layer],
                 moe_w1[layer], moe_w3[layer], moe_w2[layer], norm_final_w,
                 final_norm=(layer == depth - 1), tm=tm_moe)
    return h.reshape(B, S, D)
```

```python
import functools
import math

import jax
import jax.numpy as jnp
from jax import lax
from jax.experimental import pallas as pl
from jax.experimental.pallas import tpu as pltpu

F32 = jnp.float32
BF16 = jnp.bfloat16

D_MODEL = 1024
RMS_EPS = 1e-6
SSD_HEADS = 16
SSD_HEAD_DIM = 64
SSD_GROUPS = 2
SSD_STATE = 128
SSD_CONV = 4
SSD_CHUNK = 128
SSD_CONV_TAIL = 16
SSD_NORM_EPS = 1e-5
SSD_GROUP_WIDTH = D_MODEL // SSD_GROUPS
RET_HEADS = 4
RET_QK_HEAD = 128
RET_V_HEAD = 256
RET_CHUNK = 128
ROPE_BASE = 10000.0
RWKV_HEAD = 64
RWKV_HEADS = 16
RWKV_LN_EPS = 64e-5
RWKV_CHUNK = 64
RWKV_QUAD = 4
RWKV_SEQS_PER_STEP = 2
RWKV_LORA_PAD = 128
RWKV_GATE_PAD = 256
RWKV_PRE_SPLIT = 2
MOE_GROUPS = 4
MOE_EPG = 4
MOE_EXPERTS = 16
MOE_D_FF = 512
ROUTER_LANES = 128
MOE_SUB = 128
MOE_EXPERTS_PER_STEP = 2

LANES = 128
VMEM_LIMIT_BYTES = 56 * 1024 * 1024


def _cparams(sem):
    return pltpu.CompilerParams(dimension_semantics=sem, vmem_limit_bytes=VMEM_LIMIT_BYTES)


def _dot(a, b):
    return jnp.dot(a, b, preferred_element_type=F32)


def _dot_nt(a, b):
    return lax.dot_general(a, b, (((1,), (1,)), ((), ())), preferred_element_type=F32)


def _split2(x):
    hi = x.astype(BF16)
    lo = (x - hi.astype(F32)).astype(BF16)
    return hi, lo


def _dot_x2(x, w_exact):
    hi, lo = _split2(x)
    return _dot(hi, w_exact) + _dot(lo, w_exact)


def _dot_2x(w_exact, x):
    hi, lo = _split2(x)
    return _dot(w_exact, hi) + _dot(w_exact, lo)


def _dot_x3(x, w):
    xh, xl = _split2(x)
    wh, wl = _split2(w)
    return _dot(xh, wh) + _dot(xl, wh) + _dot(xh, wl)


def _sigmoid(x):
    return 0.5 * jnp.tanh(0.5 * x) + 0.5


def _silu(x):
    hx = 0.5 * x
    return hx * jnp.tanh(hx) + hx


def _softplus(x):
    return jnp.maximum(x, 0.0) + jnp.log(1.0 + jnp.exp(-jnp.abs(x)))


def _rms(x, w, eps=RMS_EPS):
    return x * lax.rsqrt(jnp.mean(x * x, axis=-1, keepdims=True) + eps) * w


def _inproj_kernel(x_ref, nw_ref, w_ref, wdt_ref, o_ref, dt_ref, u_ref):
    @pl.when(pl.program_id(1) == 0)
    def _():
        ub = _rms(x_ref[...], nw_ref[...]).astype(BF16)
        u_ref[...] = ub
        dt_ref[...] = _dot(ub, wdt_ref[...])

    o_ref[...] = _dot(u_ref[...], w_ref[...]).astype(o_ref.dtype)


def _inproj(h2, nw, w_main, w_dt, tm, tn):
    T, D = h2.shape
    N = w_main.shape[1]
    return pl.pallas_call(
        _inproj_kernel,
        grid=(T // tm, N // tn),
        in_specs=[
            pl.BlockSpec((tm, D), lambda i, j: (i, 0)),
            pl.BlockSpec((1, D), lambda i, j: (0, 0)),
            pl.BlockSpec((D, tn), lambda i, j: (0, j)),
            pl.BlockSpec((D, LANES), lambda i, j: (0, 0)),
        ],
        out_specs=[
            pl.BlockSpec((tm, tn), lambda i, j: (i, j)),
            pl.BlockSpec((tm, LANES), lambda i, j: (i, 0)),
        ],
        out_shape=[
            jax.ShapeDtypeStruct((T, N), BF16),
            jax.ShapeDtypeStruct((T, LANES), F32),
        ],
        scratch_shapes=[pltpu.VMEM((tm, D), BF16)],
        compiler_params=_cparams(("parallel", "arbitrary")),
    )(h2, nw, w_main, w_dt)


def _ssd_kernel(z_ref, x_ref, bc_ref, dt_ref, cwx_ref, cbx_ref, cwb_ref, cbb_ref,
                dtb_ref, alog_ref, e_ref, dskip_ref, nw_ref, o_ref,
                state_ref, xtail, bctail, ybuf):
    L = SSD_CHUNK
    c = pl.program_id(1)

    @pl.when(c == 0)
    def _():
        state_ref[...] = jnp.zeros_like(state_ref)
        xtail[...] = jnp.zeros_like(xtail)
        bctail[...] = jnp.zeros_like(bctail)

    TAIL = xtail.shape[0]
    srow = lax.broadcasted_iota(jnp.int32, (SSD_CONV * L, TAIL + L), 0)
    scol = lax.broadcasted_iota(jnp.int32, (SSD_CONV * L, TAIL + L), 1)
    shift = jnp.where(scol == (srow % L) + (srow // L) + (TAIL - SSD_CONV + 1), 1.0, 0.0).astype(BF16)

    def conv(cur_ref, tail, w_ref, b_ref):
        cur = cur_ref[0]
        taps = _dot(shift, jnp.concatenate([tail[...], cur], axis=0))
        acc = b_ref[...] + w_ref[0:1, :] * taps[0:L, :]
        for k in range(1, SSD_CONV):
            acc = acc + w_ref[k:k + 1, :] * taps[k * L:(k + 1) * L, :]
        tail[...] = cur[L - TAIL:L, :]
        return _silu(acc)

    xs = conv(x_ref, xtail, cwx_ref, cbx_ref)
    bc = conv(bc_ref, bctail, cwb_ref, cbb_ref)

    dt = _softplus(dt_ref[0] + dtb_ref[...])
    a = dt * (-jnp.exp(alog_ref[...]))
    row = lax.broadcasted_iota(jnp.int32, (L, L), 0)
    col = lax.broadcasted_iota(jnp.int32, (L, L), 1)
    causal = row >= col
    tri = jnp.where(causal, 1.0, 0.0).astype(BF16)
    acum = _dot_2x(tri, a)
    acum_t = acum.T
    e = e_ref[...]
    dt_full = _dot_x2(dt, e)
    acum_full = _dot_x2(acum, e)
    xdt = xs * dt_full
    alast_full = acum_full[L - 1:L, :]
    decay_in = jnp.exp(acum_full)
    xdt_end = (xdt * jnp.exp(alast_full - acum_full)).astype(BF16)
    xdt_b = xdt.astype(BF16)
    lane = lax.broadcasted_iota(jnp.int32, (L, LANES), 1)
    first_head = lane < SSD_HEAD_DIM

    GW = SSD_GROUP_WIDTH
    for g in range(SSD_GROUPS):
        bg = bc[:, g * SSD_STATE:(g + 1) * SSD_STATE]
        cg = bc[:, (SSD_GROUPS + g) * SSD_STATE:(SSD_GROUPS + g + 1) * SSD_STATE].astype(BF16)
        cb = _dot_nt(cg, bg.astype(BF16))
        st = state_ref[g]
        y_off = _dot(cg, st.astype(BF16)) * decay_in[:, g * GW:(g + 1) * GW]
        for p in range(GW // LANES):
            xp = xdt_b[:, g * GW + p * LANES:g * GW + (p + 1) * LANES]
            ys = []
            for s in range(2):
                hd = g * (SSD_HEADS // SSD_GROUPS) + 2 * p + s
                seg = acum[:, hd:hd + 1] - acum_t[hd:hd + 1, :]
                dec = jnp.where(causal, jnp.exp(seg), 0.0)
                ys.append(_dot((cb * dec).astype(BF16), xp))
            yd = jnp.where(first_head, ys[0], ys[1])
            lo = g * GW + p * LANES
            ybuf[:, lo:lo + LANES] = yd + y_off[:, p * LANES:(p + 1) * LANES]
        bg_t = bg.T.astype(BF16)
        state_ref[g] = st * jnp.exp(alast_full[:, g * GW:(g + 1) * GW]) + _dot(
            bg_t, xdt_end[:, g * GW:(g + 1) * GW])

    y = ybuf[...] + xs * dskip_ref[...]
    y = y * _silu(z_ref[0].astype(F32))
    for g in range(SSD_GROUPS):
        yg = y[:, g * GW:(g + 1) * GW]
        yg = yg * lax.rsqrt(jnp.mean(yg * yg, axis=-1, keepdims=True) + SSD_NORM_EPS)
        o_ref[0, :, g * GW:(g + 1) * GW] = (yg * nw_ref[:, g * GW:(g + 1) * GW]).astype(o_ref.dtype)


def _ssd(proj3, dt3, cwx, cbx, cwb, cbb, dtb, alog, expand, dskip, nw):
    B, S, _ = proj3.shape
    L = SSD_CHUNK
    D = D_MODEL
    full = lambda shape: pl.BlockSpec(shape, lambda b, c: (0,) * len(shape))
    return pl.pallas_call(
        _ssd_kernel,
        grid=(B, S // L),
        in_specs=[
            pl.BlockSpec((1, L, D), lambda b, c: (b, c, 0)),
            pl.BlockSpec((1, L, D), lambda b, c: (b, c, 1)),
            pl.BlockSpec((1, L, 512), lambda b, c: (b, c, 8)),
            pl.BlockSpec((1, L, LANES), lambda b, c: (b, c, 0)),
            full((SSD_CONV, D)), full((1, D)), full((SSD_CONV, 512)), full((1, 512)),
            full((1, LANES)), full((1, LANES)), full((LANES, D)), full((1, D)), full((1, D)),
        ],
        out_specs=pl.BlockSpec((1, L, D), lambda b, c: (b, c, 0)),
        out_shape=jax.ShapeDtypeStruct((B, S, D), BF16),
        scratch_shapes=[
            pltpu.VMEM((SSD_GROUPS, SSD_STATE, SSD_GROUP_WIDTH), F32),
            pltpu.VMEM((SSD_CONV_TAIL, D), BF16),
            pltpu.VMEM((SSD_CONV_TAIL, 512), BF16),
            pltpu.VMEM((L, D), F32),
        ],
        compiler_params=_cparams(("parallel", "arbitrary")),
    )(proj3, proj3, proj3, dt3, cwx, cbx, cwb, cbb, dtb, alog, expand, dskip, nw)


def _ret_kernel(q_ref, k_ref, v_ref, g_ref, cc_ref, ss_ref, dmask_ref, qdec_ref, kdec_ref,
                cdec_ref, o_ref, r_ref):
    c = pl.program_id(1)

    @pl.when(c == 0)
    def _():
        r_ref[...] = jnp.zeros_like(r_ref)

    cc = cc_ref[...]
    ss = ss_ref[...]
    dk, dv = RET_QK_HEAD, RET_V_HEAD
    half = dk // 2

    def rope(x):
        return x * cc + pltpu.roll(x, half, 1) * ss

    heads = range(RET_HEADS)
    qs = [rope(q_ref[0, :, hd * dk:(hd + 1) * dk].astype(F32)) for hd in heads]
    ks = [rope(k_ref[0, :, hd * dk:(hd + 1) * dk].astype(F32)) * (dk ** -0.5) for hd in heads]
    vs = [v_ref[0, :, hd * dv:(hd + 1) * dv] for hd in heads]
    qbs = [q.astype(BF16) for q in qs]
    ss = [_dot_nt(qbs[hd], ks[hd].astype(BF16)) * dmask_ref[hd] for hd in heads]
    r_olds = [r_ref[hd] for hd in heads]
    cross = [_dot(qbs[hd], r_olds[hd].astype(BF16)) for hd in heads]
    kd_ts = [(ks[hd] * kdec_ref[:, hd * dk:(hd + 1) * dk]).T.astype(BF16) for hd in heads]
    upds = [_dot(kd_ts[hd], vs[hd]) for hd in heads]
    inner = [_dot(ss[hd].astype(BF16), vs[hd]) for hd in heads]
    for hd in heads:
        r_ref[hd] = r_olds[hd] * cdec_ref[:, hd * dv:(hd + 1) * dv] + upds[hd]
        y = inner[hd] + cross[hd] * qdec_ref[:, hd * dv:(hd + 1) * dv]
        y = y * lax.rsqrt(jnp.mean(y * y, axis=-1, keepdims=True) + RMS_EPS)
        y = y * _silu(g_ref[0, :, hd * dv:(hd + 1) * dv].astype(F32))
        o_ref[0, :, hd * dv:(hd + 1) * dv] = y.astype(o_ref.dtype)


def _retention(proj3, cc, ss, dmask, qdec, kdec, cdec):
    B, S, _ = proj3.shape
    C = RET_CHUNK
    D = D_MODEL
    full = lambda shape: pl.BlockSpec(shape, lambda b, c: (0,) * len(shape))
    return pl.pallas_call(
        _ret_kernel,
        grid=(B, S // C),
        in_specs=[
            pl.BlockSpec((1, C, 512), lambda b, c: (b, c, 9)),
            pl.BlockSpec((1, C, 512), lambda b, c: (b, c, 10)),
            pl.BlockSpec((1, C, D), lambda b, c: (b, c, 2)),
            pl.BlockSpec((1, C, D), lambda b, c: (b, c, 3)),
            pl.BlockSpec((C, RET_QK_HEAD), lambda b, c: (c, 0)),
            pl.BlockSpec((C, RET_QK_HEAD), lambda b, c: (c, 0)),
            full((RET_HEADS, C, C)), full((C, D)), full((C, 512)), full((1, D)),
        ],
        out_specs=pl.BlockSpec((1, C, D), lambda b, c: (b, c, 0)),
        out_shape=jax.ShapeDtypeStruct((B, S, D), BF16),
        scratch_shapes=[pltpu.VMEM((RET_HEADS, RET_QK_HEAD, RET_V_HEAD), F32)],
        compiler_params=_cparams(("parallel", "arbitrary")),
    )(proj3, proj3, proj3, proj3, cc, ss, dmask, qdec, kdec, cdec)


def _outproj_kernel(ya_ref, yb_ref, wa_ref, wb_ref, h_ref, o_ref):
    o_ref[...] = h_ref[...] + _dot(ya_ref[...], wa_ref[...]) + _dot(yb_ref[...], wb_ref[...])


def _outproj(ya, yb, wa, wb, h2, tm):
    T, D = h2.shape
    return pl.pallas_call(
        _outproj_kernel,
        grid=(T // tm,),
        in_specs=[
            pl.BlockSpec((tm, D), lambda i: (i, 0)),
            pl.BlockSpec((tm, D), lambda i: (i, 0)),
            pl.BlockSpec((D, D), lambda i: (0, 0)),
            pl.BlockSpec((D, D), lambda i: (0, 0)),
            pl.BlockSpec((tm, D), lambda i: (i, 0)),
        ],
        out_specs=pl.BlockSpec((tm, D), lambda i: (i, 0)),
        out_shape=jax.ShapeDtypeStruct((T, D), F32),
        compiler_params=_cparams(("parallel",)),
    )(ya, yb, wa, wb, h2)


def _rwkv_pre_kernel(h_ref, hp_ref, nw_ref, mu_ref, wr_ref, wk_ref, wv_ref,
                     w1_ref, w2_ref, w0_ref, a1_ref, a2_ref, a0_ref, g1_ref, g2_ref,
                     kkw_ref, kaw_ref, hs_ref, hst_ref,
                     r_o, k_o, v_o, kk_o, a_o, lw_o, g_o, ubuf, xxbuf, *, tiles_per_seq):
    tm = h_ref.shape[0]
    i = pl.program_id(0)
    nw = nw_ref[...]
    u = _rms(h_ref[...], nw)
    up = _rms(hp_ref[...], nw)
    seq_start = (i % tiles_per_seq) == 0
    ubuf[8:8 + tm, :] = u
    ubuf[0:8, :] = jnp.where(seq_start, 0.0, up)
    xxbuf[...] = ubuf[7:7 + tm, :] - u

    def mix(j, lo, n):
        return (ubuf[8 + lo:8 + lo + n, :] + xxbuf[lo:lo + n, :] * mu_ref[j:j + 1, :]).astype(BF16)

    n = tm // RWKV_PRE_SPLIT
    parts = [s * n for s in range(RWKV_PRE_SPLIT)]
    first = []
    for lo in parts:
        first.append(dict(
            r=_dot(mix(0, lo, n), wr_ref[...]), k=_dot(mix(2, lo, n), wk_ref[...]),
            v=_dot(mix(3, lo, n), wv_ref[...]), w1=_dot(mix(1, lo, n), w1_ref[...]),
            a1=_dot(mix(4, lo, n), a1_ref[...]), g1=_dot(mix(5, lo, n), g1_ref[...])))
    second = []
    for f in first:
        kk = f["k"] * kkw_ref[...]
        second.append(dict(
            wl=w0_ref[...] + _dot(jnp.tanh(f["w1"]).astype(BF16), w2_ref[...]),
            al=a0_ref[...] + _dot(f["a1"].astype(BF16), a2_ref[...]),
            g=_dot(_sigmoid(f["g1"]).astype(BF16), g2_ref[...]),
            kk=kk, ssq=_dot((kk * kk).astype(BF16), hs_ref[...])))
    for lo, f, s in zip(parts, first, second):
        rows = slice(lo, lo + n)
        w_log = -_softplus(-s["wl"]) - 0.5
        a = _sigmoid(s["al"])
        inv = lax.rsqrt(jnp.maximum(s["ssq"], 1e-24))
        r_o[rows, :] = f["r"].astype(r_o.dtype)
        k_o[rows, :] = (f["k"] * (1.0 + (a - 1.0) * kaw_ref[...])).astype(k_o.dtype)
        v_o[rows, :] = f["v"].astype(v_o.dtype)
        kk_o[rows, :] = (s["kk"] * _dot_x2(inv, hst_ref[...])).astype(kk_o.dtype)
        a_o[rows, :] = a.astype(a_o.dtype)
        lw_o[rows, :] = -jnp.exp(w_log)
        g_o[rows, :] = s["g"].astype(g_o.dtype)


def _rwkv_pre(h2, nw, mu, wr, wk, wv, w1, w2, w0, a1, a2, a0, g1, g2, kkw, kaw, hs, hst, S, tm):
    T, D = h2.shape
    full = lambda arr: pl.BlockSpec(arr.shape, lambda i: (0,) * arr.ndim)
    row = pl.BlockSpec((tm, D), lambda i: (i, 0))
    params = (nw, mu, wr, wk, wv, w1, w2, w0, a1, a2, a0, g1, g2, kkw, kaw, hs, hst)
    bf = jax.ShapeDtypeStruct((T, D), BF16)
    return pl.pallas_call(
        functools.partial(_rwkv_pre_kernel, tiles_per_seq=S // tm),
        grid=(T // tm,),
        in_specs=[row, pl.BlockSpec((8, D), lambda i: (jnp.maximum(i * (tm // 8) - 1, 0), 0))]
        + [full(p) for p in params],
        out_specs=[row] * 7,
        out_shape=[bf, bf, bf, bf, bf, jax.ShapeDtypeStruct((T, D), F32), bf],
        scratch_shapes=[pltpu.VMEM((tm + 8, D), F32), pltpu.VMEM((tm, D), F32)],
        compiler_params=_cparams(("parallel",)),
    )(h2, h2, *params)


def _rwkv_scan_kernel(r_ref, k_ref, v_ref, kk_ref, a_ref, lw_ref, g_ref, lnw_ref, lnb_ref, rk_ref,
                      o_ref, state_ref, *, n_chunks, n_pairs):
    C = RWKV_CHUNK
    N = RWKV_HEAD
    t = pl.program_id(2)

    @pl.when(t == 0)
    def _():
        state_ref[...] = jnp.zeros_like(state_ref)

    row = lax.broadcasted_iota(jnp.int32, (C, C), 0)
    col = lax.broadcasted_iota(jnp.int32, (C, C), 1)
    tri_incl = jnp.where(row >= col, 1.0, 0.0).astype(BF16)
    eye = jnp.where(row == col, 1.0, 0.0).astype(F32)
    grow = lax.broadcasted_iota(jnp.int32, (2 * C, 2 * C), 0)
    gcol = lax.broadcasted_iota(jnp.int32, (2 * C, 2 * C), 1)
    gr, gc = grow % C, gcol % C
    gmask = (gr + grow // C) > gc
    blockdiag = (grow // N) == (gcol // N)
    lane = lax.broadcasted_iota(jnp.int32, (C, LANES), 1)
    head_masks = (lane < N, lane >= N)
    upper_lanes = lane >= C

    m0 = head_masks[0]
    n_levels = int(math.log2(C))
    pairs = range(n_pairs)
    chains = [(p, s) for p in pairs for s in range(2)]

    def head_sum(x):
        s0 = jnp.sum(jnp.where(m0, x, 0.0), axis=-1, keepdims=True)
        s1 = jnp.sum(jnp.where(m0, 0.0, x), axis=-1, keepdims=True)
        return jnp.where(m0, s0, s1)

    def chunk_body(ci, carry):
        rows = pl.ds(pl.multiple_of(ci * C, C), C)
        cum_all = _dot_2x(tri_incl, lw_ref[0, rows, :])
        pre = []
        for p in pairs:
            lanes = slice(p * LANES, (p + 1) * LANES)
            r = r_ref[0, rows, lanes].astype(F32)
            k = k_ref[0, rows, lanes].astype(F32)
            v = v_ref[0, rows, lanes].astype(F32)
            kk = kk_ref[0, rows, lanes].astype(F32)
            a = a_ref[0, rows, lanes].astype(F32)
            lw = lw_ref[0, rows, lanes]
            cum = cum_all[:, lanes]
            wc = cum[C - 1:C, :]
            e_out = jnp.exp(-cum)
            e_end = jnp.exp(wc - cum)
            b = kk * a
            at = -kk * jnp.exp(cum - lw)
            rt = r * jnp.exp(cum)
            rhs = jnp.concatenate([b * e_out, k * e_out], axis=0).astype(BF16)
            kb_end = jnp.concatenate([k * e_end, b * e_end], axis=0).astype(BF16)
            vb = v.astype(BF16)
            pre.append(dict(lanes=lanes, r=r, k=k, v=v, wc=wc, at=at, rt=rt, rhs=rhs, kb_end=kb_end,
                            vb=vb, vv=jnp.concatenate([vb, vb], axis=0)))

        grams = []
        for p, s in chains:
            d, m = pre[p], head_masks[s]
            lhs = jnp.concatenate([jnp.where(m, d["at"], 0.0), jnp.where(m, d["rt"], 0.0)], axis=0)
            grams.append(jnp.where(gmask, _dot_nt(lhs.astype(BF16), d["rhs"]), 0.0))
        tops = [g[0:C, :] for g in grams]
        g_bot = [g[C:2 * C, :].astype(BF16) for g in grams]
        mvs = [_dot(jnp.where(upper_lanes, top, 0.0).astype(BF16), pre[p]["vv"])
               for top, (p, s) in zip(tops, chains)]
        lms = [top[:, 0:C] for top in tops]
        pks = [eye + lm for lm in lms]
        lks = [_dot(lm.astype(BF16), lm.astype(BF16)) for lm in lms]
        for lvl in range(1, n_levels):
            if lvl < n_levels - 1:
                boths = [_dot(jnp.concatenate([pk, lk], axis=0).astype(BF16), lk.astype(BF16))
                         for pk, lk in zip(pks, lks)]
                pks = [pk + bo[0:C, :] for pk, bo in zip(pks, boths)]
                lks = [bo[C:2 * C, :] for bo in boths]
            else:
                pks = [pk + _dot(pk.astype(BF16), lk.astype(BF16)) for pk, lk in zip(pks, lks)]
        tws = [_dot(pk.astype(BF16), jnp.concatenate([mv, pre[p]["at"]], axis=1).astype(BF16))
               for pk, mv, (p, s) in zip(pks, mvs, chains)]

        sts = [state_ref[p] for p in pairs]
        zs = []
        for p in pairs:
            atp = jnp.where(m0, tws[2 * p][:, LANES:2 * LANES], tws[2 * p + 1][:, LANES:2 * LANES])
            zs.append(_dot_nt(jnp.concatenate([atp, pre[p]["rt"]], axis=0).astype(BF16), sts[p].astype(BF16)))
        us = [jnp.where(m0, tws[2 * p][:, 0:LANES], tws[2 * p + 1][:, 0:LANES]) + zs[p][0:C, :] for p in pairs]
        uvs = [jnp.concatenate([us[p].astype(BF16), pre[p]["vb"]], axis=0) for p in pairs]
        yss = [_dot(g_bot[i], uvs[p]) for i, (p, s) in enumerate(chains)]
        upds = [_dot(jnp.concatenate([pre[p]["v"], us[p]], axis=0).T.astype(BF16), pre[p]["kb_end"])
                for p in pairs]
        for p in pairs:
            d = pre[p]
            lanes = d["lanes"]
            state_ref[p] = sts[p] * jnp.exp(d["wc"]) + jnp.where(blockdiag, upds[p], 0.0)
            y = zs[p][C:2 * C, :] + jnp.where(m0, yss[2 * p], yss[2 * p + 1])
            mean = head_sum(y) * (1.0 / N)
            yc = y - mean
            var = head_sum(yc * yc) * (1.0 / N)
            yn = yc * lax.rsqrt(var + RWKV_LN_EPS) * lnw_ref[:, lanes] + lnb_ref[:, lanes]
            bonus = head_sum(d["r"] * d["k"] * rk_ref[:, lanes]) * d["v"]
            out = (yn + bonus) * g_ref[0, rows, lanes].astype(F32)
            o_ref[0, rows, lanes] = out.astype(o_ref.dtype)
        return carry

    lax.fori_loop(0, n_chunks, chunk_body, 0)


def _rwkv_scan4_kernel(r_ref, k_ref, v_ref, kk_ref, a_ref, lw_ref, g_ref, lnw_ref, lnb_ref, rk_ref,
                       o_ref, state_ref, y_ref, *, n_chunks, n_quads):
    C = RWKV_CHUNK
    N = RWKV_HEAD
    HQ = RWKV_QUAD
    QW = HQ * N
    t = pl.program_id(2)

    @pl.when(t == 0)
    def _():
        state_ref[...] = jnp.zeros_like(state_ref)

    row = lax.broadcasted_iota(jnp.int32, (C, C), 0)
    col = lax.broadcasted_iota(jnp.int32, (C, C), 1)
    tri_incl = jnp.where(row >= col, 1.0, 0.0).astype(BF16)
    prow = lax.broadcasted_iota(jnp.int32, (C, QW), 0)
    plane = lax.broadcasted_iota(jnp.int32, (C, QW), 1)
    eye_p = jnp.where(prow == plane % C, 1.0, 0.0).astype(F32)
    head_masks = [plane // N == h for h in range(HQ)]
    grow = lax.broadcasted_iota(jnp.int32, (2 * C, 2 * QW), 0)
    gcol = lax.broadcasted_iota(jnp.int32, (2 * C, 2 * QW), 1)
    gmask = (grow % C + grow // C) > (gcol % C)
    srow = lax.broadcasted_iota(jnp.int32, (QW, QW), 0)
    scol = lax.broadcasted_iota(jnp.int32, (QW, QW), 1)
    state_blocks = (srow // N) == (scol // N)
    n_levels = int(math.log2(C))
    n_batch = r_ref.shape[0]
    chains = [(bi, q) for bi in range(n_batch) for q in range(n_quads)]

    head_ones = [jnp.where(m, 1.0, 0.0).astype(BF16) for m in head_masks]

    def bdiag(x):
        xb = x.astype(BF16)
        return jnp.concatenate([xb * one for one in head_ones], axis=0)

    def head_sum(x):
        out = jnp.zeros_like(x)
        for m in head_masks:
            out = jnp.where(m, jnp.sum(jnp.where(m, x, 0.0), axis=-1, keepdims=True), out)
        return out

    def chunk_body(ci, carry):
        rows = pl.ds(pl.multiple_of(ci * C, C), C)
        cum_all = [_dot_2x(tri_incl, lw_ref[bi, rows, :]) for bi in range(n_batch)]
        pre = []
        for bi, q in chains:
            lanes = slice(q * QW, (q + 1) * QW)
            r = r_ref[bi, rows, lanes].astype(F32)
            k = k_ref[bi, rows, lanes].astype(F32)
            v = v_ref[bi, rows, lanes].astype(F32)
            kk = kk_ref[bi, rows, lanes].astype(F32)
            a = a_ref[bi, rows, lanes].astype(F32)
            lw = lw_ref[bi, rows, lanes]
            cum = cum_all[bi][:, lanes]
            wc = cum[C - 1:C, :]
            e_out = jnp.exp(-cum)
            e_end = jnp.exp(wc - cum)
            b = kk * a
            at = -kk * jnp.exp(cum - lw)
            rt = r * jnp.exp(cum)
            pre.append(dict(lanes=lanes, r=r, k=k, v=v, wc=wc, at=at, rt=rt,
                            bh=b * e_out, kh=k * e_out,
                            kb_end=jnp.concatenate([k * e_end, b * e_end], axis=0).astype(BF16)))

        grams = []
        for d in pre:
            lhs = jnp.concatenate([d["at"], d["rt"]], axis=0).astype(BF16)
            rhs = jnp.concatenate([bdiag(d["bh"]), bdiag(d["kh"])], axis=0)
            grams.append(jnp.where(gmask, _dot_nt(lhs, rhs), 0.0))
        lps = [g[0:C, 0:QW] for g in grams]
        bd_vs = [bdiag(d["v"]) for d in pre]
        mvs = [_dot(g[0:C, QW:2 * QW].astype(BF16), bd_v) for g, bd_v in zip(grams, bd_vs)]
        pks = [eye_p + lp for lp in lps]
        lks = [_dot(lp.astype(BF16), bdiag(lp)) for lp in lps]
        for lvl in range(1, n_levels):
            if lvl < n_levels - 1:
                boths = [_dot(jnp.concatenate([pk, lk], axis=0).astype(BF16), bdiag(lk))
                         for pk, lk in zip(pks, lks)]
                pks = [pk + bo[0:C, :] for pk, bo in zip(pks, boths)]
                lks = [bo[C:2 * C, :] for bo in boths]
            else:
                pks = [pk + _dot(pk.astype(BF16), bdiag(lk)) for pk, lk in zip(pks, lks)]
        tws = [_dot(pk.astype(BF16), jnp.concatenate([bdiag(mv), bdiag(d["at"])], axis=1))
               for pk, mv, d in zip(pks, mvs, pre)]

        sts = [state_ref[bi, q] for bi, q in chains]
        zs = [_dot_nt(jnp.concatenate([tw[:, QW:2 * QW], d["rt"]], axis=0).astype(BF16), st.astype(BF16))
              for tw, d, st in zip(tws, pre, sts)]
        us = [tw[:, 0:QW] + z[0:C, :] for tw, z in zip(tws, zs)]
        yss = [_dot(g[C:2 * C, :].astype(BF16), jnp.concatenate([bdiag(u), bd_v], axis=0))
               for g, u, bd_v in zip(grams, us, bd_vs)]
        upds = [_dot(jnp.concatenate([d["v"], u], axis=0).T.astype(BF16), d["kb_end"])
                for d, u in zip(pre, us)]
        for i, (bi, q) in enumerate(chains):
            d = pre[i]
            state_ref[bi, q] = sts[i] * jnp.exp(d["wc"]) + jnp.where(state_blocks, upds[i], 0.0)
            y_ref[bi, :, d["lanes"]] = zs[i][C:2 * C, :] + yss[i]
        return carry

    def finish(ci):
        rows = pl.ds(pl.multiple_of(ci * C, C), C)
        for bi, q in chains:
            lanes = slice(q * QW, (q + 1) * QW)
            y = y_ref[bi, :, lanes]
            mean = head_sum(y) * (1.0 / N)
            yc = y - mean
            var = head_sum(yc * yc) * (1.0 / N)
            yn = yc * lax.rsqrt(var + RWKV_LN_EPS) * lnw_ref[:, lanes] + lnb_ref[:, lanes]
            r = r_ref[bi, rows, lanes].astype(F32)
            k = k_ref[bi, rows, lanes].astype(F32)
            v = v_ref[bi, rows, lanes].astype(F32)
            bonus = head_sum(r * k * rk_ref[:, lanes]) * v
            out = (yn + bonus) * g_ref[bi, rows, lanes].astype(F32)
            o_ref[bi, rows, lanes] = out.astype(o_ref.dtype)

    y_ref[...] = jnp.zeros_like(y_ref)

    def loop_body(ci, carry):
        finish(jnp.maximum(ci - 1, 0))
        return chunk_body(ci, carry)

    lax.fori_loop(0, n_chunks, loop_body, 0)
    finish(n_chunks - 1)


def _rwkv_scan(r, k, v, kk, a, lw, g, lnw, lnb, rk, tb, n_quads, nb):
    B, S, D = r.shape
    QW = RWKV_QUAD * RWKV_HEAD
    W = n_quads * QW
    blk = pl.BlockSpec((nb, tb, W), lambda b, j, t: (b, t, j))
    par = pl.BlockSpec((1, W), lambda b, j, t: (0, j))
    return pl.pallas_call(
        functools.partial(_rwkv_scan4_kernel, n_chunks=tb // RWKV_CHUNK, n_quads=n_quads),
        grid=(B // nb, D // W, S // tb),
        in_specs=[blk] * 7 + [par] * 3,
        out_specs=blk,
        out_shape=jax.ShapeDtypeStruct((B, S, D), BF16),
        scratch_shapes=[pltpu.VMEM((nb, n_quads, QW, QW), F32), pltpu.VMEM((nb, RWKV_CHUNK, W), F32)],
        compiler_params=_cparams(("parallel", "parallel", "arbitrary")),
    )(r, k, v, kk, a, lw, g, lnw, lnb, rk)


def _rwkv_out_kernel(y_ref, w_ref, h_ref, o_ref):
    o_ref[...] = h_ref[...] + _dot(y_ref[...], w_ref[...])


def _rwkv_out(y, wo, h2, tm):
    T, D = h2.shape
    return pl.pallas_call(
        _rwkv_out_kernel,
        grid=(T // tm,),
        in_specs=[pl.BlockSpec((tm, D), lambda i: (i, 0)), pl.BlockSpec((D, D), lambda i: (0, 0)),
                  pl.BlockSpec((tm, D), lambda i: (i, 0))],
        out_specs=pl.BlockSpec((tm, D), lambda i: (i, 0)),
        out_shape=jax.ShapeDtypeStruct((T, D), F32),
        compiler_params=_cparams(("parallel",)),
    )(y, wo, h2)


def _router_kernel(h_ref, nw_ref, wr_ref, br_ref, t_ref, cw_ref):
    tn = _rms(h_ref[...], nw_ref[...])
    t_ref[...] = tn.astype(t_ref.dtype)
    logits = _dot_x3(tn, wr_ref[...]) + br_ref[...]
    lane = lax.broadcasted_iota(jnp.int32, logits.shape, 1).astype(F32)
    neg = jnp.float32(-jnp.inf)
    big = jnp.float32(1 << 20)
    is_g = lane < MOE_GROUPS
    gl = jnp.where(is_g, logits, neg)
    gmax = jnp.max(gl, axis=-1, keepdims=True)
    g_idx = jnp.min(jnp.where(gl == gmax, lane, big), axis=-1, keepdims=True)
    g_gate = 1.0 / jnp.sum(jnp.where(is_g, jnp.exp(logits - gmax), 0.0), axis=-1, keepdims=True)
    lo = MOE_GROUPS + g_idx * MOE_EPG
    sel = (lane >= lo) & (lane < lo + MOE_EPG)
    el = jnp.where(sel, logits, neg)
    m1 = jnp.max(el, axis=-1, keepdims=True)
    i1 = jnp.min(jnp.where(el == m1, lane, big), axis=-1, keepdims=True)
    el2 = jnp.where(lane == i1, neg, el)
    m2 = jnp.max(el2, axis=-1, keepdims=True)
    i2 = jnp.min(jnp.where(el2 == m2, lane, big), axis=-1, keepdims=True)
    e21 = jnp.exp(m2 - m1)
    w1 = g_gate / (1.0 + e21)
    w2 = w1 * e21
    shift = g_idx * MOE_EPG
    cw_ref[...] = jnp.where(lane == 0.0, g_idx,
                            jnp.where(lane == i1 - shift, w1, jnp.where(lane == i2 - shift, w2, 0.0)))


def _router(h2, nw, wr, br, tm):
    T, D = h2.shape
    return pl.pallas_call(
        _router_kernel,
        grid=(T // tm,),
        in_specs=[pl.BlockSpec((tm, D), lambda i: (i, 0)), pl.BlockSpec((1, D), lambda i: (0, 0)),
                  pl.BlockSpec((D, ROUTER_LANES), lambda i: (0, 0)),
                  pl.BlockSpec((1, ROUTER_LANES), lambda i: (0, 0))],
        out_specs=[pl.BlockSpec((tm, D), lambda i: (i, 0)),
                   pl.BlockSpec((tm, ROUTER_LANES), lambda i: (i, 0))],
        out_shape=[jax.ShapeDtypeStruct((T, D), BF16),
                   jax.ShapeDtypeStruct((T, ROUTER_LANES), F32)],
        compiler_params=_cparams(("parallel",)),
    )(h2, nw, wr, br)


def _dot_tn(a, b):
    return lax.dot_general(a, b, (((0,), (0,)), ((), ())), preferred_element_type=F32)


def _experts_kernel(t_ref, cw_ref, h_ref, w13_ref, w2_ref, fnw_ref, o_ref,
                    pt_ref, xs_ref, cws_ref, oacc_ref, meta_ref, *, final_norm):
    tm = t_ref.shape[0]
    cap = pt_ref.shape[1]
    R = MOE_SUB
    F = MOE_D_FF
    g = pl.program_id(1)
    part = pl.program_id(2)
    n_parts = MOE_EPG // MOE_EXPERTS_PER_STEP

    @pl.when((g == 0) & (part == 0))
    def _():
        cw = cw_ref[...]
        lane_t = lax.broadcasted_iota(jnp.int32, (tm, ROUTER_LANES), 1).astype(F32)
        memb = (cw[:, 0:1] == lane_t) & (lane_t < MOE_GROUPS)
        membf = jnp.where(memb, 1.0, 0.0)
        tr = lax.broadcasted_iota(jnp.int32, (tm, tm), 0)
        tc = lax.broadcasted_iota(jnp.int32, (tm, tm), 1)
        before = jnp.where(tr > tc, 1.0, 0.0).astype(BF16)
        ranks = _dot(before, membf.astype(BF16))
        cnt = jnp.sum(membf, axis=0, keepdims=True)
        nsub = jnp.floor((cnt + (R - 1.0)) * (1.0 / R))
        ur = lax.broadcasted_iota(jnp.int32, (ROUTER_LANES, ROUTER_LANES), 0)
        uc = lax.broadcasted_iota(jnp.int32, (ROUTER_LANES, ROUTER_LANES), 1)
        prefix = jnp.where(ur < uc, 1.0, 0.0).astype(BF16)
        base = _dot_x2(jnp.broadcast_to(nsub * R, (8, ROUTER_LANES)), prefix)[0:1, :]
        pos = jnp.sum(jnp.where(memb, ranks + base, 0.0), axis=-1, keepdims=True)
        col = lax.broadcasted_iota(jnp.int32, (tm, cap), 1).astype(F32)
        pt = jnp.where(pos == col, 1.0, 0.0).astype(BF16)
        pt_ref[...] = pt
        cw_hi, cw_lo = _split2(cw)
        D = t_ref.shape[1]
        gathered = _dot_tn(pt, jnp.concatenate([t_ref[...], cw_hi, cw_lo], axis=1))
        xs_ref[...] = gathered[:, 0:D].astype(BF16)
        cws_ref[...] = gathered[:, D:D + ROUTER_LANES] + gathered[:, D + ROUTER_LANES:D + 2 * ROUTER_LANES]
        oacc_ref[...] = jnp.zeros_like(oacc_ref)
        meta_ref[0:1, :] = nsub
        meta_ref[1:2, :] = base

    lane1 = lax.broadcasted_iota(jnp.int32, (1, ROUTER_LANES), 1)
    n_sub = jnp.sum(jnp.where(lane1 == g, meta_ref[0:1, :], 0.0)).astype(jnp.int32)
    row0 = jnp.sum(jnp.where(lane1 == g, meta_ref[1:2, :], 0.0)).astype(jnp.int32)
    def run_rows(start, n_rows):
        rows = pl.ds(pl.multiple_of(start, R), n_rows)
        lane_r = lax.broadcasted_iota(jnp.int32, (n_rows, ROUTER_LANES), 1)
        xs = xs_ref[rows, :]
        cws = cws_ref[rows, :]
        acc = oacc_ref[rows, :]
        for j in range(MOE_EXPERTS_PER_STEP):
            h13 = _dot(xs, w13_ref[j])
            ce = jnp.sum(jnp.where(lane_r == MOE_GROUPS + part * MOE_EXPERTS_PER_STEP + j, cws, 0.0),
                         axis=-1, keepdims=True)
            he = (_silu(h13[:, 0:F]) * h13[:, F:2 * F] * ce).astype(BF16)
            acc = acc + _dot(he, w2_ref[j])
        oacc_ref[rows, :] = acc

    n_pairs = n_sub // 2

    def pair_body(s, carry):
        run_rows(row0 + s * (2 * R), 2 * R)
        return carry

    lax.fori_loop(0, n_pairs, pair_body, 0)

    @pl.when(n_sub % 2 == 1)
    def _():
        run_rows(row0 + n_pairs * (2 * R), R)

    @pl.when((g == MOE_GROUPS - 1) & (part == n_parts - 1))
    def _():
        out = h_ref[...] + _dot(pt_ref[...], oacc_ref[...].astype(BF16))
        if final_norm:
            out = _rms(out, fnw_ref[...])
        o_ref[...] = out


def _experts(t, cw, h2, w13, w2, fnw, tm, final_norm):
    T, D = h2.shape
    F = MOE_D_FF
    E = MOE_EXPERTS_PER_STEP
    n_parts = MOE_EPG // E
    cap = tm + MOE_GROUPS * MOE_SUB
    return pl.pallas_call(
        functools.partial(_experts_kernel, final_norm=final_norm),
        grid=(T // tm, MOE_GROUPS, n_parts),
        in_specs=[
            pl.BlockSpec((tm, D), lambda i, g, p: (i, 0)),
            pl.BlockSpec((tm, ROUTER_LANES), lambda i, g, p: (i, 0)),
            pl.BlockSpec((tm, D), lambda i, g, p: (i, 0)),
            pl.BlockSpec((E, D, 2 * F), lambda i, g, p: (g * n_parts + p, 0, 0)),
            pl.BlockSpec((E, F, D), lambda i, g, p: (g * n_parts + p, 0, 0)),
            pl.BlockSpec((1, D), lambda i, g, p: (0, 0)),
        ],
        out_specs=pl.BlockSpec((tm, D), lambda i, g, p: (i, 0)),
        out_shape=jax.ShapeDtypeStruct((T, D), F32),
        scratch_shapes=[
            pltpu.VMEM((tm, cap), BF16),
            pltpu.VMEM((cap, D), BF16),
            pltpu.VMEM((cap, ROUTER_LANES), F32),
            pltpu.VMEM((cap, D), F32),
            pltpu.VMEM((8, ROUTER_LANES), F32),
        ],
        compiler_params=_cparams(("parallel", "arbitrary", "arbitrary")),
    )(t, cw, h2, w13, w2, fnw)


def _moe(h2, nw, wg, bg, we, be, w1, w3, w2, fnw, final_norm, tm):
    D = D_MODEL
    pad = ROUTER_LANES - MOE_GROUPS - MOE_EXPERTS
    wr = jnp.concatenate([wg, we, jnp.zeros((D, pad), F32)], axis=1)
    br = jnp.concatenate([bg, be, jnp.zeros((pad,), F32)]).reshape(1, ROUTER_LANES)
    t, cw = _router(h2, nw.reshape(1, D), wr, br, tm)
    w13 = jnp.concatenate([w1, w3], axis=-1).astype(BF16)
    return _experts(t, cw, h2, w13, w2.astype(BF16), fnw.reshape(1, D), tm, final_norm)


def _pick(n, prefs):
    for p in prefs:
        if n % p == 0:
            return p
    return n


def _ssd_retention_layer(h2, B, S, nw, w_in, conv_w, conv_b, dt_bias, a_log, d_skip, norm_w, w_out):
    D = D_MODEL
    T = B * S
    o_x, o_bc, o_dt, o_q, o_k, o_v, o_g = 1024, 2048, 2560, 2576, 3088, 3600, 4624
    w_main = jnp.concatenate([w_in[:, 0:o_x], w_in[:, o_x:o_bc], w_in[:, o_v:o_g], w_in[:, o_g:o_g + D],
                              w_in[:, o_bc:o_dt], w_in[:, o_q:o_k], w_in[:, o_k:o_v]], axis=1).astype(BF16)
    w_dt = jnp.pad(w_in[:, o_dt:o_q], ((0, 0), (0, LANES - SSD_HEADS))).astype(BF16)
    tm = _pick(T, (1024, 512, 256, 128))
    proj, dt = _inproj(h2, nw.reshape(1, D), w_main, w_dt, tm, 1408)
    proj3 = proj.reshape(B, S, proj.shape[1])
    dt3 = dt.reshape(B, S, LANES)

    head_of_channel = jnp.arange(D) // SSD_HEAD_DIM
    expand = (jnp.arange(LANES)[:, None] == head_of_channel[None, :]).astype(BF16)
    pad16 = lambda x: jnp.pad(x, (0, LANES - SSD_HEADS)).reshape(1, LANES)
    ya = _ssd(proj3, dt3, conv_w[:, :D], conv_b[:D].reshape(1, D), conv_w[:, D:], conv_b[D:].reshape(1, 512),
              pad16(dt_bias), pad16(a_log), expand, jnp.repeat(d_skip, SSD_HEAD_DIM).reshape(1, D),
              norm_w.reshape(1, D))

    C = RET_CHUNK
    half = RET_QK_HEAD // 2
    pos = jnp.arange(S, dtype=F32)
    inv_freq = ROPE_BASE ** (-jnp.arange(half, dtype=F32) / half)
    ang = pos[:, None] * inv_freq[None, :]
    cc = jnp.concatenate([jnp.cos(ang), jnp.cos(ang)], axis=1)
    ss = jnp.concatenate([-jnp.sin(ang), jnp.sin(ang)], axis=1)
    log_gamma = jnp.log(1.0 - 2.0 ** (-5.0 - jnp.arange(RET_HEADS, dtype=F32)))
    idx = jnp.arange(C, dtype=F32)
    diff = idx[:, None] - idx[None, :]
    dmask = jnp.where(diff[None] >= 0, jnp.exp(jnp.maximum(diff, 0.0)[None] * log_gamma[:, None, None]), 0.0)
    qdec = jnp.repeat(jnp.exp((idx[:, None] + 1.0) * log_gamma[None, :]), RET_V_HEAD, axis=1)
    kdec = jnp.repeat(jnp.exp((C - 1.0 - idx[:, None]) * log_gamma[None, :]), RET_QK_HEAD, axis=1)
    cdec = jnp.repeat(jnp.exp(C * log_gamma), RET_V_HEAD).reshape(1, D)
    yb = _retention(proj3, cc, ss, dmask, qdec, kdec, cdec)

    w_out_b = w_out.astype(BF16)
    return _outproj(ya.reshape(T, D), yb.reshape(T, D), w_out_b[:D], w_out_b[D:], h2, tm)


def _rwkv_layer(h2, B, S, nw, mu, w_r, w_k, w_v, w_o, w0, w1, w2, a0, a1, a2, g1, g2, k_k, k_a, r_k, lnx_w, lnx_b):
    D = D_MODEL
    T = B * S
    padc = lambda w, n: jnp.pad(w, ((0, 0), (0, n - w.shape[1]))).astype(BF16)
    padr = lambda w, n: jnp.pad(w, ((0, n - w.shape[0]), (0, 0))).astype(BF16)
    head_of_channel = jnp.arange(D) // RWKV_HEAD
    hs = (head_of_channel[:, None] == jnp.arange(LANES)[None, :]).astype(BF16)
    row = lambda x: x.reshape(1, D)
    tm = _pick(S, (512, 256, 128))
    r, k, v, kk, a, lw, g = _rwkv_pre(
        h2, row(nw), mu, w_r.astype(BF16), w_k.astype(BF16), w_v.astype(BF16),
        padc(w1, RWKV_LORA_PAD), padr(w2, RWKV_LORA_PAD), row(w0),
        padc(a1, RWKV_LORA_PAD), padr(a2, RWKV_LORA_PAD), row(a0),
        padc(g1, RWKV_GATE_PAD), padr(g2, RWKV_GATE_PAD), row(k_k), row(k_a), hs, hs.T, S, tm)
    sh = lambda x: x.reshape(B, S, D)
    tb = _pick(S, (512, 256, 128, 64))
    y = _rwkv_scan(sh(r), sh(k), sh(v), sh(kk), sh(a), sh(lw), sh(g),
                   row(lnx_w), row(lnx_b), r_k.reshape(1, D), tb, RWKV_HEADS // RWKV_QUAD,
                   _pick(B, (RWKV_SEQS_PER_STEP, 1)))
    return _rwkv_out(y.reshape(T, D), w_o.astype(BF16), h2, _pick(T, (1024, 512, 256, 128)))


def kernel(x, norm_mix_w, norm_ffn_w, norm_final_w, w_in_e, ssd_conv_w, ssd_conv_b, ssd_dt_bias, ssd_a_log, ssd_d, ssd_norm_w, w_out_e, rw_mu, rw_wr, rw_wk, rw_wv, rw_wo, rw_w0, rw_w1, rw_w2, rw_a0, rw_a1, rw_a2, rw_g1, rw_g2, rw_kk, rw_ka, rw_rk, rw_lnx_w, rw_lnx_b, moe_wg, moe_bg, moe_we, moe_be, moe_w1, moe_w3, moe_w2):
    B, S, D = x.shape
    T = B * S
    depth = norm_mix_w.shape[0]
    h = x.reshape(T, D)
    tm_moe = _pick(T, (1024, 512, 256, 128))
    for layer in range(depth):
        i = layer // 2
        if layer % 2 == 0:
            h = _ssd_retention_layer(h, B, S, norm_mix_w[layer], w_in_e[i], ssd_conv_w[i], ssd_conv_b[i],
                                     ssd_dt_bias[i], ssd_a_log[i], ssd_d[i], ssd_norm_w[i], w_out_e[i])
        else:
            h = _rwkv_layer(h, B, S, norm_mix_w[layer], rw_mu[i], rw_wr[i], rw_wk[i], rw_wv[i], rw_wo[i],
                            rw_w0[i], rw_w1[i], rw_w2[i], rw_a0[i], rw_a1[i], rw_a2[i], rw_g1[i], rw_g2[i],
                            rw_kk[i], rw_ka[i], rw_rk[i], rw_lnx_w[i], rw_lnx_b[i])
        h = _moe(h, norm_ffn_w[layer], moe_wg[layer], moe_bg[layer], moe_we[layer], moe_be[layer],
                 moe_w1[layer], moe_w3[layer], moe_w2[layer], norm_final_w,
                 final_norm=(layer == depth - 1), tm=tm_moe)
    return h.reshape(B, S, D)
```

```python
import functools
import math

import jax
import jax.numpy as jnp
from jax import lax
from jax.experimental import pallas as pl
from jax.experimental.pallas import tpu as pltpu

F32 = jnp.float32
BF16 = jnp.bfloat16

D_MODEL = 1024
RMS_EPS = 1e-6
SSD_HEADS = 16
SSD_HEAD_DIM = 64
SSD_GROUPS = 2
SSD_STATE = 128
SSD_CONV = 4
SSD_CHUNK = 128
SSD_CONV_TAIL = 16
SSD_NORM_EPS = 1e-5
SSD_GROUP_WIDTH = D_MODEL // SSD_GROUPS
RET_HEADS = 4
RET_QK_HEAD = 128
RET_V_HEAD = 256
RET_CHUNK = 128
ROPE_BASE = 10000.0
RWKV_HEAD = 64
RWKV_HEADS = 16
RWKV_LN_EPS = 64e-5
RWKV_CHUNK = 64
RWKV_QUAD = 4
RWKV_SEQS_PER_STEP = 2
RWKV_LORA_PAD = 128
RWKV_GATE_PAD = 256
RWKV_PRE_SPLIT = 2
MOE_GROUPS = 4
MOE_EPG = 4
MOE_EXPERTS = 16
MOE_D_FF = 512
ROUTER_LANES = 128
MOE_SUB = 64
MOE_SUBS_PER_PASS = 4
MOE_MAX_PASS = 6
MOE_EXPERTS_PER_STEP = 2

LANES = 128
VMEM_LIMIT_BYTES = 56 * 1024 * 1024


def _cparams(sem):
    return pltpu.CompilerParams(dimension_semantics=sem, vmem_limit_bytes=VMEM_LIMIT_BYTES)


def _dot(a, b):
    return jnp.dot(a, b, preferred_element_type=F32)


def _dot_nt(a, b):
    return lax.dot_general(a, b, (((1,), (1,)), ((), ())), preferred_element_type=F32)


def _split2(x):
    hi = x.astype(BF16)
    lo = (x - hi.astype(F32)).astype(BF16)
    return hi, lo


def _dot_x2(x, w_exact):
    hi, lo = _split2(x)
    return _dot(hi, w_exact) + _dot(lo, w_exact)


def _dot_2x(w_exact, x):
    hi, lo = _split2(x)
    return _dot(w_exact, hi) + _dot(w_exact, lo)


def _dot_x3(x, w):
    xh, xl = _split2(x)
    wh, wl = _split2(w)
    return _dot(xh, wh) + _dot(xl, wh) + _dot(xh, wl)


def _sigmoid(x):
    return 0.5 * jnp.tanh(0.5 * x) + 0.5


def _silu(x):
    hx = 0.5 * x
    return hx * jnp.tanh(hx) + hx


def _softplus(x):
    return jnp.maximum(x, 0.0) + jnp.log(1.0 + jnp.exp(-jnp.abs(x)))


def _rms(x, w, eps=RMS_EPS):
    return x * lax.rsqrt(jnp.mean(x * x, axis=-1, keepdims=True) + eps) * w


def _inproj_kernel(x_ref, nw_ref, w_ref, wdt_ref, o_ref, dt_ref, u_ref):
    @pl.when(pl.program_id(1) == 0)
    def _():
        ub = _rms(x_ref[...], nw_ref[...]).astype(BF16)
        u_ref[...] = ub
        dt_ref[...] = _dot(ub, wdt_ref[...])

    o_ref[...] = _dot(u_ref[...], w_ref[...]).astype(o_ref.dtype)


def _inproj(h2, nw, w_main, w_dt, tm, tn):
    T, D = h2.shape
    N = w_main.shape[1]
    return pl.pallas_call(
        _inproj_kernel,
        grid=(T // tm, N // tn),
        in_specs=[
            pl.BlockSpec((tm, D), lambda i, j: (i, 0)),
            pl.BlockSpec((1, D), lambda i, j: (0, 0)),
            pl.BlockSpec((D, tn), lambda i, j: (0, j)),
            pl.BlockSpec((D, LANES), lambda i, j: (0, 0)),
        ],
        out_specs=[
            pl.BlockSpec((tm, tn), lambda i, j: (i, j)),
            pl.BlockSpec((tm, LANES), lambda i, j: (i, 0)),
        ],
        out_shape=[
            jax.ShapeDtypeStruct((T, N), BF16),
            jax.ShapeDtypeStruct((T, LANES), F32),
        ],
        scratch_shapes=[pltpu.VMEM((tm, D), BF16)],
        compiler_params=_cparams(("parallel", "arbitrary")),
    )(h2, nw, w_main, w_dt)


def _ssd_kernel(z_ref, x_ref, bc_ref, dt_ref, cwx_ref, cbx_ref, cwb_ref, cbb_ref,
                dtb_ref, alog_ref, e_ref, dskip_ref, nw_ref, o_ref,
                state_ref, xtail, bctail, ybuf):
    L = SSD_CHUNK
    c = pl.program_id(1)

    @pl.when(c == 0)
    def _():
        state_ref[...] = jnp.zeros_like(state_ref)
        xtail[...] = jnp.zeros_like(xtail)
        bctail[...] = jnp.zeros_like(bctail)

    TAIL = xtail.shape[0]
    srow = lax.broadcasted_iota(jnp.int32, (SSD_CONV * L, TAIL + L), 0)
    scol = lax.broadcasted_iota(jnp.int32, (SSD_CONV * L, TAIL + L), 1)
    shift = jnp.where(scol == (srow % L) + (srow // L) + (TAIL - SSD_CONV + 1), 1.0, 0.0).astype(BF16)

    def conv(cur_ref, tail, w_ref, b_ref):
        cur = cur_ref[0]
        taps = _dot(shift, jnp.concatenate([tail[...], cur], axis=0))
        acc = b_ref[...] + w_ref[0:1, :] * taps[0:L, :]
        for k in range(1, SSD_CONV):
            acc = acc + w_ref[k:k + 1, :] * taps[k * L:(k + 1) * L, :]
        tail[...] = cur[L - TAIL:L, :]
        return _silu(acc)

    xs = conv(x_ref, xtail, cwx_ref, cbx_ref)
    bc = conv(bc_ref, bctail, cwb_ref, cbb_ref)

    dt = _softplus(dt_ref[0] + dtb_ref[...])
    a = dt * (-jnp.exp(alog_ref[...]))
    row = lax.broadcasted_iota(jnp.int32, (L, L), 0)
    col = lax.broadcasted_iota(jnp.int32, (L, L), 1)
    causal = row >= col
    tri = jnp.where(causal, 1.0, 0.0).astype(BF16)
    acum = _dot_2x(tri, a)
    acum_t = acum.T
    e = e_ref[...]
    dt_full = _dot_x2(dt, e)
    acum_full = _dot_x2(acum, e)
    xdt = xs * dt_full
    alast_full = acum_full[L - 1:L, :]
    decay_in = jnp.exp(acum_full)
    xdt_end = (xdt * jnp.exp(alast_full - acum_full)).astype(BF16)
    xdt_b = xdt.astype(BF16)
    lane = lax.broadcasted_iota(jnp.int32, (L, LANES), 1)
    first_head = lane < SSD_HEAD_DIM

    GW = SSD_GROUP_WIDTH
    for g in range(SSD_GROUPS):
        bg = bc[:, g * SSD_STATE:(g + 1) * SSD_STATE]
        cg = bc[:, (SSD_GROUPS + g) * SSD_STATE:(SSD_GROUPS + g + 1) * SSD_STATE].astype(BF16)
        cb = _dot_nt(cg, bg.astype(BF16))
        st = state_ref[g]
        y_off = _dot(cg, st.astype(BF16)) * decay_in[:, g * GW:(g + 1) * GW]
        for p in range(GW // LANES):
            xp = xdt_b[:, g * GW + p * LANES:g * GW + (p + 1) * LANES]
            ys = []
            for s in range(2):
                hd = g * (SSD_HEADS // SSD_GROUPS) + 2 * p + s
                seg = acum[:, hd:hd + 1] - acum_t[hd:hd + 1, :]
                dec = jnp.where(causal, jnp.exp(seg), 0.0)
                ys.append(_dot((cb * dec).astype(BF16), xp))
            yd = jnp.where(first_head, ys[0], ys[1])
            lo = g * GW + p * LANES
            ybuf[:, lo:lo + LANES] = yd + y_off[:, p * LANES:(p + 1) * LANES]
        bg_t = bg.T.astype(BF16)
        state_ref[g] = st * jnp.exp(alast_full[:, g * GW:(g + 1) * GW]) + _dot(
            bg_t, xdt_end[:, g * GW:(g + 1) * GW])

    y = ybuf[...] + xs * dskip_ref[...]
    y = y * _silu(z_ref[0].astype(F32))
    for g in range(SSD_GROUPS):
        yg = y[:, g * GW:(g + 1) * GW]
        yg = yg * lax.rsqrt(jnp.mean(yg * yg, axis=-1, keepdims=True) + SSD_NORM_EPS)
        o_ref[0, :, g * GW:(g + 1) * GW] = (yg * nw_ref[:, g * GW:(g + 1) * GW]).astype(o_ref.dtype)


def _ssd(proj3, dt3, cwx, cbx, cwb, cbb, dtb, alog, expand, dskip, nw):
    B, S, _ = proj3.shape
    L = SSD_CHUNK
    D = D_MODEL
    full = lambda shape: pl.BlockSpec(shape, lambda b, c: (0,) * len(shape))
    return pl.pallas_call(
        _ssd_kernel,
        grid=(B, S // L),
        in_specs=[
            pl.BlockSpec((1, L, D), lambda b, c: (b, c, 0)),
            pl.BlockSpec((1, L, D), lambda b, c: (b, c, 1)),
            pl.BlockSpec((1, L, 512), lambda b, c: (b, c, 8)),
            pl.BlockSpec((1, L, LANES), lambda b, c: (b, c, 0)),
            full((SSD_CONV, D)), full((1, D)), full((SSD_CONV, 512)), full((1, 512)),
            full((1, LANES)), full((1, LANES)), full((LANES, D)), full((1, D)), full((1, D)),
        ],
        out_specs=pl.BlockSpec((1, L, D), lambda b, c: (b, c, 0)),
        out_shape=jax.ShapeDtypeStruct((B, S, D), BF16),
        scratch_shapes=[
            pltpu.VMEM((SSD_GROUPS, SSD_STATE, SSD_GROUP_WIDTH), F32),
            pltpu.VMEM((SSD_CONV_TAIL, D), BF16),
            pltpu.VMEM((SSD_CONV_TAIL, 512), BF16),
            pltpu.VMEM((L, D), F32),
        ],
        compiler_params=_cparams(("parallel", "arbitrary")),
    )(proj3, proj3, proj3, dt3, cwx, cbx, cwb, cbb, dtb, alog, expand, dskip, nw)


def _ret_kernel(q_ref, k_ref, v_ref, g_ref, cc_ref, ss_ref, dmask_ref, qdec_ref, kdec_ref,
                cdec_ref, o_ref, r_ref):
    c = pl.program_id(1)

    @pl.when(c == 0)
    def _():
        r_ref[...] = jnp.zeros_like(r_ref)

    cc = cc_ref[...]
    ss = ss_ref[...]
    dk, dv = RET_QK_HEAD, RET_V_HEAD
    half = dk // 2

    def rope(x):
        return x * cc + pltpu.roll(x, half, 1) * ss

    heads = range(RET_HEADS)
    qs = [rope(q_ref[0, :, hd * dk:(hd + 1) * dk].astype(F32)) for hd in heads]
    ks = [rope(k_ref[0, :, hd * dk:(hd + 1) * dk].astype(F32)) * (dk ** -0.5) for hd in heads]
    vs = [v_ref[0, :, hd * dv:(hd + 1) * dv] for hd in heads]
    qbs = [q.astype(BF16) for q in qs]
    ss = [_dot_nt(qbs[hd], ks[hd].astype(BF16)) * dmask_ref[hd] for hd in heads]
    r_olds = [r_ref[hd] for hd in heads]
    cross = [_dot(qbs[hd], r_olds[hd].astype(BF16)) for hd in heads]
    kd_ts = [(ks[hd] * kdec_ref[:, hd * dk:(hd + 1) * dk]).T.astype(BF16) for hd in heads]
    upds = [_dot(kd_ts[hd], vs[hd]) for hd in heads]
    inner = [_dot(ss[hd].astype(BF16), vs[hd]) for hd in heads]
    for hd in heads:
        r_ref[hd] = r_olds[hd] * cdec_ref[:, hd * dv:(hd + 1) * dv] + upds[hd]
        y = inner[hd] + cross[hd] * qdec_ref[:, hd * dv:(hd + 1) * dv]
        y = y * lax.rsqrt(jnp.mean(y * y, axis=-1, keepdims=True) + RMS_EPS)
        y = y * _silu(g_ref[0, :, hd * dv:(hd + 1) * dv].astype(F32))
        o_ref[0, :, hd * dv:(hd + 1) * dv] = y.astype(o_ref.dtype)


def _retention(proj3, cc, ss, dmask, qdec, kdec, cdec):
    B, S, _ = proj3.shape
    C = RET_CHUNK
    D = D_MODEL
    full = lambda shape: pl.BlockSpec(shape, lambda b, c: (0,) * len(shape))
    return pl.pallas_call(
        _ret_kernel,
        grid=(B, S // C),
        in_specs=[
            pl.BlockSpec((1, C, 512), lambda b, c: (b, c, 9)),
            pl.BlockSpec((1, C, 512), lambda b, c: (b, c, 10)),
            pl.BlockSpec((1, C, D), lambda b, c: (b, c, 2)),
            pl.BlockSpec((1, C, D), lambda b, c: (b, c, 3)),
            pl.BlockSpec((C, RET_QK_HEAD), lambda b, c: (c, 0)),
            pl.BlockSpec((C, RET_QK_HEAD), lambda b, c: (c, 0)),
            full((RET_HEADS, C, C)), full((C, D)), full((C, 512)), full((1, D)),
        ],
        out_specs=pl.BlockSpec((1, C, D), lambda b, c: (b, c, 0)),
        out_shape=jax.ShapeDtypeStruct((B, S, D), BF16),
        scratch_shapes=[pltpu.VMEM((RET_HEADS, RET_QK_HEAD, RET_V_HEAD), F32)],
        compiler_params=_cparams(("parallel", "arbitrary")),
    )(proj3, proj3, proj3, proj3, cc, ss, dmask, qdec, kdec, cdec)


def _outproj_kernel(ya_ref, yb_ref, wa_ref, wb_ref, h_ref, nw_ref, wr_ref, br_ref, o_ref, t_ref, cw_ref):
    h = h_ref[...] + _dot(ya_ref[...], wa_ref[...]) + _dot(yb_ref[...], wb_ref[...])
    o_ref[...] = h
    _route_rows(h, nw_ref, wr_ref, br_ref, t_ref, cw_ref)


def _outproj(ya, yb, wa, wb, h2, router, tm):
    T, D = h2.shape
    r_in, r_out, r_shapes = _router_specs(tm, D)
    return pl.pallas_call(
        _outproj_kernel,
        grid=(T // tm,),
        in_specs=[
            pl.BlockSpec((tm, D), lambda i: (i, 0)),
            pl.BlockSpec((tm, D), lambda i: (i, 0)),
            pl.BlockSpec((D, D), lambda i: (0, 0)),
            pl.BlockSpec((D, D), lambda i: (0, 0)),
            pl.BlockSpec((tm, D), lambda i: (i, 0)),
        ] + r_in,
        out_specs=[pl.BlockSpec((tm, D), lambda i: (i, 0))] + r_out,
        out_shape=[jax.ShapeDtypeStruct((T, D), F32)] + r_shapes(T),
        compiler_params=_cparams(("parallel",)),
    )(ya, yb, wa, wb, h2, *router)


def _rwkv_pre_kernel(h_ref, hp_ref, nw_ref, mu_ref, wr_ref, wk_ref, wv_ref,
                     w1_ref, w2_ref, w0_ref, a1_ref, a2_ref, a0_ref, g1_ref, g2_ref,
                     kkw_ref, kaw_ref, hs_ref, hst_ref,
                     r_o, k_o, v_o, kk_o, a_o, lw_o, g_o, ubuf, xxbuf, *, tiles_per_seq):
    tm = h_ref.shape[0]
    i = pl.program_id(0)
    nw = nw_ref[...]
    u = _rms(h_ref[...], nw)
    up = _rms(hp_ref[...], nw)
    seq_start = (i % tiles_per_seq) == 0
    ubuf[8:8 + tm, :] = u
    ubuf[0:8, :] = jnp.where(seq_start, 0.0, up)
    xxbuf[...] = ubuf[7:7 + tm, :] - u

    def mix(j, lo, n):
        return (ubuf[8 + lo:8 + lo + n, :] + xxbuf[lo:lo + n, :] * mu_ref[j:j + 1, :]).astype(BF16)

    n = tm // RWKV_PRE_SPLIT
    parts = [s * n for s in range(RWKV_PRE_SPLIT)]
    first = []
    for lo in parts:
        first.append(dict(
            r=_dot(mix(0, lo, n), wr_ref[...]), k=_dot(mix(2, lo, n), wk_ref[...]),
            v=_dot(mix(3, lo, n), wv_ref[...]), w1=_dot(mix(1, lo, n), w1_ref[...]),
            a1=_dot(mix(4, lo, n), a1_ref[...]), g1=_dot(mix(5, lo, n), g1_ref[...])))
    second = []
    for f in first:
        kk = f["k"] * kkw_ref[...]
        second.append(dict(
            wl=w0_ref[...] + _dot(jnp.tanh(f["w1"]).astype(BF16), w2_ref[...]),
            al=a0_ref[...] + _dot(f["a1"].astype(BF16), a2_ref[...]),
            g=_dot(_sigmoid(f["g1"]).astype(BF16), g2_ref[...]),
            kk=kk, ssq=_dot((kk * kk).astype(BF16), hs_ref[...])))
    for lo, f, s in zip(parts, first, second):
        rows = slice(lo, lo + n)
        w_log = -_softplus(-s["wl"]) - 0.5
        a = _sigmoid(s["al"])
        inv = lax.rsqrt(jnp.maximum(s["ssq"], 1e-24))
        r_o[rows, :] = f["r"].astype(r_o.dtype)
        k_o[rows, :] = (f["k"] * (1.0 + (a - 1.0) * kaw_ref[...])).astype(k_o.dtype)
        v_o[rows, :] = f["v"].astype(v_o.dtype)
        kk_o[rows, :] = (s["kk"] * _dot_x2(inv, hst_ref[...])).astype(kk_o.dtype)
        a_o[rows, :] = a.astype(a_o.dtype)
        lw_o[rows, :] = -jnp.exp(w_log)
        g_o[rows, :] = s["g"].astype(g_o.dtype)


def _rwkv_pre(h2, nw, mu, wr, wk, wv, w1, w2, w0, a1, a2, a0, g1, g2, kkw, kaw, hs, hst, S, tm):
    T, D = h2.shape
    full = lambda arr: pl.BlockSpec(arr.shape, lambda i: (0,) * arr.ndim)
    row = pl.BlockSpec((tm, D), lambda i: (i, 0))
    params = (nw, mu, wr, wk, wv, w1, w2, w0, a1, a2, a0, g1, g2, kkw, kaw, hs, hst)
    bf = jax.ShapeDtypeStruct((T, D), BF16)
    return pl.pallas_call(
        functools.partial(_rwkv_pre_kernel, tiles_per_seq=S // tm),
        grid=(T // tm,),
        in_specs=[row, pl.BlockSpec((8, D), lambda i: (jnp.maximum(i * (tm // 8) - 1, 0), 0))]
        + [full(p) for p in params],
        out_specs=[row] * 7,
        out_shape=[bf, bf, bf, bf, bf, jax.ShapeDtypeStruct((T, D), F32), bf],
        scratch_shapes=[pltpu.VMEM((tm + 8, D), F32), pltpu.VMEM((tm, D), F32)],
        compiler_params=_cparams(("parallel",)),
    )(h2, h2, *params)


def _rwkv_scan_kernel(r_ref, k_ref, v_ref, kk_ref, a_ref, lw_ref, g_ref, lnw_ref, lnb_ref, rk_ref,
                      o_ref, state_ref, *, n_chunks, n_pairs):
    C = RWKV_CHUNK
    N = RWKV_HEAD
    t = pl.program_id(2)

    @pl.when(t == 0)
    def _():
        state_ref[...] = jnp.zeros_like(state_ref)

    row = lax.broadcasted_iota(jnp.int32, (C, C), 0)
    col = lax.broadcasted_iota(jnp.int32, (C, C), 1)
    tri_incl = jnp.where(row >= col, 1.0, 0.0).astype(BF16)
    eye = jnp.where(row == col, 1.0, 0.0).astype(F32)
    grow = lax.broadcasted_iota(jnp.int32, (2 * C, 2 * C), 0)
    gcol = lax.broadcasted_iota(jnp.int32, (2 * C, 2 * C), 1)
    gr, gc = grow % C, gcol % C
    gmask = (gr + grow // C) > gc
    blockdiag = (grow // N) == (gcol // N)
    lane = lax.broadcasted_iota(jnp.int32, (C, LANES), 1)
    head_masks = (lane < N, lane >= N)
    upper_lanes = lane >= C

    m0 = head_masks[0]
    n_levels = int(math.log2(C))
    pairs = range(n_pairs)
    chains = [(p, s) for p in pairs for s in range(2)]

    def head_sum(x):
        s0 = jnp.sum(jnp.where(m0, x, 0.0), axis=-1, keepdims=True)
        s1 = jnp.sum(jnp.where(m0, 0.0, x), axis=-1, keepdims=True)
        return jnp.where(m0, s0, s1)

    def chunk_body(ci, carry):
        rows = pl.ds(pl.multiple_of(ci * C, C), C)
        cum_all = _dot_2x(tri_incl, lw_ref[0, rows, :])
        pre = []
        for p in pairs:
            lanes = slice(p * LANES, (p + 1) * LANES)
            r = r_ref[0, rows, lanes].astype(F32)
            k = k_ref[0, rows, lanes].astype(F32)
            v = v_ref[0, rows, lanes].astype(F32)
            kk = kk_ref[0, rows, lanes].astype(F32)
            a = a_ref[0, rows, lanes].astype(F32)
            lw = lw_ref[0, rows, lanes]
            cum = cum_all[:, lanes]
            wc = cum[C - 1:C, :]
            e_out = jnp.exp(-cum)
            e_end = jnp.exp(wc - cum)
            b = kk * a
            at = -kk * jnp.exp(cum - lw)
            rt = r * jnp.exp(cum)
            rhs = jnp.concatenate([b * e_out, k * e_out], axis=0).astype(BF16)
            kb_end = jnp.concatenate([k * e_end, b * e_end], axis=0).astype(BF16)
            vb = v.astype(BF16)
            pre.append(dict(lanes=lanes, r=r, k=k, v=v, wc=wc, at=at, rt=rt, rhs=rhs, kb_end=kb_end,
                            vb=vb, vv=jnp.concatenate([vb, vb], axis=0)))

        grams = []
        for p, s in chains:
            d, m = pre[p], head_masks[s]
            lhs = jnp.concatenate([jnp.where(m, d["at"], 0.0), jnp.where(m, d["rt"], 0.0)], axis=0)
            grams.append(jnp.where(gmask, _dot_nt(lhs.astype(BF16), d["rhs"]), 0.0))
        tops = [g[0:C, :] for g in grams]
        g_bot = [g[C:2 * C, :].astype(BF16) for g in grams]
        mvs = [_dot(jnp.where(upper_lanes, top, 0.0).astype(BF16), pre[p]["vv"])
               for top, (p, s) in zip(tops, chains)]
        lms = [top[:, 0:C] for top in tops]
        pks = [eye + lm for lm in lms]
        lks = [_dot(lm.astype(BF16), lm.astype(BF16)) for lm in lms]
        for lvl in range(1, n_levels):
            if lvl < n_levels - 1:
                boths = [_dot(jnp.concatenate([pk, lk], axis=0).astype(BF16), lk.astype(BF16))
                         for pk, lk in zip(pks, lks)]
                pks = [pk + bo[0:C, :] for pk, bo in zip(pks, boths)]
                lks = [bo[C:2 * C, :] for bo in boths]
            else:
                pks = [pk + _dot(pk.astype(BF16), lk.astype(BF16)) for pk, lk in zip(pks, lks)]
        tws = [_dot(pk.astype(BF16), jnp.concatenate([mv, pre[p]["at"]], axis=1).astype(BF16))
               for pk, mv, (p, s) in zip(pks, mvs, chains)]

        sts = [state_ref[p] for p in pairs]
        zs = []
        for p in pairs:
            atp = jnp.where(m0, tws[2 * p][:, LANES:2 * LANES], tws[2 * p + 1][:, LANES:2 * LANES])
            zs.append(_dot_nt(jnp.concatenate([atp, pre[p]["rt"]], axis=0).astype(BF16), sts[p].astype(BF16)))
        us = [jnp.where(m0, tws[2 * p][:, 0:LANES], tws[2 * p + 1][:, 0:LANES]) + zs[p][0:C, :] for p in pairs]
        uvs = [jnp.concatenate([us[p].astype(BF16), pre[p]["vb"]], axis=0) for p in pairs]
        yss = [_dot(g_bot[i], uvs[p]) for i, (p, s) in enumerate(chains)]
        upds = [_dot(jnp.concatenate([pre[p]["v"], us[p]], axis=0).T.astype(BF16), pre[p]["kb_end"])
                for p in pairs]
        for p in pairs:
            d = pre[p]
            lanes = d["lanes"]
            state_ref[p] = sts[p] * jnp.exp(d["wc"]) + jnp.where(blockdiag, upds[p], 0.0)
            y = zs[p][C:2 * C, :] + jnp.where(m0, yss[2 * p], yss[2 * p + 1])
            mean = head_sum(y) * (1.0 / N)
            yc = y - mean
            var = head_sum(yc * yc) * (1.0 / N)
            yn = yc * lax.rsqrt(var + RWKV_LN_EPS) * lnw_ref[:, lanes] + lnb_ref[:, lanes]
            bonus = head_sum(d["r"] * d["k"] * rk_ref[:, lanes]) * d["v"]
            out = (yn + bonus) * g_ref[0, rows, lanes].astype(F32)
            o_ref[0, rows, lanes] = out.astype(o_ref.dtype)
        return carry

    lax.fori_loop(0, n_chunks, chunk_body, 0)


def _rwkv_scan4_kernel(r_ref, k_ref, v_ref, kk_ref, a_ref, lw_ref, g_ref, lnw_ref, lnb_ref, rk_ref,
                       o_ref, state_ref, y_ref, *, n_chunks, n_quads):
    C = RWKV_CHUNK
    N = RWKV_HEAD
    HQ = RWKV_QUAD
    QW = HQ * N
    t = pl.program_id(2)

    @pl.when(t == 0)
    def _():
        state_ref[...] = jnp.zeros_like(state_ref)

    row = lax.broadcasted_iota(jnp.int32, (C, C), 0)
    col = lax.broadcasted_iota(jnp.int32, (C, C), 1)
    tri_incl = jnp.where(row >= col, 1.0, 0.0).astype(BF16)
    prow = lax.broadcasted_iota(jnp.int32, (C, QW), 0)
    plane = lax.broadcasted_iota(jnp.int32, (C, QW), 1)
    eye_p = jnp.where(prow == plane % C, 1.0, 0.0).astype(F32)
    head_masks = [plane // N == h for h in range(HQ)]
    grow = lax.broadcasted_iota(jnp.int32, (2 * C, 2 * QW), 0)
    gcol = lax.broadcasted_iota(jnp.int32, (2 * C, 2 * QW), 1)
    gmask = (grow % C + grow // C) > (gcol % C)
    srow = lax.broadcasted_iota(jnp.int32, (QW, QW), 0)
    scol = lax.broadcasted_iota(jnp.int32, (QW, QW), 1)
    state_blocks = (srow // N) == (scol // N)
    n_levels = int(math.log2(C))
    n_batch = r_ref.shape[0]
    chains = [(bi, q) for bi in range(n_batch) for q in range(n_quads)]

    head_ones = [jnp.where(m, 1.0, 0.0).astype(BF16) for m in head_masks]

    def bdiag(x):
        xb = x.astype(BF16)
        return jnp.concatenate([xb * one for one in head_ones], axis=0)

    def head_sum(x):
        out = jnp.zeros_like(x)
        for m in head_masks:
            out = jnp.where(m, jnp.sum(jnp.where(m, x, 0.0), axis=-1, keepdims=True), out)
        return out

    def chunk_body(ci, carry):
        rows = pl.ds(pl.multiple_of(ci * C, C), C)
        cum_all = [_dot_2x(tri_incl, lw_ref[bi, rows, :]) for bi in range(n_batch)]
        pre = []
        for bi, q in chains:
            lanes = slice(q * QW, (q + 1) * QW)
            r = r_ref[bi, rows, lanes].astype(F32)
            k = k_ref[bi, rows, lanes].astype(F32)
            v = v_ref[bi, rows, lanes].astype(F32)
            kk = kk_ref[bi, rows, lanes].astype(F32)
            a = a_ref[bi, rows, lanes].astype(F32)
            lw = lw_ref[bi, rows, lanes]
            cum = cum_all[bi][:, lanes]
            wc = cum[C - 1:C, :]
            e_out = jnp.exp(-cum)
            e_end = jnp.exp(wc - cum)
            b = kk * a
            at = -kk * jnp.exp(cum - lw)
            rt = r * jnp.exp(cum)
            pre.append(dict(lanes=lanes, r=r, k=k, v=v, wc=wc, at=at, rt=rt,
                            bh=b * e_out, kh=k * e_out,
                            kb_end=jnp.concatenate([k * e_end, b * e_end], axis=0).astype(BF16)))

        grams = []
        for d in pre:
            lhs = jnp.concatenate([d["at"], d["rt"]], axis=0).astype(BF16)
            rhs = jnp.concatenate([bdiag(d["bh"]), bdiag(d["kh"])], axis=0)
            grams.append(jnp.where(gmask, _dot_nt(lhs, rhs), 0.0))
        lps = [g[0:C, 0:QW] for g in grams]
        bd_vs = [bdiag(d["v"]) for d in pre]
        mvs = [_dot(g[0:C, QW:2 * QW].astype(BF16), bd_v) for g, bd_v in zip(grams, bd_vs)]
        pks = [eye_p + lp for lp in lps]
        lks = [_dot(lp.astype(BF16), bdiag(lp)) for lp in lps]
        for lvl in range(1, n_levels):
            if lvl < n_levels - 1:
                boths = [_dot(jnp.concatenate([pk, lk], axis=0).astype(BF16), bdiag(lk))
                         for pk, lk in zip(pks, lks)]
                pks = [pk + bo[0:C, :] for pk, bo in zip(pks, boths)]
                lks = [bo[C:2 * C, :] for bo in boths]
            else:
                pks = [pk + _dot(pk.astype(BF16), bdiag(lk)) for pk, lk in zip(pks, lks)]
        tws = [_dot(pk.astype(BF16), jnp.concatenate([bdiag(mv), bdiag(d["at"])], axis=1))
               for pk, mv, d in zip(pks, mvs, pre)]

        sts = [state_ref[bi, q] for bi, q in chains]
        zs = [_dot_nt(jnp.concatenate([tw[:, QW:2 * QW], d["rt"]], axis=0).astype(BF16), st.astype(BF16))
              for tw, d, st in zip(tws, pre, sts)]
        us = [tw[:, 0:QW] + z[0:C, :] for tw, z in zip(tws, zs)]
        yss = [_dot(g[C:2 * C, :].astype(BF16), jnp.concatenate([bdiag(u), bd_v], axis=0))
               for g, u, bd_v in zip(grams, us, bd_vs)]
        upds = [_dot(jnp.concatenate([d["v"], u], axis=0).T.astype(BF16), d["kb_end"])
                for d, u in zip(pre, us)]
        for i, (bi, q) in enumerate(chains):
            d = pre[i]
            state_ref[bi, q] = sts[i] * jnp.exp(d["wc"]) + jnp.where(state_blocks, upds[i], 0.0)
            y_ref[bi, :, d["lanes"]] = zs[i][C:2 * C, :] + yss[i]
        return carry

    def finish(ci):
        rows = pl.ds(pl.multiple_of(ci * C, C), C)
        for bi, q in chains:
            lanes = slice(q * QW, (q + 1) * QW)
            y = y_ref[bi, :, lanes]
            mean = head_sum(y) * (1.0 / N)
            yc = y - mean
            var = head_sum(yc * yc) * (1.0 / N)
            yn = yc * lax.rsqrt(var + RWKV_LN_EPS) * lnw_ref[:, lanes] + lnb_ref[:, lanes]
            r = r_ref[bi, rows, lanes].astype(F32)
            k = k_ref[bi, rows, lanes].astype(F32)
            v = v_ref[bi, rows, lanes].astype(F32)
            bonus = head_sum(r * k * rk_ref[:, lanes]) * v
            out = (yn + bonus) * g_ref[bi, rows, lanes].astype(F32)
            o_ref[bi, rows, lanes] = out.astype(o_ref.dtype)

    y_ref[...] = jnp.zeros_like(y_ref)

    def loop_body(ci, carry):
        finish(jnp.maximum(ci - 1, 0))
        return chunk_body(ci, carry)

    lax.fori_loop(0, n_chunks, loop_body, 0)
    finish(n_chunks - 1)


def _rwkv_scan(r, k, v, kk, a, lw, g, lnw, lnb, rk, tb, n_quads, nb):
    B, S, D = r.shape
    QW = RWKV_QUAD * RWKV_HEAD
    W = n_quads * QW
    blk = pl.BlockSpec((nb, tb, W), lambda b, j, t: (b, t, j))
    par = pl.BlockSpec((1, W), lambda b, j, t: (0, j))
    return pl.pallas_call(
        functools.partial(_rwkv_scan4_kernel, n_chunks=tb // RWKV_CHUNK, n_quads=n_quads),
        grid=(B // nb, D // W, S // tb),
        in_specs=[blk] * 7 + [par] * 3,
        out_specs=blk,
        out_shape=jax.ShapeDtypeStruct((B, S, D), BF16),
        scratch_shapes=[pltpu.VMEM((nb, n_quads, QW, QW), F32), pltpu.VMEM((nb, RWKV_CHUNK, W), F32)],
        compiler_params=_cparams(("parallel", "parallel", "arbitrary")),
    )(r, k, v, kk, a, lw, g, lnw, lnb, rk)


def _rwkv_out_kernel(y_ref, w_ref, h_ref, nw_ref, wr_ref, br_ref, o_ref, t_ref, cw_ref):
    h = h_ref[...] + _dot(y_ref[...], w_ref[...])
    o_ref[...] = h
    _route_rows(h, nw_ref, wr_ref, br_ref, t_ref, cw_ref)


def _rwkv_out(y, wo, h2, router, tm):
    T, D = h2.shape
    r_in, r_out, r_shapes = _router_specs(tm, D)
    return pl.pallas_call(
        _rwkv_out_kernel,
        grid=(T // tm,),
        in_specs=[pl.BlockSpec((tm, D), lambda i: (i, 0)), pl.BlockSpec((D, D), lambda i: (0, 0)),
                  pl.BlockSpec((tm, D), lambda i: (i, 0))] + r_in,
        out_specs=[pl.BlockSpec((tm, D), lambda i: (i, 0))] + r_out,
        out_shape=[jax.ShapeDtypeStruct((T, D), F32)] + r_shapes(T),
        compiler_params=_cparams(("parallel",)),
    )(y, wo, h2, *router)


def _route_rows(h, nw_ref, wr_ref, br_ref, t_ref, cw_ref):
    tn = _rms(h, nw_ref[...])
    t_ref[...] = tn.astype(t_ref.dtype)
    logits = _dot_x3(tn, wr_ref[...]) + br_ref[...]
    lane = lax.broadcasted_iota(jnp.int32, logits.shape, 1).astype(F32)
    neg = jnp.float32(-jnp.inf)
    big = jnp.float32(1 << 20)
    is_g = lane < MOE_GROUPS
    gl = jnp.where(is_g, logits, neg)
    gmax = jnp.max(gl, axis=-1, keepdims=True)
    g_idx = jnp.min(jnp.where(gl == gmax, lane, big), axis=-1, keepdims=True)
    g_gate = 1.0 / jnp.sum(jnp.where(is_g, jnp.exp(logits - gmax), 0.0), axis=-1, keepdims=True)
    lo = MOE_GROUPS + g_idx * MOE_EPG
    sel = (lane >= lo) & (lane < lo + MOE_EPG)
    el = jnp.where(sel, logits, neg)
    m1 = jnp.max(el, axis=-1, keepdims=True)
    i1 = jnp.min(jnp.where(el == m1, lane, big), axis=-1, keepdims=True)
    el2 = jnp.where(lane == i1, neg, el)
    m2 = jnp.max(el2, axis=-1, keepdims=True)
    i2 = jnp.min(jnp.where(el2 == m2, lane, big), axis=-1, keepdims=True)
    e21 = jnp.exp(m2 - m1)
    w1 = g_gate / (1.0 + e21)
    w2 = w1 * e21
    shift = g_idx * MOE_EPG
    cw_ref[...] = jnp.where(lane == 0.0, g_idx,
                            jnp.where(lane == i1 - shift, w1, jnp.where(lane == i2 - shift, w2, 0.0)))


def _router_specs(tm, D):
    ins = [pl.BlockSpec((1, D), lambda i: (0, 0)), pl.BlockSpec((D, ROUTER_LANES), lambda i: (0, 0)),
           pl.BlockSpec((1, ROUTER_LANES), lambda i: (0, 0))]
    outs = [pl.BlockSpec((tm, D), lambda i: (i, 0)), pl.BlockSpec((tm, ROUTER_LANES), lambda i: (i, 0))]
    shapes = lambda T: [jax.ShapeDtypeStruct((T, D), BF16), jax.ShapeDtypeStruct((T, ROUTER_LANES), F32)]
    return ins, outs, shapes


def _router_params(nw, wg, bg, we, be):
    D = D_MODEL
    pad = ROUTER_LANES - MOE_GROUPS - MOE_EXPERTS
    wr = jnp.concatenate([wg, we, jnp.zeros((D, pad), F32)], axis=1)
    br = jnp.concatenate([bg, be, jnp.zeros((pad,), F32)]).reshape(1, ROUTER_LANES)
    return nw.reshape(1, D), wr, br


def _dot_tn(a, b):
    return lax.dot_general(a, b, (((0,), (0,)), ((), ())), preferred_element_type=F32)


def _experts_kernel(t_ref, cw_ref, h_ref, w13_ref, w2_ref, fnw_ref, o_ref,
                    pt_ref, xs_ref, cws_ref, oacc_ref, meta_ref, *, final_norm):
    tm = t_ref.shape[0]
    cap = pt_ref.shape[1]
    R = MOE_SUB
    F = MOE_D_FF
    g = pl.program_id(1)
    part = pl.program_id(2)
    n_parts = MOE_EPG // MOE_EXPERTS_PER_STEP

    @pl.when((g == 0) & (part == 0))
    def _():
        cw = cw_ref[...]
        lane_t = lax.broadcasted_iota(jnp.int32, (tm, ROUTER_LANES), 1).astype(F32)
        memb = (cw[:, 0:1] == lane_t) & (lane_t < MOE_GROUPS)
        membf = jnp.where(memb, 1.0, 0.0)
        tr = lax.broadcasted_iota(jnp.int32, (tm, tm), 0)
        tc = lax.broadcasted_iota(jnp.int32, (tm, tm), 1)
        before = jnp.where(tr > tc, 1.0, 0.0).astype(BF16)
        ranks = _dot(before, membf.astype(BF16))
        cnt = jnp.sum(membf, axis=0, keepdims=True)
        nsub = jnp.floor((cnt + (R - 1.0)) * (1.0 / R))
        ur = lax.broadcasted_iota(jnp.int32, (ROUTER_LANES, ROUTER_LANES), 0)
        uc = lax.broadcasted_iota(jnp.int32, (ROUTER_LANES, ROUTER_LANES), 1)
        prefix = jnp.where(ur < uc, 1.0, 0.0).astype(BF16)
        base = _dot_x2(jnp.broadcast_to(nsub * R, (8, ROUTER_LANES)), prefix)[0:1, :]
        pos = jnp.sum(jnp.where(memb, ranks + base, 0.0), axis=-1, keepdims=True)
        col = lax.broadcasted_iota(jnp.int32, (tm, cap), 1).astype(F32)
        pt = jnp.where(pos == col, 1.0, 0.0).astype(BF16)
        pt_ref[...] = pt
        cw_hi, cw_lo = _split2(cw)
        D = t_ref.shape[1]
        gathered = _dot_tn(pt, jnp.concatenate([t_ref[...], cw_hi, cw_lo], axis=1))
        xs_ref[...] = gathered[:, 0:D].astype(BF16)
        cws_ref[...] = gathered[:, D:D + ROUTER_LANES] + gathered[:, D + ROUTER_LANES:D + 2 * ROUTER_LANES]
        oacc_ref[...] = jnp.zeros_like(oacc_ref)
        meta_ref[0:1, :] = nsub
        meta_ref[1:2, :] = base

    lane1 = lax.broadcasted_iota(jnp.int32, (1, ROUTER_LANES), 1)
    n_sub = jnp.sum(jnp.where(lane1 == g, meta_ref[0:1, :], 0.0)).astype(jnp.int32)
    row0 = jnp.sum(jnp.where(lane1 == g, meta_ref[1:2, :], 0.0)).astype(jnp.int32)
    def run_rows(start, n_rows):
        rows = pl.ds(pl.multiple_of(start, R), n_rows)
        lane_r = lax.broadcasted_iota(jnp.int32, (n_rows, ROUTER_LANES), 1)
        xs = xs_ref[rows, :]
        cws = cws_ref[rows, :]
        acc = oacc_ref[rows, :]
        for j in range(MOE_EXPERTS_PER_STEP):
            h13 = _dot(xs, w13_ref[j])
            ce = jnp.sum(jnp.where(lane_r == MOE_GROUPS + part * MOE_EXPERTS_PER_STEP + j, cws, 0.0),
                         axis=-1, keepdims=True)
            he = (_silu(h13[:, 0:F]) * h13[:, F:2 * F] * ce).astype(BF16)
            acc = acc + _dot(he, w2_ref[j])
        oacc_ref[rows, :] = acc

    big = MOE_SUBS_PER_PASS
    n_big = jnp.maximum(n_sub - MOE_MAX_PASS + big - 1, 0) // big

    def big_body(s, carry):
        run_rows(row0 + s * (big * R), big * R)
        return carry

    lax.fori_loop(0, n_big, big_body, 0)
    rem = n_sub - n_big * big
    done = row0 + n_big * (big * R)
    for k in range(1, MOE_MAX_PASS + 1):
        @pl.when(rem == k)
        def _(k=k):
            run_rows(done, k * R)

    @pl.when((g == MOE_GROUPS - 1) & (part == n_parts - 1))
    def _():
        out = h_ref[...] + _dot(pt_ref[...], oacc_ref[...].astype(BF16))
        if final_norm:
            out = _rms(out, fnw_ref[...])
        o_ref[...] = out


def _experts(t, cw, h2, w13, w2, fnw, tm, final_norm):
    T, D = h2.shape
    F = MOE_D_FF
    E = MOE_EXPERTS_PER_STEP
    n_parts = MOE_EPG // E
    cap = tm + MOE_GROUPS * MOE_SUB
    return pl.pallas_call(
        functools.partial(_experts_kernel, final_norm=final_norm),
        grid=(T // tm, MOE_GROUPS, n_parts),
        in_specs=[
            pl.BlockSpec((tm, D), lambda i, g, p: (i, 0)),
            pl.BlockSpec((tm, ROUTER_LANES), lambda i, g, p: (i, 0)),
            pl.BlockSpec((tm, D), lambda i, g, p: (i, 0)),
            pl.BlockSpec((E, D, 2 * F), lambda i, g, p: (g * n_parts + p, 0, 0)),
            pl.BlockSpec((E, F, D), lambda i, g, p: (g * n_parts + p, 0, 0)),
            pl.BlockSpec((1, D), lambda i, g, p: (0, 0)),
        ],
        out_specs=pl.BlockSpec((tm, D), lambda i, g, p: (i, 0)),
        out_shape=jax.ShapeDtypeStruct((T, D), F32),
        scratch_shapes=[
            pltpu.VMEM((tm, cap), BF16),
            pltpu.VMEM((cap, D), BF16),
            pltpu.VMEM((cap, ROUTER_LANES), F32),
            pltpu.VMEM((cap, D), F32),
            pltpu.VMEM((8, ROUTER_LANES), F32),
        ],
        compiler_params=_cparams(("parallel", "arbitrary", "arbitrary")),
    )(t, cw, h2, w13, w2, fnw)


def _moe_experts(t, cw, h2, w1, w3, w2, fnw, final_norm, tm):
    w13 = jnp.concatenate([w1, w3], axis=-1).astype(BF16)
    return _experts(t, cw, h2, w13, w2.astype(BF16), fnw.reshape(1, D_MODEL), tm, final_norm)


def _pick(n, prefs):
    for p in prefs:
        if n % p == 0:
            return p
    return n


def _ssd_retention_layer(h2, B, S, nw, w_in, conv_w, conv_b, dt_bias, a_log, d_skip, norm_w, w_out, router):
    D = D_MODEL
    T = B * S
    o_x, o_bc, o_dt, o_q, o_k, o_v, o_g = 1024, 2048, 2560, 2576, 3088, 3600, 4624
    w_main = jnp.concatenate([w_in[:, 0:o_x], w_in[:, o_x:o_bc], w_in[:, o_v:o_g], w_in[:, o_g:o_g + D],
                              w_in[:, o_bc:o_dt], w_in[:, o_q:o_k], w_in[:, o_k:o_v]], axis=1).astype(BF16)
    w_dt = jnp.pad(w_in[:, o_dt:o_q], ((0, 0), (0, LANES - SSD_HEADS))).astype(BF16)
    tm = _pick(T, (1024, 512, 256, 128))
    proj, dt = _inproj(h2, nw.reshape(1, D), w_main, w_dt, tm, 1408)
    proj3 = proj.reshape(B, S, proj.shape[1])
    dt3 = dt.reshape(B, S, LANES)

    head_of_channel = jnp.arange(D) // SSD_HEAD_DIM
    expand = (jnp.arange(LANES)[:, None] == head_of_channel[None, :]).astype(BF16)
    pad16 = lambda x: jnp.pad(x, (0, LANES - SSD_HEADS)).reshape(1, LANES)
    ya = _ssd(proj3, dt3, conv_w[:, :D], conv_b[:D].reshape(1, D), conv_w[:, D:], conv_b[D:].reshape(1, 512),
              pad16(dt_bias), pad16(a_log), expand, jnp.repeat(d_skip, SSD_HEAD_DIM).reshape(1, D),
              norm_w.reshape(1, D))

    C = RET_CHUNK
    half = RET_QK_HEAD // 2
    pos = jnp.arange(S, dtype=F32)
    inv_freq = ROPE_BASE ** (-jnp.arange(half, dtype=F32) / half)
    ang = pos[:, None] * inv_freq[None, :]
    cc = jnp.concatenate([jnp.cos(ang), jnp.cos(ang)], axis=1)
    ss = jnp.concatenate([-jnp.sin(ang), jnp.sin(ang)], axis=1)
    log_gamma = jnp.log(1.0 - 2.0 ** (-5.0 - jnp.arange(RET_HEADS, dtype=F32)))
    idx = jnp.arange(C, dtype=F32)
    diff = idx[:, None] - idx[None, :]
    dmask = jnp.where(diff[None] >= 0, jnp.exp(jnp.maximum(diff, 0.0)[None] * log_gamma[:, None, None]), 0.0)
    qdec = jnp.repeat(jnp.exp((idx[:, None] + 1.0) * log_gamma[None, :]), RET_V_HEAD, axis=1)
    kdec = jnp.repeat(jnp.exp((C - 1.0 - idx[:, None]) * log_gamma[None, :]), RET_QK_HEAD, axis=1)
    cdec = jnp.repeat(jnp.exp(C * log_gamma), RET_V_HEAD).reshape(1, D)
    yb = _retention(proj3, cc, ss, dmask, qdec, kdec, cdec)

    w_out_b = w_out.astype(BF16)
    return _outproj(ya.reshape(T, D), yb.reshape(T, D), w_out_b[:D], w_out_b[D:], h2, router, tm)


def _rwkv_layer(h2, B, S, nw, mu, w_r, w_k, w_v, w_o, w0, w1, w2, a0, a1, a2, g1, g2, k_k, k_a, r_k, lnx_w, lnx_b,
                router):
    D = D_MODEL
    T = B * S
    padc = lambda w, n: jnp.pad(w, ((0, 0), (0, n - w.shape[1]))).astype(BF16)
    padr = lambda w, n: jnp.pad(w, ((0, n - w.shape[0]), (0, 0))).astype(BF16)
    head_of_channel = jnp.arange(D) // RWKV_HEAD
    hs = (head_of_channel[:, None] == jnp.arange(LANES)[None, :]).astype(BF16)
    row = lambda x: x.reshape(1, D)
    tm = _pick(S, (512, 256, 128))
    r, k, v, kk, a, lw, g = _rwkv_pre(
        h2, row(nw), mu, w_r.astype(BF16), w_k.astype(BF16), w_v.astype(BF16),
        padc(w1, RWKV_LORA_PAD), padr(w2, RWKV_LORA_PAD), row(w0),
        padc(a1, RWKV_LORA_PAD), padr(a2, RWKV_LORA_PAD), row(a0),
        padc(g1, RWKV_GATE_PAD), padr(g2, RWKV_GATE_PAD), row(k_k), row(k_a), hs, hs.T, S, tm)
    sh = lambda x: x.reshape(B, S, D)
    tb = _pick(S, (512, 256, 128, 64))
    y = _rwkv_scan(sh(r), sh(k), sh(v), sh(kk), sh(a), sh(lw), sh(g),
                   row(lnx_w), row(lnx_b), r_k.reshape(1, D), tb, RWKV_HEADS // RWKV_QUAD,
                   _pick(B, (RWKV_SEQS_PER_STEP, 1)))
    return _rwkv_out(y.reshape(T, D), w_o.astype(BF16), h2, router, _pick(T, (1024, 512, 256, 128)))


def kernel(x, norm_mix_w, norm_ffn_w, norm_final_w, w_in_e, ssd_conv_w, ssd_conv_b, ssd_dt_bias, ssd_a_log, ssd_d, ssd_norm_w, w_out_e, rw_mu, rw_wr, rw_wk, rw_wv, rw_wo, rw_w0, rw_w1, rw_w2, rw_a0, rw_a1, rw_a2, rw_g1, rw_g2, rw_kk, rw_ka, rw_rk, rw_lnx_w, rw_lnx_b, moe_wg, moe_bg, moe_we, moe_be, moe_w1, moe_w3, moe_w2):
    B, S, D = x.shape
    T = B * S
    depth = norm_mix_w.shape[0]
    h = x.reshape(T, D)
    tm_moe = _pick(T, (1024, 512, 256, 128))
    for layer in range(depth):
        i = layer // 2
        router = _router_params(norm_ffn_w[layer], moe_wg[layer], moe_bg[layer], moe_we[layer], moe_be[layer])
        if layer % 2 == 0:
            h, t, cw = _ssd_retention_layer(h, B, S, norm_mix_w[layer], w_in_e[i], ssd_conv_w[i], ssd_conv_b[i],
                                            ssd_dt_bias[i], ssd_a_log[i], ssd_d[i], ssd_norm_w[i], w_out_e[i],
                                            router)
        else:
            h, t, cw = _rwkv_layer(h, B, S, norm_mix_w[layer], rw_mu[i], rw_wr[i], rw_wk[i], rw_wv[i], rw_wo[i],
                                   rw_w0[i], rw_w1[i], rw_w2[i], rw_a0[i], rw_a1[i], rw_a2[i], rw_g1[i],
                                   rw_g2[i], rw_kk[i], rw_ka[i], rw_rk[i], rw_lnx_w[i], rw_lnx_b[i], router)
        h = _moe_experts(t, cw, h, moe_w1[layer], moe_w3[layer], moe_w2[layer], norm_final_w,
                         final_norm=(layer == depth - 1), tm=tm_moe)
    return h.reshape(B, S, D)
```

```python
import functools
import math

import jax
import jax.numpy as jnp
from jax import lax
from jax.experimental import pallas as pl
from jax.experimental.pallas import tpu as pltpu

F32 = jnp.float32
BF16 = jnp.bfloat16

D_MODEL = 1024
RMS_EPS = 1e-6
SSD_HEADS = 16
SSD_HEAD_DIM = 64
SSD_GROUPS = 2
SSD_STATE = 128
SSD_CONV = 4
SSD_CHUNK = 128
SSD_CONV_TAIL = 16
SSD_NORM_EPS = 1e-5
SSD_GROUP_WIDTH = D_MODEL // SSD_GROUPS
RET_HEADS = 4
RET_QK_HEAD = 128
RET_V_HEAD = 256
RET_CHUNK = 128
ROPE_BASE = 10000.0
RWKV_HEAD = 64
RWKV_HEADS = 16
RWKV_LN_EPS = 64e-5
RWKV_CHUNK = 64
RWKV_QUAD = 4
RWKV_SEQS_PER_STEP = 2
RWKV_LORA_PAD = 128
RWKV_GATE_PAD = 256
RWKV_PRE_SPLIT = 2
MOE_GROUPS = 4
MOE_EPG = 4
MOE_EXPERTS = 16
MOE_D_FF = 512
ROUTER_LANES = 128
MOE_SUB = 64
MOE_SUBS_PER_PASS = 4
MOE_MAX_PASS = 6
MOE_EXPERTS_PER_STEP = 2

LANES = 128
VMEM_LIMIT_BYTES = 56 * 1024 * 1024


def _cparams(sem):
    return pltpu.CompilerParams(dimension_semantics=sem, vmem_limit_bytes=VMEM_LIMIT_BYTES)


def _dot(a, b):
    return jnp.dot(a, b, preferred_element_type=F32)


def _dot_nt(a, b):
    return lax.dot_general(a, b, (((1,), (1,)), ((), ())), preferred_element_type=F32)


def _split2(x):
    hi = x.astype(BF16)
    lo = (x - hi.astype(F32)).astype(BF16)
    return hi, lo


def _dot_x2(x, w_exact):
    hi, lo = _split2(x)
    return _dot(hi, w_exact) + _dot(lo, w_exact)


def _dot_2x(w_exact, x):
    hi, lo = _split2(x)
    return _dot(w_exact, hi) + _dot(w_exact, lo)


def _dot_x3(x, w):
    xh, xl = _split2(x)
    wh, wl = _split2(w)
    return _dot(xh, wh) + _dot(xl, wh) + _dot(xh, wl)


def _sigmoid(x):
    return 0.5 * jnp.tanh(0.5 * x) + 0.5


def _silu(x):
    hx = 0.5 * x
    return hx * jnp.tanh(hx) + hx


def _softplus(x):
    return jnp.maximum(x, 0.0) + jnp.log(1.0 + jnp.exp(-jnp.abs(x)))


def _rms(x, w, eps=RMS_EPS):
    return x * lax.rsqrt(jnp.mean(x * x, axis=-1, keepdims=True) + eps) * w


def _inproj_kernel(x_ref, nw_ref, w_ref, wdt_ref, o_ref, dt_ref, u_ref):
    @pl.when(pl.program_id(1) == 0)
    def _():
        ub = _rms(x_ref[...], nw_ref[...]).astype(BF16)
        u_ref[...] = ub
        dt_ref[...] = _dot(ub, wdt_ref[...])

    o_ref[...] = _dot(u_ref[...], w_ref[...]).astype(o_ref.dtype)


def _inproj(h2, nw, w_main, w_dt, tm, tn):
    T, D = h2.shape
    N = w_main.shape[1]
    return pl.pallas_call(
        _inproj_kernel,
        grid=(T // tm, N // tn),
        in_specs=[
            pl.BlockSpec((tm, D), lambda i, j: (i, 0)),
            pl.BlockSpec((1, D), lambda i, j: (0, 0)),
            pl.BlockSpec((D, tn), lambda i, j: (0, j)),
            pl.BlockSpec((D, LANES), lambda i, j: (0, 0)),
        ],
        out_specs=[
            pl.BlockSpec((tm, tn), lambda i, j: (i, j)),
            pl.BlockSpec((tm, LANES), lambda i, j: (i, 0)),
        ],
        out_shape=[
            jax.ShapeDtypeStruct((T, N), BF16),
            jax.ShapeDtypeStruct((T, LANES), F32),
        ],
        scratch_shapes=[pltpu.VMEM((tm, D), BF16)],
        compiler_params=_cparams(("parallel", "arbitrary")),
    )(h2, nw, w_main, w_dt)


def _ssd_chunk(z_ref, x_ref, bc_ref, dt_ref, cwx_ref, cbx_ref, cwb_ref, cbb_ref,
               dtb_ref, alog_ref, e_ref, dskip_ref, nw_ref, o_ref,
               state_ref, xtail, bctail, ybuf, *, lane0):
    L = SSD_CHUNK
    TAIL = xtail.shape[0]
    srow = lax.broadcasted_iota(jnp.int32, (SSD_CONV * L, TAIL + L), 0)
    scol = lax.broadcasted_iota(jnp.int32, (SSD_CONV * L, TAIL + L), 1)
    shift = jnp.where(scol == (srow % L) + (srow // L) + (TAIL - SSD_CONV + 1), 1.0, 0.0).astype(BF16)

    def conv(cur_ref, tail, w_ref, b_ref):
        cur = cur_ref[0]
        taps = _dot(shift, jnp.concatenate([tail[...], cur], axis=0))
        acc = b_ref[...] + w_ref[0:1, :] * taps[0:L, :]
        for k in range(1, SSD_CONV):
            acc = acc + w_ref[k:k + 1, :] * taps[k * L:(k + 1) * L, :]
        tail[...] = cur[L - TAIL:L, :]
        return _silu(acc)

    xs = conv(x_ref, xtail, cwx_ref, cbx_ref)
    bc = conv(bc_ref, bctail, cwb_ref, cbb_ref)

    dt = _softplus(dt_ref[0] + dtb_ref[...])
    a = dt * (-jnp.exp(alog_ref[...]))
    row = lax.broadcasted_iota(jnp.int32, (L, L), 0)
    col = lax.broadcasted_iota(jnp.int32, (L, L), 1)
    causal = row >= col
    tri = jnp.where(causal, 1.0, 0.0).astype(BF16)
    acum = _dot_2x(tri, a)
    acum_t = acum.T
    e = e_ref[...]
    dt_full = _dot_x2(dt, e)
    acum_full = _dot_x2(acum, e)
    xdt = xs * dt_full
    alast_full = acum_full[L - 1:L, :]
    decay_in = jnp.exp(acum_full)
    xdt_end = (xdt * jnp.exp(alast_full - acum_full)).astype(BF16)
    xdt_b = xdt.astype(BF16)
    lane = lax.broadcasted_iota(jnp.int32, (L, LANES), 1)
    first_head = lane < SSD_HEAD_DIM

    GW = SSD_GROUP_WIDTH
    for g in range(SSD_GROUPS):
        bg = bc[:, g * SSD_STATE:(g + 1) * SSD_STATE]
        cg = bc[:, (SSD_GROUPS + g) * SSD_STATE:(SSD_GROUPS + g + 1) * SSD_STATE].astype(BF16)
        cb = _dot_nt(cg, bg.astype(BF16))
        st = state_ref[g]
        y_off = _dot(cg, st.astype(BF16)) * decay_in[:, g * GW:(g + 1) * GW]
        for p in range(GW // LANES):
            xp = xdt_b[:, g * GW + p * LANES:g * GW + (p + 1) * LANES]
            ys = []
            for s in range(2):
                hd = g * (SSD_HEADS // SSD_GROUPS) + 2 * p + s
                seg = acum[:, hd:hd + 1] - acum_t[hd:hd + 1, :]
                dec = jnp.where(causal, jnp.exp(seg), 0.0)
                ys.append(_dot((cb * dec).astype(BF16), xp))
            yd = jnp.where(first_head, ys[0], ys[1])
            lo = g * GW + p * LANES
            ybuf[:, lo:lo + LANES] = yd + y_off[:, p * LANES:(p + 1) * LANES]
        bg_t = bg.T.astype(BF16)
        state_ref[g] = st * jnp.exp(alast_full[:, g * GW:(g + 1) * GW]) + _dot(
            bg_t, xdt_end[:, g * GW:(g + 1) * GW])

    y = ybuf[...] + xs * dskip_ref[...]
    y = y * _silu(z_ref[0].astype(F32))
    for g in range(SSD_GROUPS):
        yg = y[:, g * GW:(g + 1) * GW]
        yg = yg * lax.rsqrt(jnp.mean(yg * yg, axis=-1, keepdims=True) + SSD_NORM_EPS)
        lo = lane0 + g * GW
        o_ref[0, :, lo:lo + GW] = (yg * nw_ref[:, g * GW:(g + 1) * GW]).astype(o_ref.dtype)


def _ret_chunk(q_ref, k_ref, v_ref, g_ref, cc_ref, ss_ref, dmask_ref, qdec_ref, kdec_ref,
               cdec_ref, o_ref, r_ref, *, lane0):
    cc = cc_ref[...]
    ss = ss_ref[...]
    dk, dv = RET_QK_HEAD, RET_V_HEAD
    half = dk // 2

    def rope(x):
        return x * cc + pltpu.roll(x, half, 1) * ss

    heads = range(RET_HEADS)
    qs = [rope(q_ref[0, :, hd * dk:(hd + 1) * dk].astype(F32)) for hd in heads]
    ks = [rope(k_ref[0, :, hd * dk:(hd + 1) * dk].astype(F32)) * (dk ** -0.5) for hd in heads]
    vs = [v_ref[0, :, hd * dv:(hd + 1) * dv] for hd in heads]
    qbs = [q.astype(BF16) for q in qs]
    ss = [_dot_nt(qbs[hd], ks[hd].astype(BF16)) * dmask_ref[hd] for hd in heads]
    r_olds = [r_ref[hd] for hd in heads]
    cross = [_dot(qbs[hd], r_olds[hd].astype(BF16)) for hd in heads]
    kd_ts = [(ks[hd] * kdec_ref[:, hd * dk:(hd + 1) * dk]).T.astype(BF16) for hd in heads]
    upds = [_dot(kd_ts[hd], vs[hd]) for hd in heads]
    inner = [_dot(ss[hd].astype(BF16), vs[hd]) for hd in heads]
    for hd in heads:
        r_ref[hd] = r_olds[hd] * cdec_ref[:, hd * dv:(hd + 1) * dv] + upds[hd]
        y = inner[hd] + cross[hd] * qdec_ref[:, hd * dv:(hd + 1) * dv]
        y = y * lax.rsqrt(jnp.mean(y * y, axis=-1, keepdims=True) + RMS_EPS)
        y = y * _silu(g_ref[0, :, hd * dv:(hd + 1) * dv].astype(F32))
        lo = lane0 + hd * dv
        o_ref[0, :, lo:lo + dv] = y.astype(o_ref.dtype)


N_SSD_IN = 13
N_RET_IN = 10


def _mix0_kernel(*refs):
    ssd_in = refs[:N_SSD_IN]
    ret_in = refs[N_SSD_IN:N_SSD_IN + N_RET_IN]
    o_ref, state_ref, xtail, bctail, ybuf, r_ref = refs[N_SSD_IN + N_RET_IN:]

    @pl.when(pl.program_id(1) == 0)
    def _():
        state_ref[...] = jnp.zeros_like(state_ref)
        xtail[...] = jnp.zeros_like(xtail)
        bctail[...] = jnp.zeros_like(bctail)
        r_ref[...] = jnp.zeros_like(r_ref)

    _ssd_chunk(*ssd_in, o_ref, state_ref, xtail, bctail, ybuf, lane0=0)
    _ret_chunk(*ret_in, o_ref, r_ref, lane0=D_MODEL)


def _mix0(proj3, dt3, ssd_params, ret_params):
    B, S, _ = proj3.shape
    L = SSD_CHUNK
    D = D_MODEL
    assert RET_CHUNK == L and len(ssd_params) + 4 == N_SSD_IN and len(ret_params) + 4 == N_RET_IN
    full = lambda arr: pl.BlockSpec(arr.shape, lambda b, c: (0,) * arr.ndim)
    cc, ss = ret_params[:2]
    ret_specs = [pl.BlockSpec((L, RET_QK_HEAD), lambda b, c: (c, 0))] * 2 + [full(p) for p in ret_params[2:]]
    return pl.pallas_call(
        _mix0_kernel,
        grid=(B, S // L),
        in_specs=[
            pl.BlockSpec((1, L, D), lambda b, c: (b, c, 0)),
            pl.BlockSpec((1, L, D), lambda b, c: (b, c, 1)),
            pl.BlockSpec((1, L, 512), lambda b, c: (b, c, 8)),
            pl.BlockSpec((1, L, LANES), lambda b, c: (b, c, 0)),
        ] + [full(p) for p in ssd_params] + [
            pl.BlockSpec((1, L, 512), lambda b, c: (b, c, 9)),
            pl.BlockSpec((1, L, 512), lambda b, c: (b, c, 10)),
            pl.BlockSpec((1, L, D), lambda b, c: (b, c, 2)),
            pl.BlockSpec((1, L, D), lambda b, c: (b, c, 3)),
        ] + ret_specs,
        out_specs=pl.BlockSpec((1, L, 2 * D), lambda b, c: (b, c, 0)),
        out_shape=jax.ShapeDtypeStruct((B, S, 2 * D), BF16),
        scratch_shapes=[
            pltpu.VMEM((SSD_GROUPS, SSD_STATE, SSD_GROUP_WIDTH), F32),
            pltpu.VMEM((SSD_CONV_TAIL, D), BF16),
            pltpu.VMEM((SSD_CONV_TAIL, 512), BF16),
            pltpu.VMEM((L, D), F32),
            pltpu.VMEM((RET_HEADS, RET_QK_HEAD, RET_V_HEAD), F32),
        ],
        compiler_params=_cparams(("parallel", "arbitrary")),
    )(proj3, proj3, proj3, dt3, *ssd_params, proj3, proj3, proj3, proj3, *ret_params)


def _outproj_kernel(y_ref, w_ref, h_ref, nw_ref, wr_ref, br_ref, o_ref, t_ref, cw_ref):
    h = h_ref[...] + _dot(y_ref[...], w_ref[...])
    o_ref[...] = h
    _route_rows(h, nw_ref, wr_ref, br_ref, t_ref, cw_ref)


def _outproj(y, w, h2, router, tm):
    T, D = h2.shape
    K = y.shape[1]
    r_in, r_out, r_shapes = _router_specs(tm, D)
    return pl.pallas_call(
        _outproj_kernel,
        grid=(T // tm,),
        in_specs=[
            pl.BlockSpec((tm, K), lambda i: (i, 0)),
            pl.BlockSpec((K, D), lambda i: (0, 0)),
            pl.BlockSpec((tm, D), lambda i: (i, 0)),
        ] + r_in,
        out_specs=[pl.BlockSpec((tm, D), lambda i: (i, 0))] + r_out,
        out_shape=[jax.ShapeDtypeStruct((T, D), F32)] + r_shapes(T),
        compiler_params=_cparams(("parallel",)),
    )(y, w, h2, *router)


def _rwkv_pre_kernel(h_ref, hp_ref, nw_ref, mu_ref, wr_ref, wk_ref, wv_ref,
                     w1_ref, w2_ref, w0_ref, a1_ref, a2_ref, a0_ref, g1_ref, g2_ref,
                     kkw_ref, kaw_ref, hs_ref, hst_ref,
                     r_o, k_o, v_o, kk_o, a_o, lw_o, g_o, ubuf, xxbuf, *, tiles_per_seq):
    tm = h_ref.shape[0]
    i = pl.program_id(0)
    nw = nw_ref[...]
    u = _rms(h_ref[...], nw)
    up = _rms(hp_ref[...], nw)
    seq_start = (i % tiles_per_seq) == 0
    ubuf[8:8 + tm, :] = u
    ubuf[0:8, :] = jnp.where(seq_start, 0.0, up)
    xxbuf[...] = ubuf[7:7 + tm, :] - u

    def mix(j, lo, n):
        return (ubuf[8 + lo:8 + lo + n, :] + xxbuf[lo:lo + n, :] * mu_ref[j:j + 1, :]).astype(BF16)

    n = tm // RWKV_PRE_SPLIT
    parts = [s * n for s in range(RWKV_PRE_SPLIT)]
    first = []
    for lo in parts:
        first.append(dict(
            r=_dot(mix(0, lo, n), wr_ref[...]), k=_dot(mix(2, lo, n), wk_ref[...]),
            v=_dot(mix(3, lo, n), wv_ref[...]), w1=_dot(mix(1, lo, n), w1_ref[...]),
            a1=_dot(mix(4, lo, n), a1_ref[...]), g1=_dot(mix(5, lo, n), g1_ref[...])))
    second = []
    for f in first:
        kk = f["k"] * kkw_ref[...]
        second.append(dict(
            wl=w0_ref[...] + _dot(jnp.tanh(f["w1"]).astype(BF16), w2_ref[...]),
            al=a0_ref[...] + _dot(f["a1"].astype(BF16), a2_ref[...]),
            g=_dot(_sigmoid(f["g1"]).astype(BF16), g2_ref[...]),
            kk=kk, ssq=_dot((kk * kk).astype(BF16), hs_ref[...])))
    for lo, f, s in zip(parts, first, second):
        rows = slice(lo, lo + n)
        w_log = -_softplus(-s["wl"]) - 0.5
        a = _sigmoid(s["al"])
        inv = lax.rsqrt(jnp.maximum(s["ssq"], 1e-24))
        r_o[rows, :] = f["r"].astype(r_o.dtype)
        k_o[rows, :] = (f["k"] * (1.0 + (a - 1.0) * kaw_ref[...])).astype(k_o.dtype)
        v_o[rows, :] = f["v"].astype(v_o.dtype)
        kk_o[rows, :] = (s["kk"] * _dot_x2(inv, hst_ref[...])).astype(kk_o.dtype)
        a_o[rows, :] = a.astype(a_o.dtype)
        lw_o[rows, :] = -jnp.exp(w_log)
        g_o[rows, :] = s["g"].astype(g_o.dtype)


def _rwkv_pre(h2, nw, mu, wr, wk, wv, w1, w2, w0, a1, a2, a0, g1, g2, kkw, kaw, hs, hst, S, tm):
    T, D = h2.shape
    full = lambda arr: pl.BlockSpec(arr.shape, lambda i: (0,) * arr.ndim)
    row = pl.BlockSpec((tm, D), lambda i: (i, 0))
    params = (nw, mu, wr, wk, wv, w1, w2, w0, a1, a2, a0, g1, g2, kkw, kaw, hs, hst)
    bf = jax.ShapeDtypeStruct((T, D), BF16)
    return pl.pallas_call(
        functools.partial(_rwkv_pre_kernel, tiles_per_seq=S // tm),
        grid=(T // tm,),
        in_specs=[row, pl.BlockSpec((8, D), lambda i: (jnp.maximum(i * (tm // 8) - 1, 0), 0))]
        + [full(p) for p in params],
        out_specs=[row] * 7,
        out_shape=[bf, bf, bf, bf, bf, jax.ShapeDtypeStruct((T, D), F32), bf],
        scratch_shapes=[pltpu.VMEM((tm + 8, D), F32), pltpu.VMEM((tm, D), F32)],
        compiler_params=_cparams(("parallel",)),
    )(h2, h2, *params)


def _rwkv_scan4_kernel(r_ref, k_ref, v_ref, kk_ref, a_ref, lw_ref, g_ref, lnw_ref, lnb_ref, rk_ref,
                       o_ref, state_ref, y_ref, *, n_chunks, n_quads):
    C = RWKV_CHUNK
    N = RWKV_HEAD
    HQ = RWKV_QUAD
    QW = HQ * N
    t = pl.program_id(2)

    @pl.when(t == 0)
    def _():
        state_ref[...] = jnp.zeros_like(state_ref)

    row = lax.broadcasted_iota(jnp.int32, (C, C), 0)
    col = lax.broadcasted_iota(jnp.int32, (C, C), 1)
    tri_incl = jnp.where(row >= col, 1.0, 0.0).astype(BF16)
    prow = lax.broadcasted_iota(jnp.int32, (C, QW), 0)
    plane = lax.broadcasted_iota(jnp.int32, (C, QW), 1)
    eye_p = jnp.where(prow == plane % C, 1.0, 0.0).astype(F32)
    head_masks = [plane // N == h for h in range(HQ)]
    grow = lax.broadcasted_iota(jnp.int32, (2 * C, 2 * QW), 0)
    gcol = lax.broadcasted_iota(jnp.int32, (2 * C, 2 * QW), 1)
    gmask = (grow % C + grow // C) > (gcol % C)
    srow = lax.broadcasted_iota(jnp.int32, (QW, QW), 0)
    scol = lax.broadcasted_iota(jnp.int32, (QW, QW), 1)
    state_blocks = (srow // N) == (scol // N)
    n_levels = int(math.log2(C))
    n_batch = r_ref.shape[0]
    chains = [(bi, q) for bi in range(n_batch) for q in range(n_quads)]

    head_ones = [jnp.where(m, 1.0, 0.0).astype(BF16) for m in head_masks]

    def bdiag(x):
        xb = x.astype(BF16)
        return jnp.concatenate([xb * one for one in head_ones], axis=0)

    first_of_pair = lax.broadcasted_iota(jnp.int32, (C, LANES), 1) < N

    def head_sum(x):
        halves = []
        for lo in range(0, QW, LANES):
            xh = x[:, lo:lo + LANES]
            total = jnp.sum(xh, axis=-1, keepdims=True)
            first = jnp.sum(jnp.where(first_of_pair, xh, 0.0), axis=-1, keepdims=True)
            halves.append(jnp.where(first_of_pair, first, total - first))
        return jnp.concatenate(halves, axis=1)

    def chunk_body(ci, carry):
        rows = pl.ds(pl.multiple_of(ci * C, C), C)
        cum_all = [_dot_2x(tri_incl, lw_ref[bi, rows, :]) for bi in range(n_batch)]
        pre = []
        for bi, q in chains:
            lanes = slice(q * QW, (q + 1) * QW)
            r = r_ref[bi, rows, lanes].astype(F32)
            k = k_ref[bi, rows, lanes].astype(F32)
            v = v_ref[bi, rows, lanes].astype(F32)
            kk = kk_ref[bi, rows, lanes].astype(F32)
            a = a_ref[bi, rows, lanes].astype(F32)
            lw = lw_ref[bi, rows, lanes]
            cum = cum_all[bi][:, lanes]
            wc = cum[C - 1:C, :]
            e_out = jnp.exp(-cum)
            e_end = jnp.exp(wc - cum)
            b = kk * a
            at = -kk * jnp.exp(cum - lw)
            rt = r * jnp.exp(cum)
            pre.append(dict(lanes=lanes, r=r, k=k, v=v, wc=wc, at=at, rt=rt,
                            bh=b * e_out, kh=k * e_out,
                            kb_end=jnp.concatenate([k * e_end, b * e_end], axis=0).astype(BF16)))

        grams = []
        for d in pre:
            lhs = jnp.concatenate([d["at"], d["rt"]], axis=0).astype(BF16)
            rhs = jnp.concatenate([bdiag(d["bh"]), bdiag(d["kh"])], axis=0)
            grams.append(jnp.where(gmask, _dot_nt(lhs, rhs), 0.0))
        lps = [g[0:C, 0:QW] for g in grams]
        bd_vs = [bdiag(d["v"]) for d in pre]
        mvs = [_dot(g[0:C, QW:2 * QW].astype(BF16), bd_v) for g, bd_v in zip(grams, bd_vs)]
        pks = [eye_p + lp for lp in lps]
        lks = [_dot(lp.astype(BF16), bdiag(lp)) for lp in lps]
        for lvl in range(1, n_levels):
            if lvl < n_levels - 1:
                boths = [_dot(jnp.concatenate([pk, lk], axis=0).astype(BF16), bdiag(lk))
                         for pk, lk in zip(pks, lks)]
                pks = [pk + bo[0:C, :] for pk, bo in zip(pks, boths)]
                lks = [bo[C:2 * C, :] for bo in boths]
            else:
                pks = [pk + _dot(pk.astype(BF16), bdiag(lk)) for pk, lk in zip(pks, lks)]
        tws = [_dot(pk.astype(BF16), jnp.concatenate([bdiag(mv), bdiag(d["at"])], axis=1))
               for pk, mv, d in zip(pks, mvs, pre)]

        sts = [state_ref[bi, q] for bi, q in chains]
        zs = [_dot_nt(jnp.concatenate([tw[:, QW:2 * QW], d["rt"]], axis=0).astype(BF16), st.astype(BF16))
              for tw, d, st in zip(tws, pre, sts)]
        us = [tw[:, 0:QW] + z[0:C, :] for tw, z in zip(tws, zs)]
        yss = [_dot(g[C:2 * C, :].astype(BF16), jnp.concatenate([bdiag(u), bd_v], axis=0))
               for g, u, bd_v in zip(grams, us, bd_vs)]
        upds = [_dot(jnp.concatenate([d["v"], u], axis=0).T.astype(BF16), d["kb_end"])
                for d, u in zip(pre, us)]
        for i, (bi, q) in enumerate(chains):
            d = pre[i]
            state_ref[bi, q] = sts[i] * jnp.exp(d["wc"]) + jnp.where(state_blocks, upds[i], 0.0)
            y_ref[bi, :, d["lanes"]] = zs[i][C:2 * C, :] + yss[i]
        return carry

    def finish(ci):
        rows = pl.ds(pl.multiple_of(ci * C, C), C)
        for bi, q in chains:
            lanes = slice(q * QW, (q + 1) * QW)
            y = y_ref[bi, :, lanes]
            mean = head_sum(y) * (1.0 / N)
            yc = y - mean
            var = head_sum(yc * yc) * (1.0 / N)
            yn = yc * lax.rsqrt(var + RWKV_LN_EPS) * lnw_ref[:, lanes] + lnb_ref[:, lanes]
            r = r_ref[bi, rows, lanes].astype(F32)
            k = k_ref[bi, rows, lanes].astype(F32)
            v = v_ref[bi, rows, lanes].astype(F32)
            bonus = head_sum(r * k * rk_ref[:, lanes]) * v
            out = (yn + bonus) * g_ref[bi, rows, lanes].astype(F32)
            o_ref[bi, rows, lanes] = out.astype(o_ref.dtype)

    y_ref[...] = jnp.zeros_like(y_ref)

    def loop_body(ci, carry):
        finish(jnp.maximum(ci - 1, 0))
        return chunk_body(ci, carry)

    lax.fori_loop(0, n_chunks, loop_body, 0)
    finish(n_chunks - 1)


def _rwkv_scan(r, k, v, kk, a, lw, g, lnw, lnb, rk, tb, n_quads, nb):
    B, S, D = r.shape
    QW = RWKV_QUAD * RWKV_HEAD
    W = n_quads * QW
    blk = pl.BlockSpec((nb, tb, W), lambda b, j, t: (b, t, j))
    par = pl.BlockSpec((1, W), lambda b, j, t: (0, j))
    return pl.pallas_call(
        functools.partial(_rwkv_scan4_kernel, n_chunks=tb // RWKV_CHUNK, n_quads=n_quads),
        grid=(B // nb, D // W, S // tb),
        in_specs=[blk] * 7 + [par] * 3,
        out_specs=blk,
        out_shape=jax.ShapeDtypeStruct((B, S, D), BF16),
        scratch_shapes=[pltpu.VMEM((nb, n_quads, QW, QW), F32), pltpu.VMEM((nb, RWKV_CHUNK, W), F32)],
        compiler_params=_cparams(("parallel", "parallel", "arbitrary")),
    )(r, k, v, kk, a, lw, g, lnw, lnb, rk)


def _rwkv_out_kernel(y_ref, w_ref, h_ref, nw_ref, wr_ref, br_ref, o_ref, t_ref, cw_ref):
    h = h_ref[...] + _dot(y_ref[...], w_ref[...])
    o_ref[...] = h
    _route_rows(h, nw_ref, wr_ref, br_ref, t_ref, cw_ref)


def _rwkv_out(y, wo, h2, router, tm):
    T, D = h2.shape
    r_in, r_out, r_shapes = _router_specs(tm, D)
    return pl.pallas_call(
        _rwkv_out_kernel,
        grid=(T // tm,),
        in_specs=[pl.BlockSpec((tm, D), lambda i: (i, 0)), pl.BlockSpec((D, D), lambda i: (0, 0)),
                  pl.BlockSpec((tm, D), lambda i: (i, 0))] + r_in,
        out_specs=[pl.BlockSpec((tm, D), lambda i: (i, 0))] + r_out,
        out_shape=[jax.ShapeDtypeStruct((T, D), F32)] + r_shapes(T),
        compiler_params=_cparams(("parallel",)),
    )(y, wo, h2, *router)


def _route_rows(h, nw_ref, wr_ref, br_ref, t_ref, cw_ref):
    tn = _rms(h, nw_ref[...])
    t_ref[...] = tn.astype(t_ref.dtype)
    logits = _dot_x3(tn, wr_ref[...]) + br_ref[...]
    lane = lax.broadcasted_iota(jnp.int32, logits.shape, 1).astype(F32)
    neg = jnp.float32(-jnp.inf)
    big = jnp.float32(1 << 20)
    is_g = lane < MOE_GROUPS
    gl = jnp.where(is_g, logits, neg)
    gmax = jnp.max(gl, axis=-1, keepdims=True)
    g_idx = jnp.min(jnp.where(gl == gmax, lane, big), axis=-1, keepdims=True)
    g_gate = 1.0 / jnp.sum(jnp.where(is_g, jnp.exp(logits - gmax), 0.0), axis=-1, keepdims=True)
    lo = MOE_GROUPS + g_idx * MOE_EPG
    sel = (lane >= lo) & (lane < lo + MOE_EPG)
    el = jnp.where(sel, logits, neg)
    m1 = jnp.max(el, axis=-1, keepdims=True)
    i1 = jnp.min(jnp.where(el == m1, lane, big), axis=-1, keepdims=True)
    el2 = jnp.where(lane == i1, neg, el)
    m2 = jnp.max(el2, axis=-1, keepdims=True)
    i2 = jnp.min(jnp.where(el2 == m2, lane, big), axis=-1, keepdims=True)
    e21 = jnp.exp(m2 - m1)
    w1 = g_gate / (1.0 + e21)
    w2 = w1 * e21
    shift = g_idx * MOE_EPG
    cw_ref[...] = jnp.where(lane == 0.0, g_idx,
                            jnp.where(lane == i1 - shift, w1, jnp.where(lane == i2 - shift, w2, 0.0)))


def _router_specs(tm, D):
    ins = [pl.BlockSpec((1, D), lambda i: (0, 0)), pl.BlockSpec((D, ROUTER_LANES), lambda i: (0, 0)),
           pl.BlockSpec((1, ROUTER_LANES), lambda i: (0, 0))]
    outs = [pl.BlockSpec((tm, D), lambda i: (i, 0)), pl.BlockSpec((tm, ROUTER_LANES), lambda i: (i, 0))]
    shapes = lambda T: [jax.ShapeDtypeStruct((T, D), BF16), jax.ShapeDtypeStruct((T, ROUTER_LANES), F32)]
    return ins, outs, shapes


def _router_params(nw, wg, bg, we, be):
    D = D_MODEL
    pad = ROUTER_LANES - MOE_GROUPS - MOE_EXPERTS
    wr = jnp.concatenate([wg, we, jnp.zeros((D, pad), F32)], axis=1)
    br = jnp.concatenate([bg, be, jnp.zeros((pad,), F32)]).reshape(1, ROUTER_LANES)
    return nw.reshape(1, D), wr, br


def _dot_tn(a, b):
    return lax.dot_general(a, b, (((0,), (0,)), ((), ())), preferred_element_type=F32)


def _experts_kernel(t_ref, cw_ref, h_ref, w13_ref, w2_ref, fnw_ref, o_ref,
                    pt_ref, xs_ref, cws_ref, oacc_ref, meta_ref, *, final_norm):
    tm = t_ref.shape[0]
    cap = pt_ref.shape[1]
    R = MOE_SUB
    F = MOE_D_FF
    g = pl.program_id(1)
    part = pl.program_id(2)
    n_parts = MOE_EPG // MOE_EXPERTS_PER_STEP

    @pl.when((g == 0) & (part == 0))
    def _():
        cw = cw_ref[...]
        lane_t = lax.broadcasted_iota(jnp.int32, (tm, ROUTER_LANES), 1).astype(F32)
        memb = (cw[:, 0:1] == lane_t) & (lane_t < MOE_GROUPS)
        membf = jnp.where(memb, 1.0, 0.0)
        tr = lax.broadcasted_iota(jnp.int32, (tm, tm), 0)
        tc = lax.broadcasted_iota(jnp.int32, (tm, tm), 1)
        before = jnp.where(tr > tc, 1.0, 0.0).astype(BF16)
        ranks = _dot(before, membf.astype(BF16))
        cnt = jnp.sum(membf, axis=0, keepdims=True)
        nsub = jnp.floor((cnt + (R - 1.0)) * (1.0 / R))
        ur = lax.broadcasted_iota(jnp.int32, (ROUTER_LANES, ROUTER_LANES), 0)
        uc = lax.broadcasted_iota(jnp.int32, (ROUTER_LANES, ROUTER_LANES), 1)
        prefix = jnp.where(ur < uc, 1.0, 0.0).astype(BF16)
        base = _dot_x2(jnp.broadcast_to(nsub * R, (8, ROUTER_LANES)), prefix)[0:1, :]
        pos = jnp.sum(jnp.where(memb, ranks + base, 0.0), axis=-1, keepdims=True)
        col = lax.broadcasted_iota(jnp.int32, (tm, cap), 1).astype(F32)
        pt = jnp.where(pos == col, 1.0, 0.0).astype(BF16)
        pt_ref[...] = pt
        cw_hi, cw_lo = _split2(cw)
        D = t_ref.shape[1]
        gathered = _dot_tn(pt, jnp.concatenate([t_ref[...], cw_hi, cw_lo], axis=1))
        xs_ref[...] = gathered[:, 0:D].astype(BF16)
        cws_ref[...] = gathered[:, D:D + ROUTER_LANES] + gathered[:, D + ROUTER_LANES:D + 2 * ROUTER_LANES]
        oacc_ref[...] = jnp.zeros_like(oacc_ref)
        meta_ref[0:1, :] = nsub
        meta_ref[1:2, :] = base

    lane1 = lax.broadcasted_iota(jnp.int32, (1, ROUTER_LANES), 1)
    n_sub = jnp.sum(jnp.where(lane1 == g, meta_ref[0:1, :], 0.0)).astype(jnp.int32)
    row0 = jnp.sum(jnp.where(lane1 == g, meta_ref[1:2, :], 0.0)).astype(jnp.int32)
    def run_rows(start, n_rows):
        rows = pl.ds(pl.multiple_of(start, R), n_rows)
        lane_r = lax.broadcasted_iota(jnp.int32, (n_rows, ROUTER_LANES), 1)
        xs = xs_ref[rows, :]
        cws = cws_ref[rows, :]
        acc = oacc_ref[rows, :]
        for j in range(MOE_EXPERTS_PER_STEP):
            h13 = _dot(xs, w13_ref[j])
            ce = jnp.sum(jnp.where(lane_r == MOE_GROUPS + part * MOE_EXPERTS_PER_STEP + j, cws, 0.0),
                         axis=-1, keepdims=True)
            he = (_silu(h13[:, 0:F]) * h13[:, F:2 * F] * ce).astype(BF16)
            acc = acc + _dot(he, w2_ref[j])
        oacc_ref[rows, :] = acc

    big = MOE_SUBS_PER_PASS
    n_big = jnp.maximum(n_sub - MOE_MAX_PASS + big - 1, 0) // big

    def big_body(s, carry):
        run_rows(row0 + s * (big * R), big * R)
        return carry

    lax.fori_loop(0, n_big, big_body, 0)
    rem = n_sub - n_big * big
    done = row0 + n_big * (big * R)
    for k in range(1, MOE_MAX_PASS + 1):
        @pl.when(rem == k)
        def _(k=k):
            run_rows(done, k * R)

    @pl.when((g == MOE_GROUPS - 1) & (part == n_parts - 1))
    def _():
        out = h_ref[...] + _dot(pt_ref[...], oacc_ref[...].astype(BF16))
        if final_norm:
            out = _rms(out, fnw_ref[...])
        o_ref[...] = out


def _experts(t, cw, h2, w13, w2, fnw, tm, final_norm):
    T, D = h2.shape
    F = MOE_D_FF
    E = MOE_EXPERTS_PER_STEP
    n_parts = MOE_EPG // E
    cap = tm + MOE_GROUPS * MOE_SUB
    return pl.pallas_call(
        functools.partial(_experts_kernel, final_norm=final_norm),
        grid=(T // tm, MOE_GROUPS, n_parts),
        in_specs=[
            pl.BlockSpec((tm, D), lambda i, g, p: (i, 0)),
            pl.BlockSpec((tm, ROUTER_LANES), lambda i, g, p: (i, 0)),
            pl.BlockSpec((tm, D), lambda i, g, p: (i, 0)),
            pl.BlockSpec((E, D, 2 * F), lambda i, g, p: (g * n_parts + p, 0, 0)),
            pl.BlockSpec((E, F, D), lambda i, g, p: (g * n_parts + p, 0, 0)),
            pl.BlockSpec((1, D), lambda i, g, p: (0, 0)),
        ],
        out_specs=pl.BlockSpec((tm, D), lambda i, g, p: (i, 0)),
        out_shape=jax.ShapeDtypeStruct((T, D), F32),
        scratch_shapes=[
            pltpu.VMEM((tm, cap), BF16),
            pltpu.VMEM((cap, D), BF16),
            pltpu.VMEM((cap, ROUTER_LANES), F32),
            pltpu.VMEM((cap, D), F32),
            pltpu.VMEM((8, ROUTER_LANES), F32),
        ],
        compiler_params=_cparams(("parallel", "arbitrary", "arbitrary")),
    )(t, cw, h2, w13, w2, fnw)


def _moe_experts(t, cw, h2, w1, w3, w2, fnw, final_norm, tm):
    w13 = jnp.concatenate([w1, w3], axis=-1).astype(BF16)
    return _experts(t, cw, h2, w13, w2.astype(BF16), fnw.reshape(1, D_MODEL), tm, final_norm)


def _pick(n, prefs):
    for p in prefs:
        if n % p == 0:
            return p
    return n


def _ssd_retention_layer(h2, B, S, nw, w_in, conv_w, conv_b, dt_bias, a_log, d_skip, norm_w, w_out, router):
    D = D_MODEL
    T = B * S
    o_x, o_bc, o_dt, o_q, o_k, o_v, o_g = 1024, 2048, 2560, 2576, 3088, 3600, 4624
    w_main = jnp.concatenate([w_in[:, 0:o_x], w_in[:, o_x:o_bc], w_in[:, o_v:o_g], w_in[:, o_g:o_g + D],
                              w_in[:, o_bc:o_dt], w_in[:, o_q:o_k], w_in[:, o_k:o_v]], axis=1).astype(BF16)
    w_dt = jnp.pad(w_in[:, o_dt:o_q], ((0, 0), (0, LANES - SSD_HEADS))).astype(BF16)
    tm = _pick(T, (1024, 512, 256, 128))
    proj, dt = _inproj(h2, nw.reshape(1, D), w_main, w_dt, tm, 1408)
    proj3 = proj.reshape(B, S, proj.shape[1])
    dt3 = dt.reshape(B, S, LANES)

    head_of_channel = jnp.arange(D) // SSD_HEAD_DIM
    expand = (jnp.arange(LANES)[:, None] == head_of_channel[None, :]).astype(BF16)
    pad16 = lambda x: jnp.pad(x, (0, LANES - SSD_HEADS)).reshape(1, LANES)
    ssd_params = (conv_w[:, :D], conv_b[:D].reshape(1, D), conv_w[:, D:], conv_b[D:].reshape(1, 512),
                  pad16(dt_bias), pad16(a_log), expand, jnp.repeat(d_skip, SSD_HEAD_DIM).reshape(1, D),
                  norm_w.reshape(1, D))

    C = RET_CHUNK
    half = RET_QK_HEAD // 2
    pos = jnp.arange(S, dtype=F32)
    inv_freq = ROPE_BASE ** (-jnp.arange(half, dtype=F32) / half)
    ang = pos[:, None] * inv_freq[None, :]
    cc = jnp.concatenate([jnp.cos(ang), jnp.cos(ang)], axis=1)
    ss = jnp.concatenate([-jnp.sin(ang), jnp.sin(ang)], axis=1)
    log_gamma = jnp.log(1.0 - 2.0 ** (-5.0 - jnp.arange(RET_HEADS, dtype=F32)))
    idx = jnp.arange(C, dtype=F32)
    diff = idx[:, None] - idx[None, :]
    dmask = jnp.where(diff[None] >= 0, jnp.exp(jnp.maximum(diff, 0.0)[None] * log_gamma[:, None, None]), 0.0)
    qdec = jnp.repeat(jnp.exp((idx[:, None] + 1.0) * log_gamma[None, :]), RET_V_HEAD, axis=1)
    kdec = jnp.repeat(jnp.exp((C - 1.0 - idx[:, None]) * log_gamma[None, :]), RET_QK_HEAD, axis=1)
    cdec = jnp.repeat(jnp.exp(C * log_gamma), RET_V_HEAD).reshape(1, D)
    y = _mix0(proj3, dt3, ssd_params, (cc, ss, dmask, qdec, kdec, cdec))
    return _outproj(y.reshape(T, 2 * D), w_out.astype(BF16), h2, router, tm)


def _rwkv_layer(h2, B, S, nw, mu, w_r, w_k, w_v, w_o, w0, w1, w2, a0, a1, a2, g1, g2, k_k, k_a, r_k, lnx_w, lnx_b,
                router):
    D = D_MODEL
    T = B * S
    padc = lambda w, n: jnp.pad(w, ((0, 0), (0, n - w.shape[1]))).astype(BF16)
    padr = lambda w, n: jnp.pad(w, ((0, n - w.shape[0]), (0, 0))).astype(BF16)
    head_of_channel = jnp.arange(D) // RWKV_HEAD
    hs = (head_of_channel[:, None] == jnp.arange(LANES)[None, :]).astype(BF16)
    row = lambda x: x.reshape(1, D)
    tm = _pick(S, (512, 256, 128))
    r, k, v, kk, a, lw, g = _rwkv_pre(
        h2, row(nw), mu, w_r.astype(BF16), w_k.astype(BF16), w_v.astype(BF16),
        padc(w1, RWKV_LORA_PAD), padr(w2, RWKV_LORA_PAD), row(w0),
        padc(a1, RWKV_LORA_PAD), padr(a2, RWKV_LORA_PAD), row(a0),
        padc(g1, RWKV_GATE_PAD), padr(g2, RWKV_GATE_PAD), row(k_k), row(k_a), hs, hs.T, S, tm)
    sh = lambda x: x.reshape(B, S, D)
    tb = _pick(S, (512, 256, 128, 64))
    y = _rwkv_scan(sh(r), sh(k), sh(v), sh(kk), sh(a), sh(lw), sh(g),
                   row(lnx_w), row(lnx_b), r_k.reshape(1, D), tb, RWKV_HEADS // RWKV_QUAD,
                   _pick(B, (RWKV_SEQS_PER_STEP, 1)))
    return _rwkv_out(y.reshape(T, D), w_o.astype(BF16), h2, router, _pick(T, (1024, 512, 256, 128)))


def kernel(x, norm_mix_w, norm_ffn_w, norm_final_w, w_in_e, ssd_conv_w, ssd_conv_b, ssd_dt_bias, ssd_a_log, ssd_d, ssd_norm_w, w_out_e, rw_mu, rw_wr, rw_wk, rw_wv, rw_wo, rw_w0, rw_w1, rw_w2, rw_a0, rw_a1, rw_a2, rw_g1, rw_g2, rw_kk, rw_ka, rw_rk, rw_lnx_w, rw_lnx_b, moe_wg, moe_bg, moe_we, moe_be, moe_w1, moe_w3, moe_w2):
    B, S, D = x.shape
    T = B * S
    depth = norm_mix_w.shape[0]
    h = x.reshape(T, D)
    tm_moe = _pick(T, (1024, 512, 256, 128))
    for layer in range(depth):
        i = layer // 2
        router = _router_params(norm_ffn_w[layer], moe_wg[layer], moe_bg[layer], moe_we[layer], moe_be[layer])
        if layer % 2 == 0:
            h, t, cw = _ssd_retention_layer(h, B, S, norm_mix_w[layer], w_in_e[i], ssd_conv_w[i], ssd_conv_b[i],
                                            ssd_dt_bias[i], ssd_a_log[i], ssd_d[i], ssd_norm_w[i], w_out_e[i],
                                            router)
        else:
            h, t, cw = _rwkv_layer(h, B, S, norm_mix_w[layer], rw_mu[i], rw_wr[i], rw_wk[i], rw_wv[i], rw_wo[i],
                                   rw_w0[i], rw_w1[i], rw_w2[i], rw_a0[i], rw_a1[i], rw_a2[i], rw_g1[i],
                                   rw_g2[i], rw_kk[i], rw_ka[i], rw_rk[i], rw_lnx_w[i], rw_lnx_b[i], router)
        h = _moe_experts(t, cw, h, moe_w1[layer], moe_w3[layer], moe_w2[layer], norm_final_w,
                         final_norm=(layer == depth - 1), tm=tm_moe)
    return h.reshape(B, S, D)
```

```python
import functools
import math

import jax
import jax.numpy as jnp
from jax import lax
from jax.experimental import pallas as pl
from jax.experimental.pallas import tpu as pltpu

F32 = jnp.float32
BF16 = jnp.bfloat16

D_MODEL = 1024
RMS_EPS = 1e-6
SSD_HEADS = 16
SSD_HEAD_DIM = 64
SSD_GROUPS = 2
SSD_STATE = 128
SSD_CONV = 4
SSD_CHUNK = 128
MIX0_CHUNKS_PER_STEP = 2
SSD_CONV_TAIL = 16
SSD_NORM_EPS = 1e-5
SSD_GROUP_WIDTH = D_MODEL // SSD_GROUPS
RET_HEADS = 4
RET_QK_HEAD = 128
RET_V_HEAD = 256
RET_CHUNK = 128
ROPE_BASE = 10000.0
RWKV_HEAD = 64
RWKV_HEADS = 16
RWKV_LN_EPS = 64e-5
RWKV_CHUNK = 64
RWKV_QUAD = 4
RWKV_SEQS_PER_STEP = 2
RWKV_LORA_PAD = 128
RWKV_GATE_PAD = 256
RWKV_PRE_SPLIT = 2
MOE_GROUPS = 4
MOE_EPG = 4
MOE_EXPERTS = 16
MOE_D_FF = 512
ROUTER_LANES = 128
MOE_SUB = 64
MOE_SUBS_PER_PASS = 4
MOE_MAX_PASS = 6
MOE_EXPERTS_PER_STEP = 2

LANES = 128
VMEM_LIMIT_BYTES = 56 * 1024 * 1024


def _cparams(sem):
    return pltpu.CompilerParams(dimension_semantics=sem, vmem_limit_bytes=VMEM_LIMIT_BYTES)


def _dot(a, b):
    return jnp.dot(a, b, preferred_element_type=F32)


def _dot_nt(a, b):
    return lax.dot_general(a, b, (((1,), (1,)), ((), ())), preferred_element_type=F32)


def _split2(x):
    hi = x.astype(BF16)
    lo = (x - hi.astype(F32)).astype(BF16)
    return hi, lo


def _dot_x2(x, w_exact):
    hi, lo = _split2(x)
    return _dot(hi, w_exact) + _dot(lo, w_exact)


def _dot_2x(w_exact, x):
    hi, lo = _split2(x)
    return _dot(w_exact, hi) + _dot(w_exact, lo)


def _dot_x3(x, w):
    xh, xl = _split2(x)
    wh, wl = _split2(w)
    return _dot(xh, wh) + _dot(xl, wh) + _dot(xh, wl)


def _sigmoid(x):
    return 0.5 * jnp.tanh(0.5 * x) + 0.5


def _silu(x):
    hx = 0.5 * x
    return hx * jnp.tanh(hx) + hx


def _softplus(x):
    return jnp.maximum(x, 0.0) + jnp.log(1.0 + jnp.exp(-jnp.abs(x)))


def _rms(x, w, eps=RMS_EPS):
    return x * lax.rsqrt(jnp.mean(x * x, axis=-1, keepdims=True) + eps) * w


def _inproj_kernel(x_ref, nw_ref, w_ref, wdt_ref, o_ref, dt_ref, u_ref):
    @pl.when(pl.program_id(1) == 0)
    def _():
        ub = _rms(x_ref[...], nw_ref[...]).astype(BF16)
        u_ref[...] = ub
        dt_ref[...] = _dot(ub, wdt_ref[...])

    o_ref[...] = _dot(u_ref[...], w_ref[...]).astype(o_ref.dtype)


def _inproj(h2, nw, w_main, w_dt, tm, tn):
    T, D = h2.shape
    N = w_main.shape[1]
    return pl.pallas_call(
        _inproj_kernel,
        grid=(T // tm, N // tn),
        in_specs=[
            pl.BlockSpec((tm, D), lambda i, j: (i, 0)),
            pl.BlockSpec((1, D), lambda i, j: (0, 0)),
            pl.BlockSpec((D, tn), lambda i, j: (0, j)),
            pl.BlockSpec((D, LANES), lambda i, j: (0, 0)),
        ],
        out_specs=[
            pl.BlockSpec((tm, tn), lambda i, j: (i, j)),
            pl.BlockSpec((tm, LANES), lambda i, j: (i, 0)),
        ],
        out_shape=[
            jax.ShapeDtypeStruct((T, N), BF16),
            jax.ShapeDtypeStruct((T, LANES), F32),
        ],
        scratch_shapes=[pltpu.VMEM((tm, D), BF16)],
        compiler_params=_cparams(("parallel", "arbitrary")),
    )(h2, nw, w_main, w_dt)


def _ssd_chunk(z_ref, x_ref, bc_ref, dt_ref, cwx_ref, cbx_ref, cwb_ref, cbb_ref,
               dtb_ref, alog_ref, e_ref, dskip_ref, nw_ref, o_ref,
               state_ref, xtail, bctail, ybuf, *, rows, lane0):
    L = SSD_CHUNK
    TAIL = xtail.shape[0]
    srow = lax.broadcasted_iota(jnp.int32, (SSD_CONV * L, TAIL + L), 0)
    scol = lax.broadcasted_iota(jnp.int32, (SSD_CONV * L, TAIL + L), 1)
    shift = jnp.where(scol == (srow % L) + (srow // L) + (TAIL - SSD_CONV + 1), 1.0, 0.0).astype(BF16)

    def conv(cur_ref, tail, w_ref, b_ref):
        cur = cur_ref[0, rows, :]
        taps = _dot(shift, jnp.concatenate([tail[...], cur], axis=0))
        acc = b_ref[...] + w_ref[0:1, :] * taps[0:L, :]
        for k in range(1, SSD_CONV):
            acc = acc + w_ref[k:k + 1, :] * taps[k * L:(k + 1) * L, :]
        tail[...] = cur[L - TAIL:L, :]
        return _silu(acc)

    xs = conv(x_ref, xtail, cwx_ref, cbx_ref)
    bc = conv(bc_ref, bctail, cwb_ref, cbb_ref)

    dt = _softplus(dt_ref[0, rows, :] + dtb_ref[...])
    a = dt * (-jnp.exp(alog_ref[...]))
    row = lax.broadcasted_iota(jnp.int32, (L, L), 0)
    col = lax.broadcasted_iota(jnp.int32, (L, L), 1)
    causal = row >= col
    tri = jnp.where(causal, 1.0, 0.0).astype(BF16)
    acum = _dot_2x(tri, a)
    acum_t = acum.T
    e = e_ref[...]
    dt_full = _dot_x2(dt, e)
    acum_full = _dot_x2(acum, e)
    xdt = xs * dt_full
    alast_full = acum_full[L - 1:L, :]
    decay_in = jnp.exp(acum_full)
    xdt_end = (xdt * jnp.exp(alast_full - acum_full)).astype(BF16)
    xdt_b = xdt.astype(BF16)
    lane = lax.broadcasted_iota(jnp.int32, (L, LANES), 1)
    first_head = lane < SSD_HEAD_DIM

    GW = SSD_GROUP_WIDTH
    for g in range(SSD_GROUPS):
        bg = bc[:, g * SSD_STATE:(g + 1) * SSD_STATE]
        cg = bc[:, (SSD_GROUPS + g) * SSD_STATE:(SSD_GROUPS + g + 1) * SSD_STATE].astype(BF16)
        cb = _dot_nt(cg, bg.astype(BF16))
        st = state_ref[g]
        y_off = _dot(cg, st.astype(BF16)) * decay_in[:, g * GW:(g + 1) * GW]
        for p in range(GW // LANES):
            xp = xdt_b[:, g * GW + p * LANES:g * GW + (p + 1) * LANES]
            ys = []
            for s in range(2):
                hd = g * (SSD_HEADS // SSD_GROUPS) + 2 * p + s
                seg = acum[:, hd:hd + 1] - acum_t[hd:hd + 1, :]
                dec = jnp.where(causal, jnp.exp(seg), 0.0)
                ys.append(_dot((cb * dec).astype(BF16), xp))
            yd = jnp.where(first_head, ys[0], ys[1])
            lo = g * GW + p * LANES
            ybuf[:, lo:lo + LANES] = yd + y_off[:, p * LANES:(p + 1) * LANES]
        bg_t = bg.T.astype(BF16)
        state_ref[g] = st * jnp.exp(alast_full[:, g * GW:(g + 1) * GW]) + _dot(
            bg_t, xdt_end[:, g * GW:(g + 1) * GW])

    y = ybuf[...] + xs * dskip_ref[...]
    y = y * _silu(z_ref[0, rows, :].astype(F32))
    for g in range(SSD_GROUPS):
        yg = y[:, g * GW:(g + 1) * GW]
        yg = yg * lax.rsqrt(jnp.mean(yg * yg, axis=-1, keepdims=True) + SSD_NORM_EPS)
        lo = lane0 + g * GW
        o_ref[0, rows, lo:lo + GW] = (yg * nw_ref[:, g * GW:(g + 1) * GW]).astype(o_ref.dtype)


def _ret_chunk(q_ref, k_ref, v_ref, g_ref, cc_ref, ss_ref, dmask_ref, qdec_ref, kdec_ref,
               cdec_ref, o_ref, r_ref, *, rows, lane0):
    cc = cc_ref[rows, :]
    ss = ss_ref[rows, :]
    dk, dv = RET_QK_HEAD, RET_V_HEAD
    half = dk // 2

    def rope(x):
        return x * cc + pltpu.roll(x, half, 1) * ss

    heads = range(RET_HEADS)
    qs = [rope(q_ref[0, rows, hd * dk:(hd + 1) * dk].astype(F32)) for hd in heads]
    ks = [rope(k_ref[0, rows, hd * dk:(hd + 1) * dk].astype(F32)) * (dk ** -0.5) for hd in heads]
    vs = [v_ref[0, rows, hd * dv:(hd + 1) * dv] for hd in heads]
    qbs = [q.astype(BF16) for q in qs]
    ss = [_dot_nt(qbs[hd], ks[hd].astype(BF16)) * dmask_ref[hd] for hd in heads]
    r_olds = [r_ref[hd] for hd in heads]
    cross = [_dot(qbs[hd], r_olds[hd].astype(BF16)) for hd in heads]
    kd_ts = [(ks[hd] * kdec_ref[:, hd * dk:(hd + 1) * dk]).T.astype(BF16) for hd in heads]
    upds = [_dot(kd_ts[hd], vs[hd]) for hd in heads]
    inner = [_dot(ss[hd].astype(BF16), vs[hd]) for hd in heads]
    for hd in heads:
        r_ref[hd] = r_olds[hd] * cdec_ref[:, hd * dv:(hd + 1) * dv] + upds[hd]
        y = inner[hd] + cross[hd] * qdec_ref[:, hd * dv:(hd + 1) * dv]
        y = y * lax.rsqrt(jnp.mean(y * y, axis=-1, keepdims=True) + RMS_EPS)
        y = y * _silu(g_ref[0, rows, hd * dv:(hd + 1) * dv].astype(F32))
        lo = lane0 + hd * dv
        o_ref[0, rows, lo:lo + dv] = y.astype(o_ref.dtype)


N_SSD_IN = 13
N_RET_IN = 10


def _mix0_kernel(*refs):
    ssd_in = refs[:N_SSD_IN]
    ret_in = refs[N_SSD_IN:N_SSD_IN + N_RET_IN]
    o_ref, state_ref, xtail, bctail, ybuf, r_ref = refs[N_SSD_IN + N_RET_IN:]

    @pl.when(pl.program_id(1) == 0)
    def _():
        state_ref[...] = jnp.zeros_like(state_ref)
        xtail[...] = jnp.zeros_like(xtail)
        bctail[...] = jnp.zeros_like(bctail)
        r_ref[...] = jnp.zeros_like(r_ref)

    for ci in range(ybuf.shape[0]):
        rows = slice(ci * SSD_CHUNK, (ci + 1) * SSD_CHUNK)
        _ssd_chunk(*ssd_in, o_ref, state_ref, xtail, bctail, ybuf.at[ci], rows=rows, lane0=0)
        _ret_chunk(*ret_in, o_ref, r_ref, rows=rows, lane0=D_MODEL)


def _mix0(proj3, dt3, ssd_params, ret_params):
    B, S, _ = proj3.shape
    n_chunks = _pick(S // SSD_CHUNK, (MIX0_CHUNKS_PER_STEP, 1))
    L = n_chunks * SSD_CHUNK
    D = D_MODEL
    assert RET_CHUNK == SSD_CHUNK and len(ssd_params) + 4 == N_SSD_IN and len(ret_params) + 4 == N_RET_IN
    full = lambda arr: pl.BlockSpec(arr.shape, lambda b, c: (0,) * arr.ndim)
    cc, ss = ret_params[:2]
    ret_specs = [pl.BlockSpec((L, RET_QK_HEAD), lambda b, c: (c, 0))] * 2 + [full(p) for p in ret_params[2:]]
    return pl.pallas_call(
        _mix0_kernel,
        grid=(B, S // L),
        in_specs=[
            pl.BlockSpec((1, L, D), lambda b, c: (b, c, 0)),
            pl.BlockSpec((1, L, D), lambda b, c: (b, c, 1)),
            pl.BlockSpec((1, L, 512), lambda b, c: (b, c, 8)),
            pl.BlockSpec((1, L, LANES), lambda b, c: (b, c, 0)),
        ] + [full(p) for p in ssd_params] + [
            pl.BlockSpec((1, L, 512), lambda b, c: (b, c, 9)),
            pl.BlockSpec((1, L, 512), lambda b, c: (b, c, 10)),
            pl.BlockSpec((1, L, D), lambda b, c: (b, c, 2)),
            pl.BlockSpec((1, L, D), lambda b, c: (b, c, 3)),
        ] + ret_specs,
        out_specs=pl.BlockSpec((1, L, 2 * D), lambda b, c: (b, c, 0)),
        out_shape=jax.ShapeDtypeStruct((B, S, 2 * D), BF16),
        scratch_shapes=[
            pltpu.VMEM((SSD_GROUPS, SSD_STATE, SSD_GROUP_WIDTH), F32),
            pltpu.VMEM((SSD_CONV_TAIL, D), BF16),
            pltpu.VMEM((SSD_CONV_TAIL, 512), BF16),
            pltpu.VMEM((n_chunks, SSD_CHUNK, D), F32),
            pltpu.VMEM((RET_HEADS, RET_QK_HEAD, RET_V_HEAD), F32),
        ],
        compiler_params=_cparams(("parallel", "arbitrary")),
    )(proj3, proj3, proj3, dt3, *ssd_params, proj3, proj3, proj3, proj3, *ret_params)


def _outproj_kernel(y_ref, w_ref, h_ref, nw_ref, wr_ref, br_ref, o_ref, t_ref, cw_ref):
    h = h_ref[...] + _dot(y_ref[...], w_ref[...])
    o_ref[...] = h
    _route_rows(h, nw_ref, wr_ref, br_ref, t_ref, cw_ref)


def _outproj(y, w, h2, router, tm):
    T, D = h2.shape
    K = y.shape[1]
    r_in, r_out, r_shapes = _router_specs(tm, D)
    return pl.pallas_call(
        _outproj_kernel,
        grid=(T // tm,),
        in_specs=[
            pl.BlockSpec((tm, K), lambda i: (i, 0)),
            pl.BlockSpec((K, D), lambda i: (0, 0)),
            pl.BlockSpec((tm, D), lambda i: (i, 0)),
        ] + r_in,
        out_specs=[pl.BlockSpec((tm, D), lambda i: (i, 0))] + r_out,
        out_shape=[jax.ShapeDtypeStruct((T, D), F32)] + r_shapes(T),
        compiler_params=_cparams(("parallel",)),
    )(y, w, h2, *router)


def _rwkv_pre_kernel(h_ref, hp_ref, nw_ref, mu_ref, wr_ref, wk_ref, wv_ref,
                     w1_ref, w2_ref, w0_ref, a1_ref, a2_ref, a0_ref, g1_ref, g2_ref,
                     kkw_ref, kaw_ref, hs_ref, hst_ref,
                     r_o, k_o, v_o, kk_o, a_o, lw_o, g_o, ubuf, xxbuf, *, tiles_per_seq):
    tm = h_ref.shape[0]
    i = pl.program_id(0)
    nw = nw_ref[...]
    u = _rms(h_ref[...], nw)
    up = _rms(hp_ref[...], nw)
    seq_start = (i % tiles_per_seq) == 0
    ubuf[8:8 + tm, :] = u
    ubuf[0:8, :] = jnp.where(seq_start, 0.0, up)
    xxbuf[...] = ubuf[7:7 + tm, :] - u

    def mix(j, lo, n):
        return (ubuf[8 + lo:8 + lo + n, :] + xxbuf[lo:lo + n, :] * mu_ref[j:j + 1, :]).astype(BF16)

    n = tm // RWKV_PRE_SPLIT
    parts = [s * n for s in range(RWKV_PRE_SPLIT)]
    first = []
    for lo in parts:
        first.append(dict(
            r=_dot(mix(0, lo, n), wr_ref[...]), k=_dot(mix(2, lo, n), wk_ref[...]),
            v=_dot(mix(3, lo, n), wv_ref[...]), w1=_dot(mix(1, lo, n), w1_ref[...]),
            a1=_dot(mix(4, lo, n), a1_ref[...]), g1=_dot(mix(5, lo, n), g1_ref[...])))
    second = []
    for f in first:
        kk = f["k"] * kkw_ref[...]
        second.append(dict(
            wl=w0_ref[...] + _dot(jnp.tanh(f["w1"]).astype(BF16), w2_ref[...]),
            al=a0_ref[...] + _dot(f["a1"].astype(BF16), a2_ref[...]),
            g=_dot(_sigmoid(f["g1"]).astype(BF16), g2_ref[...]),
            kk=kk, ssq=_dot((kk * kk).astype(BF16), hs_ref[...])))
    for lo, f, s in zip(parts, first, second):
        rows = slice(lo, lo + n)
        w_log = -_softplus(-s["wl"]) - 0.5
        a = _sigmoid(s["al"])
        inv = lax.rsqrt(jnp.maximum(s["ssq"], 1e-24))
        r_o[rows, :] = f["r"].astype(r_o.dtype)
        k_o[rows, :] = (f["k"] * (1.0 + (a - 1.0) * kaw_ref[...])).astype(k_o.dtype)
        v_o[rows, :] = f["v"].astype(v_o.dtype)
        kk_o[rows, :] = (s["kk"] * _dot_x2(inv, hst_ref[...])).astype(kk_o.dtype)
        a_o[rows, :] = a.astype(a_o.dtype)
        lw_o[rows, :] = -jnp.exp(w_log)
        g_o[rows, :] = s["g"].astype(g_o.dtype)


def _rwkv_pre(h2, nw, mu, wr, wk, wv, w1, w2, w0, a1, a2, a0, g1, g2, kkw, kaw, hs, hst, S, tm):
    T, D = h2.shape
    full = lambda arr: pl.BlockSpec(arr.shape, lambda i: (0,) * arr.ndim)
    row = pl.BlockSpec((tm, D), lambda i: (i, 0))
    params = (nw, mu, wr, wk, wv, w1, w2, w0, a1, a2, a0, g1, g2, kkw, kaw, hs, hst)
    bf = jax.ShapeDtypeStruct((T, D), BF16)
    return pl.pallas_call(
        functools.partial(_rwkv_pre_kernel, tiles_per_seq=S // tm),
        grid=(T // tm,),
        in_specs=[row, pl.BlockSpec((8, D), lambda i: (jnp.maximum(i * (tm // 8) - 1, 0), 0))]
        + [full(p) for p in params],
        out_specs=[row] * 7,
        out_shape=[bf, bf, bf, bf, bf, jax.ShapeDtypeStruct((T, D), F32), bf],
        scratch_shapes=[pltpu.VMEM((tm + 8, D), F32), pltpu.VMEM((tm, D), F32)],
        compiler_params=_cparams(("parallel",)),
    )(h2, h2, *params)


def _rwkv_scan4_kernel(r_ref, k_ref, v_ref, kk_ref, a_ref, lw_ref, g_ref, lnw_ref, lnb_ref, rk_ref,
                       o_ref, state_ref, y_ref, *, n_chunks, n_quads):
    C = RWKV_CHUNK
    N = RWKV_HEAD
    HQ = RWKV_QUAD
    QW = HQ * N
    t = pl.program_id(2)

    @pl.when(t == 0)
    def _():
        state_ref[...] = jnp.zeros_like(state_ref)

    row = lax.broadcasted_iota(jnp.int32, (C, C), 0)
    col = lax.broadcasted_iota(jnp.int32, (C, C), 1)
    tri_incl = jnp.where(row >= col, 1.0, 0.0).astype(BF16)
    prow = lax.broadcasted_iota(jnp.int32, (C, QW), 0)
    plane = lax.broadcasted_iota(jnp.int32, (C, QW), 1)
    eye_p = jnp.where(prow == plane % C, 1.0, 0.0).astype(F32)
    head_masks = [plane // N == h for h in range(HQ)]
    grow = lax.broadcasted_iota(jnp.int32, (2 * C, 2 * QW), 0)
    gcol = lax.broadcasted_iota(jnp.int32, (2 * C, 2 * QW), 1)
    gmask = (grow % C + grow // C) > (gcol % C)
    srow = lax.broadcasted_iota(jnp.int32, (QW, QW), 0)
    scol = lax.broadcasted_iota(jnp.int32, (QW, QW), 1)
    state_blocks = (srow // N) == (scol // N)
    n_levels = int(math.log2(C))
    n_batch = r_ref.shape[0]
    chains = [(bi, q) for bi in range(n_batch) for q in range(n_quads)]

    head_ones = [jnp.where(m, 1.0, 0.0).astype(BF16) for m in head_masks]

    def bdiag(x):
        xb = x.astype(BF16)
        return jnp.concatenate([xb * one for one in head_ones], axis=0)

    first_of_pair = lax.broadcasted_iota(jnp.int32, (C, LANES), 1) < N

    def head_sum(x):
        halves = []
        for lo in range(0, QW, LANES):
            xh = x[:, lo:lo + LANES]
            total = jnp.sum(xh, axis=-1, keepdims=True)
            first = jnp.sum(jnp.where(first_of_pair, xh, 0.0), axis=-1, keepdims=True)
            halves.append(jnp.where(first_of_pair, first, total - first))
        return jnp.concatenate(halves, axis=1)

    def chunk_body(ci, carry):
        rows = pl.ds(pl.multiple_of(ci * C, C), C)
        cum_all = [_dot_2x(tri_incl, lw_ref[bi, rows, :]) for bi in range(n_batch)]
        pre = []
        for bi, q in chains:
            lanes = slice(q * QW, (q + 1) * QW)
            r = r_ref[bi, rows, lanes].astype(F32)
            k = k_ref[bi, rows, lanes].astype(F32)
            v = v_ref[bi, rows, lanes].astype(F32)
            kk = kk_ref[bi, rows, lanes].astype(F32)
            a = a_ref[bi, rows, lanes].astype(F32)
            lw = lw_ref[bi, rows, lanes]
            cum = cum_all[bi][:, lanes]
            wc = cum[C - 1:C, :]
            e_out = jnp.exp(-cum)
            e_end = jnp.exp(wc - cum)
            b = kk * a
            at = -kk * jnp.exp(cum - lw)
            rt = r * jnp.exp(cum)
            pre.append(dict(lanes=lanes, r=r, k=k, v=v, wc=wc, at=at, rt=rt,
                            bh=b * e_out, kh=k * e_out,
                            kb_end=jnp.concatenate([k * e_end, b * e_end], axis=0).astype(BF16)))

        grams = []
        for d in pre:
            lhs = jnp.concatenate([d["at"], d["rt"]], axis=0).astype(BF16)
            rhs = jnp.concatenate([bdiag(d["bh"]), bdiag(d["kh"])], axis=0)
            grams.append(jnp.where(gmask, _dot_nt(lhs, rhs), 0.0))
        lps = [g[0:C, 0:QW] for g in grams]
        bd_vs = [bdiag(d["v"]) for d in pre]
        mvs = [_dot(g[0:C, QW:2 * QW].astype(BF16), bd_v) for g, bd_v in zip(grams, bd_vs)]
        pks = [eye_p + lp for lp in lps]
        lks = [_dot(lp.astype(BF16), bdiag(lp)) for lp in lps]
        for lvl in range(1, n_levels):
            if lvl < n_levels - 1:
                boths = [_dot(jnp.concatenate([pk, lk], axis=0).astype(BF16), bdiag(lk))
                         for pk, lk in zip(pks, lks)]
                pks = [pk + bo[0:C, :] for pk, bo in zip(pks, boths)]
                lks = [bo[C:2 * C, :] for bo in boths]
            else:
                pks = [pk + _dot(pk.astype(BF16), bdiag(lk)) for pk, lk in zip(pks, lks)]
        tws = [_dot(pk.astype(BF16), jnp.concatenate([bdiag(mv), bdiag(d["at"])], axis=1))
               for pk, mv, d in zip(pks, mvs, pre)]

        sts = [state_ref[bi, q] for bi, q in chains]
        zs = [_dot_nt(jnp.concatenate([tw[:, QW:2 * QW], d["rt"]], axis=0).astype(BF16), st.astype(BF16))
              for tw, d, st in zip(tws, pre, sts)]
        us = [tw[:, 0:QW] + z[0:C, :] for tw, z in zip(tws, zs)]
        yss = [_dot(g[C:2 * C, :].astype(BF16), jnp.concatenate([bdiag(u), bd_v], axis=0))
               for g, u, bd_v in zip(grams, us, bd_vs)]
        upds = [_dot(jnp.concatenate([d["v"], u], axis=0).T.astype(BF16), d["kb_end"])
                for d, u in zip(pre, us)]
        for i, (bi, q) in enumerate(chains):
            d = pre[i]
            state_ref[bi, q] = sts[i] * jnp.exp(d["wc"]) + jnp.where(state_blocks, upds[i], 0.0)
            y_ref[bi, :, d["lanes"]] = zs[i][C:2 * C, :] + yss[i]
        return carry

    def finish(ci):
        rows = pl.ds(pl.multiple_of(ci * C, C), C)
        for bi, q in chains:
            lanes = slice(q * QW, (q + 1) * QW)
            y = y_ref[bi, :, lanes]
            mean = head_sum(y) * (1.0 / N)
            yc = y - mean
            var = head_sum(yc * yc) * (1.0 / N)
            yn = yc * lax.rsqrt(var + RWKV_LN_EPS) * lnw_ref[:, lanes] + lnb_ref[:, lanes]
            r = r_ref[bi, rows, lanes].astype(F32)
            k = k_ref[bi, rows, lanes].astype(F32)
            v = v_ref[bi, rows, lanes].astype(F32)
            bonus = head_sum(r * k * rk_ref[:, lanes]) * v
            out = (yn + bonus) * g_ref[bi, rows, lanes].astype(F32)
            o_ref[bi, rows, lanes] = out.astype(o_ref.dtype)

    y_ref[...] = jnp.zeros_like(y_ref)

    def loop_body(ci, carry):
        finish(jnp.maximum(ci - 1, 0))
        return chunk_body(ci, carry)

    lax.fori_loop(0, n_chunks, loop_body, 0)
    finish(n_chunks - 1)


def _rwkv_scan(r, k, v, kk, a, lw, g, lnw, lnb, rk, tb, n_quads, nb):
    B, S, D = r.shape
    QW = RWKV_QUAD * RWKV_HEAD
    W = n_quads * QW
    blk = pl.BlockSpec((nb, tb, W), lambda b, j, t: (b, t, j))
    par = pl.BlockSpec((1, W), lambda b, j, t: (0, j))
    return pl.pallas_call(
        functools.partial(_rwkv_scan4_kernel, n_chunks=tb // RWKV_CHUNK, n_quads=n_quads),
        grid=(B // nb, D // W, S // tb),
        in_specs=[blk] * 7 + [par] * 3,
        out_specs=blk,
        out_shape=jax.ShapeDtypeStruct((B, S, D), BF16),
        scratch_shapes=[pltpu.VMEM((nb, n_quads, QW, QW), F32), pltpu.VMEM((nb, RWKV_CHUNK, W), F32)],
        compiler_params=_cparams(("parallel", "parallel", "arbitrary")),
    )(r, k, v, kk, a, lw, g, lnw, lnb, rk)


def _rwkv_out_kernel(y_ref, w_ref, h_ref, nw_ref, wr_ref, br_ref, o_ref, t_ref, cw_ref):
    h = h_ref[...] + _dot(y_ref[...], w_ref[...])
    o_ref[...] = h
    _route_rows(h, nw_ref, wr_ref, br_ref, t_ref, cw_ref)


def _rwkv_out(y, wo, h2, router, tm):
    T, D = h2.shape
    r_in, r_out, r_shapes = _router_specs(tm, D)
    return pl.pallas_call(
        _rwkv_out_kernel,
        grid=(T // tm,),
        in_specs=[pl.BlockSpec((tm, D), lambda i: (i, 0)), pl.BlockSpec((D, D), lambda i: (0, 0)),
                  pl.BlockSpec((tm, D), lambda i: (i, 0))] + r_in,
        out_specs=[pl.BlockSpec((tm, D), lambda i: (i, 0))] + r_out,
        out_shape=[jax.ShapeDtypeStruct((T, D), F32)] + r_shapes(T),
        compiler_params=_cparams(("parallel",)),
    )(y, wo, h2, *router)


def _route_rows(h, nw_ref, wr_ref, br_ref, t_ref, cw_ref):
    tn = _rms(h, nw_ref[...])
    t_ref[...] = tn.astype(t_ref.dtype)
    logits = _dot_x3(tn, wr_ref[...]) + br_ref[...]
    lane = lax.broadcasted_iota(jnp.int32, logits.shape, 1).astype(F32)
    neg = jnp.float32(-jnp.inf)
    big = jnp.float32(1 << 20)
    is_g = lane < MOE_GROUPS
    gl = jnp.where(is_g, logits, neg)
    gmax = jnp.max(gl, axis=-1, keepdims=True)
    g_idx = jnp.min(jnp.where(gl == gmax, lane, big), axis=-1, keepdims=True)
    g_gate = 1.0 / jnp.sum(jnp.where(is_g, jnp.exp(logits - gmax), 0.0), axis=-1, keepdims=True)
    lo = MOE_GROUPS + g_idx * MOE_EPG
    sel = (lane >= lo) & (lane < lo + MOE_EPG)
    el = jnp.where(sel, logits, neg)
    m1 = jnp.max(el, axis=-1, keepdims=True)
    i1 = jnp.min(jnp.where(el == m1, lane, big), axis=-1, keepdims=True)
    el2 = jnp.where(lane == i1, neg, el)
    m2 = jnp.max(el2, axis=-1, keepdims=True)
    i2 = jnp.min(jnp.where(el2 == m2, lane, big), axis=-1, keepdims=True)
    e21 = jnp.exp(m2 - m1)
    w1 = g_gate / (1.0 + e21)
    w2 = w1 * e21
    shift = g_idx * MOE_EPG
    cw_ref[...] = jnp.where(lane == 0.0, g_idx,
                            jnp.where(lane == i1 - shift, w1, jnp.where(lane == i2 - shift, w2, 0.0)))


def _router_specs(tm, D):
    ins = [pl.BlockSpec((1, D), lambda i: (0, 0)), pl.BlockSpec((D, ROUTER_LANES), lambda i: (0, 0)),
           pl.BlockSpec((1, ROUTER_LANES), lambda i: (0, 0))]
    outs = [pl.BlockSpec((tm, D), lambda i: (i, 0)), pl.BlockSpec((tm, ROUTER_LANES), lambda i: (i, 0))]
    shapes = lambda T: [jax.ShapeDtypeStruct((T, D), BF16), jax.ShapeDtypeStruct((T, ROUTER_LANES), F32)]
    return ins, outs, shapes


def _router_params(nw, wg, bg, we, be):
    D = D_MODEL
    pad = ROUTER_LANES - MOE_GROUPS - MOE_EXPERTS
    wr = jnp.concatenate([wg, we, jnp.zeros((D, pad), F32)], axis=1)
    br = jnp.concatenate([bg, be, jnp.zeros((pad,), F32)]).reshape(1, ROUTER_LANES)
    return nw.reshape(1, D), wr, br


def _dot_tn(a, b):
    return lax.dot_general(a, b, (((0,), (0,)), ((), ())), preferred_element_type=F32)


def _experts_kernel(t_ref, cw_ref, h_ref, w13_ref, w2_ref, fnw_ref, o_ref,
                    pt_ref, xs_ref, cws_ref, oacc_ref, meta_ref, *, final_norm):
    tm = t_ref.shape[0]
    cap = pt_ref.shape[1]
    R = MOE_SUB
    F = MOE_D_FF
    g = pl.program_id(1)
    part = pl.program_id(2)
    n_parts = MOE_EPG // MOE_EXPERTS_PER_STEP

    @pl.when((g == 0) & (part == 0))
    def _():
        cw = cw_ref[...]
        lane_t = lax.broadcasted_iota(jnp.int32, (tm, ROUTER_LANES), 1).astype(F32)
        memb = (cw[:, 0:1] == lane_t) & (lane_t < MOE_GROUPS)
        membf = jnp.where(memb, 1.0, 0.0)
        tr = lax.broadcasted_iota(jnp.int32, (tm, tm), 0)
        tc = lax.broadcasted_iota(jnp.int32, (tm, tm), 1)
        before = jnp.where(tr > tc, 1.0, 0.0).astype(BF16)
        ranks = _dot(before, membf.astype(BF16))
        cnt = jnp.sum(membf, axis=0, keepdims=True)
        nsub = jnp.floor((cnt + (R - 1.0)) * (1.0 / R))
        ur = lax.broadcasted_iota(jnp.int32, (ROUTER_LANES, ROUTER_LANES), 0)
        uc = lax.broadcasted_iota(jnp.int32, (ROUTER_LANES, ROUTER_LANES), 1)
        prefix = jnp.where(ur < uc, 1.0, 0.0).astype(BF16)
        base = _dot_x2(jnp.broadcast_to(nsub * R, (8, ROUTER_LANES)), prefix)[0:1, :]
        pos = jnp.sum(jnp.where(memb, ranks + base, 0.0), axis=-1, keepdims=True)
        col = lax.broadcasted_iota(jnp.int32, (tm, cap), 1).astype(F32)
        pt = jnp.where(pos == col, 1.0, 0.0).astype(BF16)
        pt_ref[...] = pt
        cw_hi, cw_lo = _split2(cw)
        D = t_ref.shape[1]
        gathered = _dot_tn(pt, jnp.concatenate([t_ref[...], cw_hi, cw_lo], axis=1))
        xs_ref[...] = gathered[:, 0:D].astype(BF16)
        cws_ref[...] = gathered[:, D:D + ROUTER_LANES] + gathered[:, D + ROUTER_LANES:D + 2 * ROUTER_LANES]
        oacc_ref[...] = jnp.zeros_like(oacc_ref)
        meta_ref[0:1, :] = nsub
        meta_ref[1:2, :] = base

    lane1 = lax.broadcasted_iota(jnp.int32, (1, ROUTER_LANES), 1)
    n_sub = jnp.sum(jnp.where(lane1 == g, meta_ref[0:1, :], 0.0)).astype(jnp.int32)
    row0 = jnp.sum(jnp.where(lane1 == g, meta_ref[1:2, :], 0.0)).astype(jnp.int32)
    def run_rows(start, n_rows):
        rows = pl.ds(pl.multiple_of(start, R), n_rows)
        lane_r = lax.broadcasted_iota(jnp.int32, (n_rows, ROUTER_LANES), 1)
        xs = xs_ref[rows, :]
        cws = cws_ref[rows, :]
        acc = oacc_ref[rows, :]
        for j in range(MOE_EXPERTS_PER_STEP):
            h13 = _dot(xs, w13_ref[j])
            ce = jnp.sum(jnp.where(lane_r == MOE_GROUPS + part * MOE_EXPERTS_PER_STEP + j, cws, 0.0),
                         axis=-1, keepdims=True)
            he = (_silu(h13[:, 0:F]) * h13[:, F:2 * F] * ce).astype(BF16)
            acc = acc + _dot(he, w2_ref[j])
        oacc_ref[rows, :] = acc

    big = MOE_SUBS_PER_PASS
    n_big = jnp.maximum(n_sub - MOE_MAX_PASS + big - 1, 0) // big

    def big_body(s, carry):
        run_rows(row0 + s * (big * R), big * R)
        return carry

    lax.fori_loop(0, n_big, big_body, 0)
    rem = n_sub - n_big * big
    done = row0 + n_big * (big * R)
    for k in range(1, MOE_MAX_PASS + 1):
        @pl.when(rem == k)
        def _(k=k):
            run_rows(done, k * R)

    @pl.when((g == MOE_GROUPS - 1) & (part == n_parts - 1))
    def _():
        out = h_ref[...] + _dot(pt_ref[...], oacc_ref[...].astype(BF16))
        if final_norm:
            out = _rms(out, fnw_ref[...])
        o_ref[...] = out


def _experts(t, cw, h2, w13, w2, fnw, tm, final_norm):
    T, D = h2.shape
    F = MOE_D_FF
    E = MOE_EXPERTS_PER_STEP
    n_parts = MOE_EPG // E
    cap = tm + MOE_GROUPS * MOE_SUB
    return pl.pallas_call(
        functools.partial(_experts_kernel, final_norm=final_norm),
        grid=(T // tm, MOE_GROUPS, n_parts),
        in_specs=[
            pl.BlockSpec((tm, D), lambda i, g, p: (i, 0)),
            pl.BlockSpec((tm, ROUTER_LANES), lambda i, g, p: (i, 0)),
            pl.BlockSpec((tm, D), lambda i, g, p: (i, 0)),
            pl.BlockSpec((E, D, 2 * F), lambda i, g, p: (g * n_parts + p, 0, 0)),
            pl.BlockSpec((E, F, D), lambda i, g, p: (g * n_parts + p, 0, 0)),
            pl.BlockSpec((1, D), lambda i, g, p: (0, 0)),
        ],
        out_specs=pl.BlockSpec((tm, D), lambda i, g, p: (i, 0)),
        out_shape=jax.ShapeDtypeStruct((T, D), F32),
        scratch_shapes=[
            pltpu.VMEM((tm, cap), BF16),
            pltpu.VMEM((cap, D), BF16),
            pltpu.VMEM((cap, ROUTER_LANES), F32),
            pltpu.VMEM((cap, D), F32),
            pltpu.VMEM((8, ROUTER_LANES), F32),
        ],
        compiler_params=_cparams(("parallel", "arbitrary", "arbitrary")),
    )(t, cw, h2, w13, w2, fnw)


def _moe_experts(t, cw, h2, w1, w3, w2, fnw, final_norm, tm):
    w13 = jnp.concatenate([w1, w3], axis=-1).astype(BF16)
    return _experts(t, cw, h2, w13, w2.astype(BF16), fnw.reshape(1, D_MODEL), tm, final_norm)


def _pick(n, prefs):
    for p in prefs:
        if n % p == 0:
            return p
    return n


def _ssd_retention_layer(h2, B, S, nw, w_in, conv_w, conv_b, dt_bias, a_log, d_skip, norm_w, w_out, router):
    D = D_MODEL
    T = B * S
    o_x, o_bc, o_dt, o_q, o_k, o_v, o_g = 1024, 2048, 2560, 2576, 3088, 3600, 4624
    w_main = jnp.concatenate([w_in[:, 0:o_x], w_in[:, o_x:o_bc], w_in[:, o_v:o_g], w_in[:, o_g:o_g + D],
                              w_in[:, o_bc:o_dt], w_in[:, o_q:o_k], w_in[:, o_k:o_v]], axis=1).astype(BF16)
    w_dt = jnp.pad(w_in[:, o_dt:o_q], ((0, 0), (0, LANES - SSD_HEADS))).astype(BF16)
    tm = _pick(T, (1024, 512, 256, 128))
    proj, dt = _inproj(h2, nw.reshape(1, D), w_main, w_dt, _pick(T, (2048, 1024, 512, 256, 128)), 1408)
    proj3 = proj.reshape(B, S, proj.shape[1])
    dt3 = dt.reshape(B, S, LANES)

    head_of_channel = jnp.arange(D) // SSD_HEAD_DIM
    expand = (jnp.arange(LANES)[:, None] == head_of_channel[None, :]).astype(BF16)
    pad16 = lambda x: jnp.pad(x, (0, LANES - SSD_HEADS)).reshape(1, LANES)
    ssd_params = (conv_w[:, :D], conv_b[:D].reshape(1, D), conv_w[:, D:], conv_b[D:].reshape(1, 512),
                  pad16(dt_bias), pad16(a_log), expand, jnp.repeat(d_skip, SSD_HEAD_DIM).reshape(1, D),
                  norm_w.reshape(1, D))

    C = RET_CHUNK
    half = RET_QK_HEAD // 2
    pos = jnp.arange(S, dtype=F32)
    inv_freq = ROPE_BASE ** (-jnp.arange(half, dtype=F32) / half)
    ang = pos[:, None] * inv_freq[None, :]
    cc = jnp.concatenate([jnp.cos(ang), jnp.cos(ang)], axis=1)
    ss = jnp.concatenate([-jnp.sin(ang), jnp.sin(ang)], axis=1)
    log_gamma = jnp.log(1.0 - 2.0 ** (-5.0 - jnp.arange(RET_HEADS, dtype=F32)))
    idx = jnp.arange(C, dtype=F32)
    diff = idx[:, None] - idx[None, :]
    dmask = jnp.where(diff[None] >= 0, jnp.exp(jnp.maximum(diff, 0.0)[None] * log_gamma[:, None, None]), 0.0)
    qdec = jnp.repeat(jnp.exp((idx[:, None] + 1.0) * log_gamma[None, :]), RET_V_HEAD, axis=1)
    kdec = jnp.repeat(jnp.exp((C - 1.0 - idx[:, None]) * log_gamma[None, :]), RET_QK_HEAD, axis=1)
    cdec = jnp.repeat(jnp.exp(C * log_gamma), RET_V_HEAD).reshape(1, D)
    y = _mix0(proj3, dt3, ssd_params, (cc, ss, dmask, qdec, kdec, cdec))
    return _outproj(y.reshape(T, 2 * D), w_out.astype(BF16), h2, router, tm)


def _rwkv_layer(h2, B, S, nw, mu, w_r, w_k, w_v, w_o, w0, w1, w2, a0, a1, a2, g1, g2, k_k, k_a, r_k, lnx_w, lnx_b,
                router):
    D = D_MODEL
    T = B * S
    padc = lambda w, n: jnp.pad(w, ((0, 0), (0, n - w.shape[1]))).astype(BF16)
    padr = lambda w, n: jnp.pad(w, ((0, n - w.shape[0]), (0, 0))).astype(BF16)
    head_of_channel = jnp.arange(D) // RWKV_HEAD
    hs = (head_of_channel[:, None] == jnp.arange(LANES)[None, :]).astype(BF16)
    row = lambda x: x.reshape(1, D)
    tm = _pick(S, (512, 256, 128))
    r, k, v, kk, a, lw, g = _rwkv_pre(
        h2, row(nw), mu, w_r.astype(BF16), w_k.astype(BF16), w_v.astype(BF16),
        padc(w1, RWKV_LORA_PAD), padr(w2, RWKV_LORA_PAD), row(w0),
        padc(a1, RWKV_LORA_PAD), padr(a2, RWKV_LORA_PAD), row(a0),
        padc(g1, RWKV_GATE_PAD), padr(g2, RWKV_GATE_PAD), row(k_k), row(k_a), hs, hs.T, S, tm)
    sh = lambda x: x.reshape(B, S, D)
    tb = _pick(S, (512, 256, 128, 64))
    y = _rwkv_scan(sh(r), sh(k), sh(v), sh(kk), sh(a), sh(lw), sh(g),
                   row(lnx_w), row(lnx_b), r_k.reshape(1, D), tb, RWKV_HEADS // RWKV_QUAD,
                   _pick(B, (RWKV_SEQS_PER_STEP, 1)))
    return _rwkv_out(y.reshape(T, D), w_o.astype(BF16), h2, router, _pick(T, (1024, 512, 256, 128)))


def kernel(x, norm_mix_w, norm_ffn_w, norm_final_w, w_in_e, ssd_conv_w, ssd_conv_b, ssd_dt_bias, ssd_a_log, ssd_d, ssd_norm_w, w_out_e, rw_mu, rw_wr, rw_wk, rw_wv, rw_wo, rw_w0, rw_w1, rw_w2, rw_a0, rw_a1, rw_a2, rw_g1, rw_g2, rw_kk, rw_ka, rw_rk, rw_lnx_w, rw_lnx_b, moe_wg, moe_bg, moe_we, moe_be, moe_w1, moe_w3, moe_w2):
    B, S, D = x.shape
    T = B * S
    depth = norm_mix_w.shape[0]
    h = x.reshape(T, D)
    tm_moe = _pick(T, (1024, 512, 256, 128))
    for layer in range(depth):
        i = layer // 2
        router = _router_params(norm_ffn_w[layer], moe_wg[layer], moe_bg[layer], moe_we[layer], moe_be[layer])
        if layer % 2 == 0:
            h, t, cw = _ssd_retention_layer(h, B, S, norm_mix_w[layer], w_in_e[i], ssd_conv_w[i], ssd_conv_b[i],
                                            ssd_dt_bias[i], ssd_a_log[i], ssd_d[i], ssd_norm_w[i], w_out_e[i],
                                            router)
        else:
            h, t, cw = _rwkv_layer(h, B, S, norm_mix_w[layer], rw_mu[i], rw_wr[i], rw_wk[i], rw_wv[i], rw_wo[i],
                                   rw_w0[i], rw_w1[i], rw_w2[i], rw_a0[i], rw_a1[i], rw_a2[i], rw_g1[i],
                                   rw_g2[i], rw_kk[i], rw_ka[i], rw_rk[i], rw_lnx_w[i], rw_lnx_b[i], router)
        h = _moe_experts(t, cw, h, moe_w1[layer], moe_w3[layer], moe_w2[layer], norm_final_w,
                         final_norm=(layer == depth - 1), tm=tm_moe)
    return h.reshape(B, S, D)
```

```python
import functools
import math

import jax
import jax.numpy as jnp
from jax import lax
from jax.experimental import pallas as pl
from jax.experimental.pallas import tpu as pltpu

F32 = jnp.float32
BF16 = jnp.bfloat16

D_MODEL = 1024
RMS_EPS = 1e-6
SSD_HEADS = 16
SSD_HEAD_DIM = 64
SSD_GROUPS = 2
SSD_STATE = 128
SSD_CONV = 4
SSD_CHUNK = 128
MIX0_CHUNKS_PER_STEP = 4
SSD_CONV_TAIL = 16
SSD_NORM_EPS = 1e-5
SSD_GROUP_WIDTH = D_MODEL // SSD_GROUPS
RET_HEADS = 4
RET_QK_HEAD = 128
RET_V_HEAD = 256
RET_CHUNK = 128
ROPE_BASE = 10000.0
RWKV_HEAD = 64
RWKV_HEADS = 16
RWKV_LN_EPS = 64e-5
RWKV_CHUNK = 64
RWKV_QUAD = 4
RWKV_SEQS_PER_STEP = 2
RWKV_LORA_PAD = 128
RWKV_GATE_PAD = 256
RWKV_PRE_SPLIT = 2
MOE_GROUPS = 4
MOE_EPG = 4
MOE_EXPERTS = 16
MOE_D_FF = 512
ROUTER_LANES = 128
MOE_SUB = 64
MOE_SUBS_PER_PASS = 4
MOE_MAX_PASS = 6
MOE_EXPERTS_PER_STEP = 2

LANES = 128
SUBLANES = 8
VMEM_LIMIT_BYTES = 56 * 1024 * 1024

PROJ_NARROW = 512
PROJ_BLOCK_Z, PROJ_BLOCK_X, PROJ_BLOCK_V, PROJ_BLOCK_G = 0, 1, 2, 3
PROJ_BLOCK_BC, PROJ_BLOCK_Q, PROJ_BLOCK_K = 8, 9, 10
PROJ_WIDTH = 4 * D_MODEL + 3 * PROJ_NARROW
INPROJ_TN = PROJ_WIDTH // 4


def _cparams(sem):
    return pltpu.CompilerParams(dimension_semantics=sem, vmem_limit_bytes=VMEM_LIMIT_BYTES)


def _dot(a, b):
    return jnp.dot(a, b, preferred_element_type=F32)


def _dot_nt(a, b):
    return lax.dot_general(a, b, (((1,), (1,)), ((), ())), preferred_element_type=F32)


def _split2(x):
    hi = x.astype(BF16)
    lo = (x - hi.astype(F32)).astype(BF16)
    return hi, lo


def _dot_x2(x, w_exact):
    hi, lo = _split2(x)
    return _dot(hi, w_exact) + _dot(lo, w_exact)


def _dot_2x(w_exact, x):
    hi, lo = _split2(x)
    return _dot(w_exact, hi) + _dot(w_exact, lo)


def _dot_x3(x, w):
    xh, xl = _split2(x)
    wh, wl = _split2(w)
    return _dot(xh, wh) + _dot(xl, wh) + _dot(xh, wl)


def _sigmoid(x):
    return 0.5 * jnp.tanh(0.5 * x) + 0.5


def _silu(x):
    hx = 0.5 * x
    return hx * jnp.tanh(hx) + hx


def _softplus(x):
    return jnp.maximum(x, 0.0) + jnp.log(1.0 + jnp.exp(-jnp.abs(x)))


def _rms(x, w, eps=RMS_EPS):
    return x * lax.rsqrt(jnp.mean(x * x, axis=-1, keepdims=True) + eps) * w


def _inproj_kernel(x_ref, nw_ref, w_ref, wdt_ref, o_ref, dt_ref, u_ref):
    @pl.when(pl.program_id(1) == 0)
    def _():
        ub = _rms(x_ref[...], nw_ref[...]).astype(BF16)
        u_ref[...] = ub
        dt_ref[...] = _dot(ub, wdt_ref[...])

    o_ref[...] = _dot(u_ref[...], w_ref[...]).astype(o_ref.dtype)


def _inproj(h2, nw, w_main, w_dt, tm, tn):
    T, D = h2.shape
    N = w_main.shape[1]
    return pl.pallas_call(
        _inproj_kernel,
        grid=(T // tm, N // tn),
        in_specs=[
            pl.BlockSpec((tm, D), lambda i, j: (i, 0)),
            pl.BlockSpec((1, D), lambda i, j: (0, 0)),
            pl.BlockSpec((D, tn), lambda i, j: (0, j)),
            pl.BlockSpec((D, LANES), lambda i, j: (0, 0)),
        ],
        out_specs=[
            pl.BlockSpec((tm, tn), lambda i, j: (i, j)),
            pl.BlockSpec((tm, LANES), lambda i, j: (i, 0)),
        ],
        out_shape=[
            jax.ShapeDtypeStruct((T, N), BF16),
            jax.ShapeDtypeStruct((T, LANES), F32),
        ],
        scratch_shapes=[pltpu.VMEM((tm, D), BF16)],
        compiler_params=_cparams(("parallel", "arbitrary")),
    )(h2, nw, w_main, w_dt)


def _ssd_chunk(z_ref, x_ref, bc_ref, dt_ref, cwx_ref, cbx_ref, cwb_ref, cbb_ref,
               dtb_ref, alog_ref, e_ref, dskip_ref, nw_ref, o_ref,
               state_ref, xtail, bctail, ybuf, *, rows, lane0):
    L = SSD_CHUNK
    TAIL = xtail.shape[0]
    srow = lax.broadcasted_iota(jnp.int32, (SSD_CONV * L, TAIL + L), 0)
    scol = lax.broadcasted_iota(jnp.int32, (SSD_CONV * L, TAIL + L), 1)
    shift = jnp.where(scol == (srow % L) + (srow // L) + (TAIL - SSD_CONV + 1), 1.0, 0.0).astype(BF16)

    def conv(cur_ref, tail, w_ref, b_ref):
        cur = cur_ref[0, rows, :]
        taps = _dot(shift, jnp.concatenate([tail[...], cur], axis=0))
        acc = b_ref[...] + w_ref[0:1, :] * taps[0:L, :]
        for k in range(1, SSD_CONV):
            acc = acc + w_ref[k:k + 1, :] * taps[k * L:(k + 1) * L, :]
        tail[...] = cur[L - TAIL:L, :]
        return _silu(acc)

    xs = conv(x_ref, xtail, cwx_ref, cbx_ref)
    bc = conv(bc_ref, bctail, cwb_ref, cbb_ref)

    dt = _softplus(dt_ref[0, rows, :] + dtb_ref[...])
    a = dt * (-jnp.exp(alog_ref[...]))
    row = lax.broadcasted_iota(jnp.int32, (L, L), 0)
    col = lax.broadcasted_iota(jnp.int32, (L, L), 1)
    causal = row >= col
    tri = jnp.where(causal, 1.0, 0.0).astype(BF16)
    acum = _dot_2x(tri, a)
    acum_t = acum.T
    e = e_ref[...]
    dt_full = _dot_x2(dt, e)
    acum_full = _dot_x2(acum, e)
    xdt = xs * dt_full
    alast_full = acum_full[L - 1:L, :]
    decay_in = jnp.exp(acum_full)
    xdt_end = (xdt * jnp.exp(alast_full - acum_full)).astype(BF16)
    xdt_b = xdt.astype(BF16)
    lane = lax.broadcasted_iota(jnp.int32, (L, LANES), 1)
    first_head = lane < SSD_HEAD_DIM

    GW = SSD_GROUP_WIDTH
    for g in range(SSD_GROUPS):
        bg = bc[:, g * SSD_STATE:(g + 1) * SSD_STATE]
        cg = bc[:, (SSD_GROUPS + g) * SSD_STATE:(SSD_GROUPS + g + 1) * SSD_STATE].astype(BF16)
        cb = _dot_nt(cg, bg.astype(BF16))
        st = state_ref[g]
        y_off = _dot(cg, st.astype(BF16)) * decay_in[:, g * GW:(g + 1) * GW]
        for p in range(GW // LANES):
            xp = xdt_b[:, g * GW + p * LANES:g * GW + (p + 1) * LANES]
            ys = []
            for s in range(2):
                hd = g * (SSD_HEADS // SSD_GROUPS) + 2 * p + s
                seg = acum[:, hd:hd + 1] - acum_t[hd:hd + 1, :]
                dec = jnp.where(causal, jnp.exp(seg), 0.0)
                ys.append(_dot((cb * dec).astype(BF16), xp))
            yd = jnp.where(first_head, ys[0], ys[1])
            lo = g * GW + p * LANES
            ybuf[:, lo:lo + LANES] = yd + y_off[:, p * LANES:(p + 1) * LANES]
        bg_t = bg.T.astype(BF16)
        state_ref[g] = st * jnp.exp(alast_full[:, g * GW:(g + 1) * GW]) + _dot(
            bg_t, xdt_end[:, g * GW:(g + 1) * GW])

    y = ybuf[...] + xs * dskip_ref[...]
    y = y * _silu(z_ref[0, rows, :].astype(F32))
    for g in range(SSD_GROUPS):
        yg = y[:, g * GW:(g + 1) * GW]
        yg = yg * lax.rsqrt(jnp.mean(yg * yg, axis=-1, keepdims=True) + SSD_NORM_EPS)
        lo = lane0 + g * GW
        o_ref[0, rows, lo:lo + GW] = (yg * nw_ref[:, g * GW:(g + 1) * GW]).astype(o_ref.dtype)


def _ret_chunk(q_ref, k_ref, v_ref, g_ref, cc_ref, ss_ref, dmask_ref, qdec_ref, kdec_ref,
               cdec_ref, o_ref, r_ref, *, rows, lane0):
    cc = cc_ref[rows, :]
    ss = ss_ref[rows, :]
    dk, dv = RET_QK_HEAD, RET_V_HEAD
    half = dk // 2

    def rope(x):
        return x * cc + pltpu.roll(x, half, 1) * ss

    heads = range(RET_HEADS)
    qs = [rope(q_ref[0, rows, hd * dk:(hd + 1) * dk].astype(F32)) for hd in heads]
    ks = [rope(k_ref[0, rows, hd * dk:(hd + 1) * dk].astype(F32)) * (dk ** -0.5) for hd in heads]
    vs = [v_ref[0, rows, hd * dv:(hd + 1) * dv] for hd in heads]
    qbs = [q.astype(BF16) for q in qs]
    ss = [_dot_nt(qbs[hd], ks[hd].astype(BF16)) * dmask_ref[hd] for hd in heads]
    r_olds = [r_ref[hd] for hd in heads]
    cross = [_dot(qbs[hd], r_olds[hd].astype(BF16)) for hd in heads]
    kd_ts = [(ks[hd] * kdec_ref[:, hd * dk:(hd + 1) * dk]).T.astype(BF16) for hd in heads]
    upds = [_dot(kd_ts[hd], vs[hd]) for hd in heads]
    inner = [_dot(ss[hd].astype(BF16), vs[hd]) for hd in heads]
    for hd in heads:
        r_ref[hd] = r_olds[hd] * cdec_ref[:, hd * dv:(hd + 1) * dv] + upds[hd]
        y = inner[hd] + cross[hd] * qdec_ref[:, hd * dv:(hd + 1) * dv]
        y = y * lax.rsqrt(jnp.mean(y * y, axis=-1, keepdims=True) + RMS_EPS)
        y = y * _silu(g_ref[0, rows, hd * dv:(hd + 1) * dv].astype(F32))
        lo = lane0 + hd * dv
        o_ref[0, rows, lo:lo + dv] = y.astype(o_ref.dtype)


N_SSD_IN = 13
N_RET_IN = 10


def _mix0_kernel(*refs):
    ssd_in = refs[:N_SSD_IN]
    ret_in = refs[N_SSD_IN:N_SSD_IN + N_RET_IN]
    o_ref, state_ref, xtail, bctail, ybuf, r_ref = refs[N_SSD_IN + N_RET_IN:]

    @pl.when(pl.program_id(1) == 0)
    def _():
        state_ref[...] = jnp.zeros_like(state_ref)
        xtail[...] = jnp.zeros_like(xtail)
        bctail[...] = jnp.zeros_like(bctail)
        r_ref[...] = jnp.zeros_like(r_ref)

    for ci in range(ybuf.shape[0]):
        rows = slice(ci * SSD_CHUNK, (ci + 1) * SSD_CHUNK)
        _ssd_chunk(*ssd_in, o_ref, state_ref, xtail, bctail, ybuf.at[ci], rows=rows, lane0=0)
        _ret_chunk(*ret_in, o_ref, r_ref, rows=rows, lane0=D_MODEL)


def _mix0(proj3, dt3, ssd_params, ret_params):
    B, S, _ = proj3.shape
    n_chunks = _pick(S // SSD_CHUNK, (MIX0_CHUNKS_PER_STEP, 1))
    L = n_chunks * SSD_CHUNK
    D = D_MODEL
    assert RET_CHUNK == SSD_CHUNK and len(ssd_params) + 4 == N_SSD_IN and len(ret_params) + 4 == N_RET_IN
    full = lambda arr: pl.BlockSpec(arr.shape, lambda b, c: (0,) * arr.ndim)
    cc, ss = ret_params[:2]
    ret_specs = [pl.BlockSpec((L, RET_QK_HEAD), lambda b, c: (c, 0))] * 2 + [full(p) for p in ret_params[2:]]
    return pl.pallas_call(
        _mix0_kernel,
        grid=(B, S // L),
        in_specs=[
            pl.BlockSpec((1, L, D), lambda b, c: (b, c, PROJ_BLOCK_Z)),
            pl.BlockSpec((1, L, D), lambda b, c: (b, c, PROJ_BLOCK_X)),
            pl.BlockSpec((1, L, PROJ_NARROW), lambda b, c: (b, c, PROJ_BLOCK_BC)),
            pl.BlockSpec((1, L, LANES), lambda b, c: (b, c, 0)),
        ] + [full(p) for p in ssd_params] + [
            pl.BlockSpec((1, L, PROJ_NARROW), lambda b, c: (b, c, PROJ_BLOCK_Q)),
            pl.BlockSpec((1, L, PROJ_NARROW), lambda b, c: (b, c, PROJ_BLOCK_K)),
            pl.BlockSpec((1, L, D), lambda b, c: (b, c, PROJ_BLOCK_V)),
            pl.BlockSpec((1, L, D), lambda b, c: (b, c, PROJ_BLOCK_G)),
        ] + ret_specs,
        out_specs=pl.BlockSpec((1, L, 2 * D), lambda b, c: (b, c, 0)),
        out_shape=jax.ShapeDtypeStruct((B, S, 2 * D), BF16),
        scratch_shapes=[
            pltpu.VMEM((SSD_GROUPS, SSD_STATE, SSD_GROUP_WIDTH), F32),
            pltpu.VMEM((SSD_CONV_TAIL, D), BF16),
            pltpu.VMEM((SSD_CONV_TAIL, PROJ_NARROW), BF16),
            pltpu.VMEM((n_chunks, SSD_CHUNK, D), F32),
            pltpu.VMEM((RET_HEADS, RET_QK_HEAD, RET_V_HEAD), F32),
        ],
        compiler_params=_cparams(("parallel", "arbitrary")),
    )(proj3, proj3, proj3, dt3, *ssd_params, proj3, proj3, proj3, proj3, *ret_params)


def _outproj_kernel(y_ref, w_ref, h_ref, nw_ref, wr_ref, br_ref, o_ref, t_ref, cw_ref):
    h = h_ref[...] + _dot(y_ref[...], w_ref[...])
    o_ref[...] = h
    _route_rows(h, nw_ref, wr_ref, br_ref, t_ref, cw_ref)


def _outproj(y, w, h2, router, tm):
    T, D = h2.shape
    K = y.shape[1]
    r_in, r_out, r_shapes = _router_specs(tm, D)
    return pl.pallas_call(
        _outproj_kernel,
        grid=(T // tm,),
        in_specs=[
            pl.BlockSpec((tm, K), lambda i: (i, 0)),
            pl.BlockSpec((K, D), lambda i: (0, 0)),
            pl.BlockSpec((tm, D), lambda i: (i, 0)),
        ] + r_in,
        out_specs=[pl.BlockSpec((tm, D), lambda i: (i, 0))] + r_out,
        out_shape=[jax.ShapeDtypeStruct((T, D), F32)] + r_shapes(T),
        compiler_params=_cparams(("parallel",)),
    )(y, w, h2, *router)


def _rwkv_pre_kernel(h_ref, hp_ref, nw_ref, mu_ref, wr_ref, wk_ref, wv_ref,
                     w1_ref, w2_ref, w0_ref, a1_ref, a2_ref, a0_ref, g1_ref, g2_ref,
                     kkw_ref, kaw_ref, hs_ref, hst_ref,
                     r_o, k_o, v_o, kk_o, a_o, lw_o, g_o, ubuf, xxbuf, *, tiles_per_seq):
    tm = h_ref.shape[0]
    i = pl.program_id(0)
    nw = nw_ref[...]
    u = _rms(h_ref[...], nw)
    up = _rms(hp_ref[...], nw)
    seq_start = (i % tiles_per_seq) == 0
    P = SUBLANES
    ubuf[P:P + tm, :] = u
    ubuf[0:P, :] = jnp.where(seq_start, 0.0, up)
    xxbuf[...] = ubuf[P - 1:P - 1 + tm, :] - u

    def mix(j, lo, n):
        return (ubuf[P + lo:P + lo + n, :] + xxbuf[lo:lo + n, :] * mu_ref[j:j + 1, :]).astype(BF16)

    n = tm // RWKV_PRE_SPLIT
    parts = [s * n for s in range(RWKV_PRE_SPLIT)]
    first = []
    for lo in parts:
        first.append(dict(
            r=_dot(mix(0, lo, n), wr_ref[...]), k=_dot(mix(2, lo, n), wk_ref[...]),
            v=_dot(mix(3, lo, n), wv_ref[...]), w1=_dot(mix(1, lo, n), w1_ref[...]),
            a1=_dot(mix(4, lo, n), a1_ref[...]), g1=_dot(mix(5, lo, n), g1_ref[...])))
    second = []
    for f in first:
        kk = f["k"] * kkw_ref[...]
        second.append(dict(
            wl=w0_ref[...] + _dot(jnp.tanh(f["w1"]).astype(BF16), w2_ref[...]),
            al=a0_ref[...] + _dot(f["a1"].astype(BF16), a2_ref[...]),
            g=_dot(_sigmoid(f["g1"]).astype(BF16), g2_ref[...]),
            kk=kk, ssq=_dot((kk * kk).astype(BF16), hs_ref[...])))
    for lo, f, s in zip(parts, first, second):
        rows = slice(lo, lo + n)
        w_log = -_softplus(-s["wl"]) - 0.5
        a = _sigmoid(s["al"])
        inv = lax.rsqrt(jnp.maximum(s["ssq"], 1e-24))
        r_o[rows, :] = f["r"].astype(r_o.dtype)
        k_o[rows, :] = (f["k"] * (1.0 + (a - 1.0) * kaw_ref[...])).astype(k_o.dtype)
        v_o[rows, :] = f["v"].astype(v_o.dtype)
        kk_o[rows, :] = (s["kk"] * _dot_x2(inv, hst_ref[...])).astype(kk_o.dtype)
        a_o[rows, :] = a.astype(a_o.dtype)
        lw_o[rows, :] = -jnp.exp(w_log)
        g_o[rows, :] = s["g"].astype(g_o.dtype)


def _rwkv_pre(h2, nw, mu, wr, wk, wv, w1, w2, w0, a1, a2, a0, g1, g2, kkw, kaw, hs, hst, S, tm):
    T, D = h2.shape
    full = lambda arr: pl.BlockSpec(arr.shape, lambda i: (0,) * arr.ndim)
    row = pl.BlockSpec((tm, D), lambda i: (i, 0))
    params = (nw, mu, wr, wk, wv, w1, w2, w0, a1, a2, a0, g1, g2, kkw, kaw, hs, hst)
    bf = jax.ShapeDtypeStruct((T, D), BF16)
    return pl.pallas_call(
        functools.partial(_rwkv_pre_kernel, tiles_per_seq=S // tm),
        grid=(T // tm,),
        in_specs=[row, pl.BlockSpec((SUBLANES, D), lambda i: (jnp.maximum(i * (tm // SUBLANES) - 1, 0), 0))]
        + [full(p) for p in params],
        out_specs=[row] * 7,
        out_shape=[bf, bf, bf, bf, bf, jax.ShapeDtypeStruct((T, D), F32), bf],
        scratch_shapes=[pltpu.VMEM((tm + SUBLANES, D), F32), pltpu.VMEM((tm, D), F32)],
        compiler_params=_cparams(("parallel",)),
    )(h2, h2, *params)


def _rwkv_scan4_kernel(r_ref, k_ref, v_ref, kk_ref, a_ref, lw_ref, g_ref, lnw_ref, lnb_ref, rk_ref,
                       o_ref, state_ref, y_ref, *, n_chunks, n_quads):
    C = RWKV_CHUNK
    N = RWKV_HEAD
    HQ = RWKV_QUAD
    QW = HQ * N
    t = pl.program_id(2)

    @pl.when(t == 0)
    def _():
        state_ref[...] = jnp.zeros_like(state_ref)

    row = lax.broadcasted_iota(jnp.int32, (C, C), 0)
    col = lax.broadcasted_iota(jnp.int32, (C, C), 1)
    tri_incl = jnp.where(row >= col, 1.0, 0.0).astype(BF16)
    prow = lax.broadcasted_iota(jnp.int32, (C, QW), 0)
    plane = lax.broadcasted_iota(jnp.int32, (C, QW), 1)
    eye_p = jnp.where(prow == plane % C, 1.0, 0.0).astype(F32)
    head_masks = [plane // N == h for h in range(HQ)]
    grow = lax.broadcasted_iota(jnp.int32, (2 * C, 2 * QW), 0)
    gcol = lax.broadcasted_iota(jnp.int32, (2 * C, 2 * QW), 1)
    gmask = (grow % C + grow // C) > (gcol % C)
    srow = lax.broadcasted_iota(jnp.int32, (QW, QW), 0)
    scol = lax.broadcasted_iota(jnp.int32, (QW, QW), 1)
    state_blocks = (srow // N) == (scol // N)
    n_levels = int(math.log2(C))
    n_batch = r_ref.shape[0]
    chains = [(bi, q) for bi in range(n_batch) for q in range(n_quads)]

    head_ones = [jnp.where(m, 1.0, 0.0).astype(BF16) for m in head_masks]

    def bdiag(x):
        xb = x.astype(BF16)
        return jnp.concatenate([xb * one for one in head_ones], axis=0)

    first_of_pair = lax.broadcasted_iota(jnp.int32, (C, LANES), 1) < N

    def head_sum(x):
        halves = []
        for lo in range(0, QW, LANES):
            xh = x[:, lo:lo + LANES]
            total = jnp.sum(xh, axis=-1, keepdims=True)
            first = jnp.sum(jnp.where(first_of_pair, xh, 0.0), axis=-1, keepdims=True)
            halves.append(jnp.where(first_of_pair, first, total - first))
        return jnp.concatenate(halves, axis=1)

    def chunk_body(ci, carry):
        rows = pl.ds(pl.multiple_of(ci * C, C), C)
        cum_all = [_dot_2x(tri_incl, lw_ref[bi, rows, :]) for bi in range(n_batch)]
        pre = []
        for bi, q in chains:
            lanes = slice(q * QW, (q + 1) * QW)
            r = r_ref[bi, rows, lanes].astype(F32)
            k = k_ref[bi, rows, lanes].astype(F32)
            v = v_ref[bi, rows, lanes].astype(F32)
            kk = kk_ref[bi, rows, lanes].astype(F32)
            a = a_ref[bi, rows, lanes].astype(F32)
            lw = lw_ref[bi, rows, lanes]
            cum = cum_all[bi][:, lanes]
            wc = cum[C - 1:C, :]
            e_out = jnp.exp(-cum)
            e_end = jnp.exp(wc - cum)
            b = kk * a
            at = -kk * jnp.exp(cum - lw)
            rt = r * jnp.exp(cum)
            pre.append(dict(lanes=lanes, r=r, k=k, v=v, wc=wc, at=at, rt=rt,
                            bh=b * e_out, kh=k * e_out,
                            kb_end=jnp.concatenate([k * e_end, b * e_end], axis=0).astype(BF16)))

        grams = []
        for d in pre:
            lhs = jnp.concatenate([d["at"], d["rt"]], axis=0).astype(BF16)
            rhs = jnp.concatenate([bdiag(d["bh"]), bdiag(d["kh"])], axis=0)
            grams.append(jnp.where(gmask, _dot_nt(lhs, rhs), 0.0))
        lps = [g[0:C, 0:QW] for g in grams]
        bd_vs = [bdiag(d["v"]) for d in pre]
        mvs = [_dot(g[0:C, QW:2 * QW].astype(BF16), bd_v) for g, bd_v in zip(grams, bd_vs)]
        pks = [eye_p + lp for lp in lps]
        lks = [_dot(lp.astype(BF16), bdiag(lp)) for lp in lps]
        for lvl in range(1, n_levels):
            if lvl < n_levels - 1:
                boths = [_dot(jnp.concatenate([pk, lk], axis=0).astype(BF16), bdiag(lk))
                         for pk, lk in zip(pks, lks)]
                pks = [pk + bo[0:C, :] for pk, bo in zip(pks, boths)]
                lks = [bo[C:2 * C, :] for bo in boths]
            else:
                pks = [pk + _dot(pk.astype(BF16), bdiag(lk)) for pk, lk in zip(pks, lks)]
        tws = [_dot(pk.astype(BF16), jnp.concatenate([bdiag(mv), bdiag(d["at"])], axis=1))
               for pk, mv, d in zip(pks, mvs, pre)]

        sts = [state_ref[bi, q] for bi, q in chains]
        zs = [_dot_nt(jnp.concatenate([tw[:, QW:2 * QW], d["rt"]], axis=0).astype(BF16), st.astype(BF16))
              for tw, d, st in zip(tws, pre, sts)]
        us = [tw[:, 0:QW] + z[0:C, :] for tw, z in zip(tws, zs)]
        yss = [_dot(g[C:2 * C, :].astype(BF16), jnp.concatenate([bdiag(u), bd_v], axis=0))
               for g, u, bd_v in zip(grams, us, bd_vs)]
        upds = [_dot(jnp.concatenate([d["v"], u], axis=0).T.astype(BF16), d["kb_end"])
                for d, u in zip(pre, us)]
        for i, (bi, q) in enumerate(chains):
            d = pre[i]
            state_ref[bi, q] = sts[i] * jnp.exp(d["wc"]) + jnp.where(state_blocks, upds[i], 0.0)
            y_ref[bi, :, d["lanes"]] = zs[i][C:2 * C, :] + yss[i]
        return carry

    def finish(ci):
        rows = pl.ds(pl.multiple_of(ci * C, C), C)
        for bi, q in chains:
            lanes = slice(q * QW, (q + 1) * QW)
            y = y_ref[bi, :, lanes]
            mean = head_sum(y) * (1.0 / N)
            yc = y - mean
            var = head_sum(yc * yc) * (1.0 / N)
            yn = yc * lax.rsqrt(var + RWKV_LN_EPS) * lnw_ref[:, lanes] + lnb_ref[:, lanes]
            r = r_ref[bi, rows, lanes].astype(F32)
            k = k_ref[bi, rows, lanes].astype(F32)
            v = v_ref[bi, rows, lanes].astype(F32)
            bonus = head_sum(r * k * rk_ref[:, lanes]) * v
            out = (yn + bonus) * g_ref[bi, rows, lanes].astype(F32)
            o_ref[bi, rows, lanes] = out.astype(o_ref.dtype)

    y_ref[...] = jnp.zeros_like(y_ref)

    def loop_body(ci, carry):
        finish(jnp.maximum(ci - 1, 0))
        return chunk_body(ci, carry)

    lax.fori_loop(0, n_chunks, loop_body, 0)
    finish(n_chunks - 1)


def _rwkv_scan(r, k, v, kk, a, lw, g, lnw, lnb, rk, tb, n_quads, nb):
    B, S, D = r.shape
    QW = RWKV_QUAD * RWKV_HEAD
    W = n_quads * QW
    blk = pl.BlockSpec((nb, tb, W), lambda b, j, t: (b, t, j))
    par = pl.BlockSpec((1, W), lambda b, j, t: (0, j))
    return pl.pallas_call(
        functools.partial(_rwkv_scan4_kernel, n_chunks=tb // RWKV_CHUNK, n_quads=n_quads),
        grid=(B // nb, D // W, S // tb),
        in_specs=[blk] * 7 + [par] * 3,
        out_specs=blk,
        out_shape=jax.ShapeDtypeStruct((B, S, D), BF16),
        scratch_shapes=[pltpu.VMEM((nb, n_quads, QW, QW), F32), pltpu.VMEM((nb, RWKV_CHUNK, W), F32)],
        compiler_params=_cparams(("parallel", "parallel", "arbitrary")),
    )(r, k, v, kk, a, lw, g, lnw, lnb, rk)


def _rwkv_out_kernel(y_ref, w_ref, h_ref, nw_ref, wr_ref, br_ref, o_ref, t_ref, cw_ref):
    h = h_ref[...] + _dot(y_ref[...], w_ref[...])
    o_ref[...] = h
    _route_rows(h, nw_ref, wr_ref, br_ref, t_ref, cw_ref)


def _rwkv_out(y, wo, h2, router, tm):
    T, D = h2.shape
    r_in, r_out, r_shapes = _router_specs(tm, D)
    return pl.pallas_call(
        _rwkv_out_kernel,
        grid=(T // tm,),
        in_specs=[pl.BlockSpec((tm, D), lambda i: (i, 0)), pl.BlockSpec((D, D), lambda i: (0, 0)),
                  pl.BlockSpec((tm, D), lambda i: (i, 0))] + r_in,
        out_specs=[pl.BlockSpec((tm, D), lambda i: (i, 0))] + r_out,
        out_shape=[jax.ShapeDtypeStruct((T, D), F32)] + r_shapes(T),
        compiler_params=_cparams(("parallel",)),
    )(y, wo, h2, *router)


def _route_rows(h, nw_ref, wr_ref, br_ref, t_ref, cw_ref):
    tn = _rms(h, nw_ref[...])
    t_ref[...] = tn.astype(t_ref.dtype)
    logits = _dot_x3(tn, wr_ref[...]) + br_ref[...]
    lane = lax.broadcasted_iota(jnp.int32, logits.shape, 1).astype(F32)
    neg = jnp.float32(-jnp.inf)
    big = jnp.float32(1 << 20)
    is_g = lane < MOE_GROUPS
    gl = jnp.where(is_g, logits, neg)
    gmax = jnp.max(gl, axis=-1, keepdims=True)
    g_idx = jnp.min(jnp.where(gl == gmax, lane, big), axis=-1, keepdims=True)
    g_gate = 1.0 / jnp.sum(jnp.where(is_g, jnp.exp(logits - gmax), 0.0), axis=-1, keepdims=True)
    lo = MOE_GROUPS + g_idx * MOE_EPG
    sel = (lane >= lo) & (lane < lo + MOE_EPG)
    el = jnp.where(sel, logits, neg)
    m1 = jnp.max(el, axis=-1, keepdims=True)
    i1 = jnp.min(jnp.where(el == m1, lane, big), axis=-1, keepdims=True)
    el2 = jnp.where(lane == i1, neg, el)
    m2 = jnp.max(el2, axis=-1, keepdims=True)
    i2 = jnp.min(jnp.where(el2 == m2, lane, big), axis=-1, keepdims=True)
    e21 = jnp.exp(m2 - m1)
    w1 = g_gate / (1.0 + e21)
    w2 = w1 * e21
    shift = g_idx * MOE_EPG
    cw_ref[...] = jnp.where(lane == 0.0, g_idx,
                            jnp.where(lane == i1 - shift, w1, jnp.where(lane == i2 - shift, w2, 0.0)))


def _router_specs(tm, D):
    ins = [pl.BlockSpec((1, D), lambda i: (0, 0)), pl.BlockSpec((D, ROUTER_LANES), lambda i: (0, 0)),
           pl.BlockSpec((1, ROUTER_LANES), lambda i: (0, 0))]
    outs = [pl.BlockSpec((tm, D), lambda i: (i, 0)), pl.BlockSpec((tm, ROUTER_LANES), lambda i: (i, 0))]
    shapes = lambda T: [jax.ShapeDtypeStruct((T, D), BF16), jax.ShapeDtypeStruct((T, ROUTER_LANES), F32)]
    return ins, outs, shapes


def _router_params(nw, wg, bg, we, be):
    D = D_MODEL
    pad = ROUTER_LANES - MOE_GROUPS - MOE_EXPERTS
    wr = jnp.concatenate([wg, we, jnp.zeros((D, pad), F32)], axis=1)
    br = jnp.concatenate([bg, be, jnp.zeros((pad,), F32)]).reshape(1, ROUTER_LANES)
    return nw.reshape(1, D), wr, br


def _dot_tn(a, b):
    return lax.dot_general(a, b, (((0,), (0,)), ((), ())), preferred_element_type=F32)


def _experts_kernel(t_ref, cw_ref, h_ref, w13_ref, w2_ref, fnw_ref, o_ref,
                    pt_ref, xs_ref, cws_ref, oacc_ref, meta_ref, *, final_norm):
    tm = t_ref.shape[0]
    cap = pt_ref.shape[1]
    R = MOE_SUB
    F = MOE_D_FF
    g = pl.program_id(1)
    part = pl.program_id(2)
    n_parts = MOE_EPG // MOE_EXPERTS_PER_STEP

    @pl.when((g == 0) & (part == 0))
    def _():
        cw = cw_ref[...]
        lane_t = lax.broadcasted_iota(jnp.int32, (tm, ROUTER_LANES), 1).astype(F32)
        memb = (cw[:, 0:1] == lane_t) & (lane_t < MOE_GROUPS)
        membf = jnp.where(memb, 1.0, 0.0)
        tr = lax.broadcasted_iota(jnp.int32, (tm, tm), 0)
        tc = lax.broadcasted_iota(jnp.int32, (tm, tm), 1)
        before = jnp.where(tr > tc, 1.0, 0.0).astype(BF16)
        ranks = _dot(before, membf.astype(BF16))
        cnt = jnp.sum(membf, axis=0, keepdims=True)
        nsub = jnp.floor((cnt + (R - 1.0)) * (1.0 / R))
        ur = lax.broadcasted_iota(jnp.int32, (ROUTER_LANES, ROUTER_LANES), 0)
        uc = lax.broadcasted_iota(jnp.int32, (ROUTER_LANES, ROUTER_LANES), 1)
        prefix = jnp.where(ur < uc, 1.0, 0.0).astype(BF16)
        base = _dot_x2(jnp.broadcast_to(nsub * R, (8, ROUTER_LANES)), prefix)[0:1, :]
        pos = jnp.sum(jnp.where(memb, ranks + base, 0.0), axis=-1, keepdims=True)
        col = lax.broadcasted_iota(jnp.int32, (tm, cap), 1).astype(F32)
        pt = jnp.where(pos == col, 1.0, 0.0).astype(BF16)
        pt_ref[...] = pt
        cw_hi, cw_lo = _split2(cw)
        D = t_ref.shape[1]
        gathered = _dot_tn(pt, jnp.concatenate([t_ref[...], cw_hi, cw_lo], axis=1))
        xs_ref[...] = gathered[:, 0:D].astype(BF16)
        cws_ref[...] = gathered[:, D:D + ROUTER_LANES] + gathered[:, D + ROUTER_LANES:D + 2 * ROUTER_LANES]
        oacc_ref[...] = jnp.zeros_like(oacc_ref)
        meta_ref[0:1, :] = nsub
        meta_ref[1:2, :] = base

    lane1 = lax.broadcasted_iota(jnp.int32, (1, ROUTER_LANES), 1)
    n_sub = jnp.sum(jnp.where(lane1 == g, meta_ref[0:1, :], 0.0)).astype(jnp.int32)
    row0 = jnp.sum(jnp.where(lane1 == g, meta_ref[1:2, :], 0.0)).astype(jnp.int32)
    def run_rows(start, n_rows):
        rows = pl.ds(pl.multiple_of(start, R), n_rows)
        lane_r = lax.broadcasted_iota(jnp.int32, (n_rows, ROUTER_LANES), 1)
        xs = xs_ref[rows, :]
        cws = cws_ref[rows, :]
        acc = oacc_ref[rows, :]
        for j in range(MOE_EXPERTS_PER_STEP):
            h13 = _dot(xs, w13_ref[j])
            ce = jnp.sum(jnp.where(lane_r == MOE_GROUPS + part * MOE_EXPERTS_PER_STEP + j, cws, 0.0),
                         axis=-1, keepdims=True)
            he = (_silu(h13[:, 0:F]) * h13[:, F:2 * F] * ce).astype(BF16)
            acc = acc + _dot(he, w2_ref[j])
        oacc_ref[rows, :] = acc

    big = MOE_SUBS_PER_PASS
    n_big = jnp.maximum(n_sub - MOE_MAX_PASS + big - 1, 0) // big

    def big_body(s, carry):
        run_rows(row0 + s * (big * R), big * R)
        return carry

    lax.fori_loop(0, n_big, big_body, 0)
    rem = n_sub - n_big * big
    done = row0 + n_big * (big * R)
    for k in range(1, MOE_MAX_PASS + 1):
        @pl.when(rem == k)
        def _(k=k):
            run_rows(done, k * R)

    @pl.when((g == MOE_GROUPS - 1) & (part == n_parts - 1))
    def _():
        out = h_ref[...] + _dot(pt_ref[...], oacc_ref[...].astype(BF16))
        if final_norm:
            out = _rms(out, fnw_ref[...])
        o_ref[...] = out


def _experts(t, cw, h2, w13, w2, fnw, tm, final_norm):
    T, D = h2.shape
    F = MOE_D_FF
    E = MOE_EXPERTS_PER_STEP
    n_parts = MOE_EPG // E
    cap = tm + MOE_GROUPS * MOE_SUB
    return pl.pallas_call(
        functools.partial(_experts_kernel, final_norm=final_norm),
        grid=(T // tm, MOE_GROUPS, n_parts),
        in_specs=[
            pl.BlockSpec((tm, D), lambda i, g, p: (i, 0)),
            pl.BlockSpec((tm, ROUTER_LANES), lambda i, g, p: (i, 0)),
            pl.BlockSpec((tm, D), lambda i, g, p: (i, 0)),
            pl.BlockSpec((E, D, 2 * F), lambda i, g, p: (g * n_parts + p, 0, 0)),
            pl.BlockSpec((E, F, D), lambda i, g, p: (g * n_parts + p, 0, 0)),
            pl.BlockSpec((1, D), lambda i, g, p: (0, 0)),
        ],
        out_specs=pl.BlockSpec((tm, D), lambda i, g, p: (i, 0)),
        out_shape=jax.ShapeDtypeStruct((T, D), F32),
        scratch_shapes=[
            pltpu.VMEM((tm, cap), BF16),
            pltpu.VMEM((cap, D), BF16),
            pltpu.VMEM((cap, ROUTER_LANES), F32),
            pltpu.VMEM((cap, D), F32),
            pltpu.VMEM((8, ROUTER_LANES), F32),
        ],
        compiler_params=_cparams(("parallel", "arbitrary", "arbitrary")),
    )(t, cw, h2, w13, w2, fnw)


def _moe_experts(t, cw, h2, w1, w3, w2, fnw, final_norm, tm):
    w13 = jnp.concatenate([w1, w3], axis=-1).astype(BF16)
    return _experts(t, cw, h2, w13, w2.astype(BF16), fnw.reshape(1, D_MODEL), tm, final_norm)


def _pick(n, prefs):
    for p in prefs:
        if n % p == 0:
            return p
    return n


def _ssd_retention_layer(h2, B, S, nw, w_in, conv_w, conv_b, dt_bias, a_log, d_skip, norm_w, w_out, router):
    D = D_MODEL
    T = B * S
    qk = RET_HEADS * RET_QK_HEAD
    o_x = D
    o_bc = o_x + D
    o_dt = o_bc + 2 * SSD_GROUPS * SSD_STATE
    o_q = o_dt + SSD_HEADS
    o_k = o_q + qk
    o_v = o_k + qk
    o_g = o_v + D
    w_main = jnp.concatenate([w_in[:, 0:o_x], w_in[:, o_x:o_bc], w_in[:, o_v:o_g], w_in[:, o_g:o_g + D],
                              w_in[:, o_bc:o_dt], w_in[:, o_q:o_k], w_in[:, o_k:o_v]], axis=1).astype(BF16)
    w_dt = jnp.pad(w_in[:, o_dt:o_q], ((0, 0), (0, LANES - SSD_HEADS))).astype(BF16)
    tm = _pick(T, (1024, 512, 256, 128))
    proj, dt = _inproj(h2, nw.reshape(1, D), w_main, w_dt, _pick(T, (2048, 1024, 512, 256, 128)), INPROJ_TN)
    proj3 = proj.reshape(B, S, proj.shape[1])
    dt3 = dt.reshape(B, S, LANES)

    head_of_channel = jnp.arange(D) // SSD_HEAD_DIM
    expand = (jnp.arange(LANES)[:, None] == head_of_channel[None, :]).astype(BF16)
    pad16 = lambda x: jnp.pad(x, (0, LANES - SSD_HEADS)).reshape(1, LANES)
    ssd_params = (conv_w[:, :D], conv_b[:D].reshape(1, D), conv_w[:, D:], conv_b[D:].reshape(1, PROJ_NARROW),
                  pad16(dt_bias), pad16(a_log), expand, jnp.repeat(d_skip, SSD_HEAD_DIM).reshape(1, D),
                  norm_w.reshape(1, D))

    C = RET_CHUNK
    half = RET_QK_HEAD // 2
    pos = jnp.arange(S, dtype=F32)
    inv_freq = ROPE_BASE ** (-jnp.arange(half, dtype=F32) / half)
    ang = pos[:, None] * inv_freq[None, :]
    cc = jnp.concatenate([jnp.cos(ang), jnp.cos(ang)], axis=1)
    ss = jnp.concatenate([-jnp.sin(ang), jnp.sin(ang)], axis=1)
    log_gamma = jnp.log(1.0 - 2.0 ** (-5.0 - jnp.arange(RET_HEADS, dtype=F32)))
    idx = jnp.arange(C, dtype=F32)
    diff = idx[:, None] - idx[None, :]
    dmask = jnp.where(diff[None] >= 0, jnp.exp(jnp.maximum(diff, 0.0)[None] * log_gamma[:, None, None]), 0.0)
    qdec = jnp.repeat(jnp.exp((idx[:, None] + 1.0) * log_gamma[None, :]), RET_V_HEAD, axis=1)
    kdec = jnp.repeat(jnp.exp((C - 1.0 - idx[:, None]) * log_gamma[None, :]), RET_QK_HEAD, axis=1)
    cdec = jnp.repeat(jnp.exp(C * log_gamma), RET_V_HEAD).reshape(1, D)
    y = _mix0(proj3, dt3, ssd_params, (cc, ss, dmask, qdec, kdec, cdec))
    return _outproj(y.reshape(T, 2 * D), w_out.astype(BF16), h2, router, tm)


def _rwkv_layer(h2, B, S, nw, mu, w_r, w_k, w_v, w_o, w0, w1, w2, a0, a1, a2, g1, g2, k_k, k_a, r_k, lnx_w, lnx_b,
                router):
    D = D_MODEL
    T = B * S
    padc = lambda w, n: jnp.pad(w, ((0, 0), (0, n - w.shape[1]))).astype(BF16)
    padr = lambda w, n: jnp.pad(w, ((0, n - w.shape[0]), (0, 0))).astype(BF16)
    head_of_channel = jnp.arange(D) // RWKV_HEAD
    hs = (head_of_channel[:, None] == jnp.arange(LANES)[None, :]).astype(BF16)
    row = lambda x: x.reshape(1, D)
    tm = _pick(S, (512, 256, 128))
    r, k, v, kk, a, lw, g = _rwkv_pre(
        h2, row(nw), mu, w_r.astype(BF16), w_k.astype(BF16), w_v.astype(BF16),
        padc(w1, RWKV_LORA_PAD), padr(w2, RWKV_LORA_PAD), row(w0),
        padc(a1, RWKV_LORA_PAD), padr(a2, RWKV_LORA_PAD), row(a0),
        padc(g1, RWKV_GATE_PAD), padr(g2, RWKV_GATE_PAD), row(k_k), row(k_a), hs, hs.T, S, tm)
    sh = lambda x: x.reshape(B, S, D)
    tb = _pick(S, (512, 256, 128, 64))
    y = _rwkv_scan(sh(r), sh(k), sh(v), sh(kk), sh(a), sh(lw), sh(g),
                   row(lnx_w), row(lnx_b), r_k.reshape(1, D), tb, RWKV_HEADS // RWKV_QUAD,
                   _pick(B, (RWKV_SEQS_PER_STEP, 1)))
    return _rwkv_out(y.reshape(T, D), w_o.astype(BF16), h2, router, _pick(T, (1024, 512, 256, 128)))


def kernel(x, norm_mix_w, norm_ffn_w, norm_final_w, w_in_e, ssd_conv_w, ssd_conv_b, ssd_dt_bias, ssd_a_log, ssd_d, ssd_norm_w, w_out_e, rw_mu, rw_wr, rw_wk, rw_wv, rw_wo, rw_w0, rw_w1, rw_w2, rw_a0, rw_a1, rw_a2, rw_g1, rw_g2, rw_kk, rw_ka, rw_rk, rw_lnx_w, rw_lnx_b, moe_wg, moe_bg, moe_we, moe_be, moe_w1, moe_w3, moe_w2):
    B, S, D = x.shape
    T = B * S
    depth = norm_mix_w.shape[0]
    h = x.reshape(T, D)
    tm_moe = _pick(T, (1024, 512, 256, 128))
    for layer in range(depth):
        i = layer // 2
        router = _router_params(norm_ffn_w[layer], moe_wg[layer], moe_bg[layer], moe_we[layer], moe_be[layer])
        if layer % 2 == 0:
            h, t, cw = _ssd_retention_layer(h, B, S, norm_mix_w[layer], w_in_e[i], ssd_conv_w[i], ssd_conv_b[i],
                                            ssd_dt_bias[i], ssd_a_log[i], ssd_d[i], ssd_norm_w[i], w_out_e[i],
                                            router)
        else:
            h, t, cw = _rwkv_layer(h, B, S, norm_mix_w[layer], rw_mu[i], rw_wr[i], rw_wk[i], rw_wv[i], rw_wo[i],
                                   rw_w0[i], rw_w1[i], rw_w2[i], rw_a0[i], rw_a1[i], rw_a2[i], rw_g1[i],
                                   rw_g2[i], rw_kk[i], rw_ka[i], rw_rk[i], rw_lnx_w[i], rw_lnx_b[i], router)
        h = _moe_experts(t, cw, h, moe_w1[layer], moe_w3[layer], moe_w2[layer], norm_final_w,
                         final_norm=(layer == depth - 1), tm=tm_moe)
    return h.reshape(B, S, D)
```

```python
import functools
import math

import jax
import jax.numpy as jnp
from jax import lax
from jax.experimental import pallas as pl
from jax.experimental.pallas import tpu as pltpu

F32 = jnp.float32
BF16 = jnp.bfloat16

D_MODEL = 1024
RMS_EPS = 1e-6
SSD_HEADS = 16
SSD_HEAD_DIM = 64
SSD_GROUPS = 2
SSD_STATE = 128
SSD_CONV = 4
SSD_CHUNK = 128
MIX0_CHUNKS_PER_STEP = 4
SSD_CONV_TAIL = 16
SSD_NORM_EPS = 1e-5
SSD_GROUP_WIDTH = D_MODEL // SSD_GROUPS
RET_HEADS = 4
RET_QK_HEAD = 128
RET_V_HEAD = 256
RET_CHUNK = 128
ROPE_BASE = 10000.0
RWKV_HEAD = 64
RWKV_HEADS = 16
RWKV_LN_EPS = 64e-5
RWKV_CHUNK = 64
RWKV_QUAD = 4
RWKV_SEQS_PER_STEP = 2
RWKV_LORA_PAD = 128
RWKV_GATE_PAD = 256
RWKV_PRE_SPLIT = 2
MOE_GROUPS = 4
MOE_EPG = 4
MOE_EXPERTS = 16
MOE_D_FF = 512
ROUTER_LANES = 128
MOE_SUB = 64
MOE_SUBS_PER_PASS = 4
MOE_MAX_PASS = 6
MOE_EXPERTS_PER_STEP = 4

LANES = 128
SUBLANES = 8
VMEM_LIMIT_BYTES = 56 * 1024 * 1024
MOE_VMEM_LIMIT_BYTES = 60 * 1024 * 1024

PROJ_NARROW = 512
PROJ_BLOCK_Z, PROJ_BLOCK_X, PROJ_BLOCK_V, PROJ_BLOCK_G = 0, 1, 2, 3
PROJ_BLOCK_BC, PROJ_BLOCK_Q, PROJ_BLOCK_K = 8, 9, 10
PROJ_WIDTH = 4 * D_MODEL + 3 * PROJ_NARROW
INPROJ_TN = PROJ_WIDTH // 4


def _cparams(sem, vmem_limit_bytes=VMEM_LIMIT_BYTES):
    return pltpu.CompilerParams(dimension_semantics=sem, vmem_limit_bytes=vmem_limit_bytes)


def _dot(a, b):
    return jnp.dot(a, b, preferred_element_type=F32)


def _dot_nt(a, b):
    return lax.dot_general(a, b, (((1,), (1,)), ((), ())), preferred_element_type=F32)


def _split2(x):
    hi = x.astype(BF16)
    lo = (x - hi.astype(F32)).astype(BF16)
    return hi, lo


def _dot_x2(x, w_exact):
    hi, lo = _split2(x)
    return _dot(hi, w_exact) + _dot(lo, w_exact)


def _dot_2x(w_exact, x):
    hi, lo = _split2(x)
    return _dot(w_exact, hi) + _dot(w_exact, lo)


def _dot_x3(x, w):
    xh, xl = _split2(x)
    wh, wl = _split2(w)
    return _dot(xh, wh) + _dot(xl, wh) + _dot(xh, wl)


def _sigmoid(x):
    return 0.5 * jnp.tanh(0.5 * x) + 0.5


def _silu(x):
    hx = 0.5 * x
    return hx * jnp.tanh(hx) + hx


def _softplus(x):
    return jnp.maximum(x, 0.0) + jnp.log(1.0 + jnp.exp(-jnp.abs(x)))


def _rms(x, w, eps=RMS_EPS):
    return x * lax.rsqrt(jnp.mean(x * x, axis=-1, keepdims=True) + eps) * w


def _inproj_kernel(x_ref, nw_ref, w_ref, wdt_ref, o_ref, dt_ref, u_ref):
    @pl.when(pl.program_id(1) == 0)
    def _():
        ub = _rms(x_ref[...], nw_ref[...]).astype(BF16)
        u_ref[...] = ub
        dt_ref[...] = _dot(ub, wdt_ref[...])

    o_ref[...] = _dot(u_ref[...], w_ref[...]).astype(o_ref.dtype)


def _inproj(h2, nw, w_main, w_dt, tm, tn):
    T, D = h2.shape
    N = w_main.shape[1]
    return pl.pallas_call(
        _inproj_kernel,
        grid=(T // tm, N // tn),
        in_specs=[
            pl.BlockSpec((tm, D), lambda i, j: (i, 0)),
            pl.BlockSpec((1, D), lambda i, j: (0, 0)),
            pl.BlockSpec((D, tn), lambda i, j: (0, j)),
            pl.BlockSpec((D, LANES), lambda i, j: (0, 0)),
        ],
        out_specs=[
            pl.BlockSpec((tm, tn), lambda i, j: (i, j)),
            pl.BlockSpec((tm, LANES), lambda i, j: (i, 0)),
        ],
        out_shape=[
            jax.ShapeDtypeStruct((T, N), BF16),
            jax.ShapeDtypeStruct((T, LANES), F32),
        ],
        scratch_shapes=[pltpu.VMEM((tm, D), BF16)],
        compiler_params=_cparams(("parallel", "arbitrary")),
    )(h2, nw, w_main, w_dt)


def _ssd_chunk(z_ref, x_ref, bc_ref, dt_ref, cwx_ref, cbx_ref, cwb_ref, cbb_ref,
               dtb_ref, alog_ref, e_ref, dskip_ref, nw_ref, o_ref,
               state_ref, xtail, bctail, ybuf, *, rows, lane0):
    L = SSD_CHUNK
    TAIL = xtail.shape[0]
    srow = lax.broadcasted_iota(jnp.int32, (SSD_CONV * L, TAIL + L), 0)
    scol = lax.broadcasted_iota(jnp.int32, (SSD_CONV * L, TAIL + L), 1)
    shift = jnp.where(scol == (srow % L) + (srow // L) + (TAIL - SSD_CONV + 1), 1.0, 0.0).astype(BF16)

    def conv(cur_ref, tail, w_ref, b_ref):
        cur = cur_ref[0, rows, :]
        taps = _dot(shift, jnp.concatenate([tail[...], cur], axis=0))
        acc = b_ref[...] + w_ref[0:1, :] * taps[0:L, :]
        for k in range(1, SSD_CONV):
            acc = acc + w_ref[k:k + 1, :] * taps[k * L:(k + 1) * L, :]
        tail[...] = cur[L - TAIL:L, :]
        return _silu(acc)

    xs = conv(x_ref, xtail, cwx_ref, cbx_ref)
    bc = conv(bc_ref, bctail, cwb_ref, cbb_ref)

    dt = _softplus(dt_ref[0, rows, :] + dtb_ref[...])
    a = dt * (-jnp.exp(alog_ref[...]))
    row = lax.broadcasted_iota(jnp.int32, (L, L), 0)
    col = lax.broadcasted_iota(jnp.int32, (L, L), 1)
    causal = row >= col
    tri = jnp.where(causal, 1.0, 0.0).astype(BF16)
    acum = _dot_2x(tri, a)
    acum_t = acum.T
    e = e_ref[...]
    dt_full = _dot_x2(dt, e)
    acum_full = _dot_x2(acum, e)
    xdt = xs * dt_full
    alast_full = acum_full[L - 1:L, :]
    decay_in = jnp.exp(acum_full)
    xdt_end = (xdt * jnp.exp(alast_full - acum_full)).astype(BF16)
    xdt_b = xdt.astype(BF16)
    lane = lax.broadcasted_iota(jnp.int32, (L, LANES), 1)
    first_head = lane < SSD_HEAD_DIM

    GW = SSD_GROUP_WIDTH
    for g in range(SSD_GROUPS):
        bg = bc[:, g * SSD_STATE:(g + 1) * SSD_STATE]
        cg = bc[:, (SSD_GROUPS + g) * SSD_STATE:(SSD_GROUPS + g + 1) * SSD_STATE].astype(BF16)
        cb = _dot_nt(cg, bg.astype(BF16))
        st = state_ref[g]
        y_off = _dot(cg, st.astype(BF16)) * decay_in[:, g * GW:(g + 1) * GW]
        for p in range(GW // LANES):
            xp = xdt_b[:, g * GW + p * LANES:g * GW + (p + 1) * LANES]
            ys = []
            for s in range(2):
                hd = g * (SSD_HEADS // SSD_GROUPS) + 2 * p + s
                seg = acum[:, hd:hd + 1] - acum_t[hd:hd + 1, :]
                dec = jnp.where(causal, jnp.exp(seg), 0.0)
                ys.append(_dot((cb * dec).astype(BF16), xp))
            yd = jnp.where(first_head, ys[0], ys[1])
            lo = g * GW + p * LANES
            ybuf[:, lo:lo + LANES] = yd + y_off[:, p * LANES:(p + 1) * LANES]
        bg_t = bg.T.astype(BF16)
        state_ref[g] = st * jnp.exp(alast_full[:, g * GW:(g + 1) * GW]) + _dot(
            bg_t, xdt_end[:, g * GW:(g + 1) * GW])

    y = ybuf[...] + xs * dskip_ref[...]
    y = y * _silu(z_ref[0, rows, :].astype(F32))
    for g in range(SSD_GROUPS):
        yg = y[:, g * GW:(g + 1) * GW]
        yg = yg * lax.rsqrt(jnp.mean(yg * yg, axis=-1, keepdims=True) + SSD_NORM_EPS)
        lo = lane0 + g * GW
        o_ref[0, rows, lo:lo + GW] = (yg * nw_ref[:, g * GW:(g + 1) * GW]).astype(o_ref.dtype)


def _ret_chunk(q_ref, k_ref, v_ref, g_ref, cc_ref, ss_ref, dmask_ref, qdec_ref, kdec_ref,
               cdec_ref, o_ref, r_ref, *, rows, lane0):
    cc = cc_ref[rows, :]
    ss = ss_ref[rows, :]
    dk, dv = RET_QK_HEAD, RET_V_HEAD
    half = dk // 2

    def rope(x):
        return x * cc + pltpu.roll(x, half, 1) * ss

    heads = range(RET_HEADS)
    qs = [rope(q_ref[0, rows, hd * dk:(hd + 1) * dk].astype(F32)) for hd in heads]
    ks = [rope(k_ref[0, rows, hd * dk:(hd + 1) * dk].astype(F32)) * (dk ** -0.5) for hd in heads]
    vs = [v_ref[0, rows, hd * dv:(hd + 1) * dv] for hd in heads]
    qbs = [q.astype(BF16) for q in qs]
    ss = [_dot_nt(qbs[hd], ks[hd].astype(BF16)) * dmask_ref[hd] for hd in heads]
    r_olds = [r_ref[hd] for hd in heads]
    cross = [_dot(qbs[hd], r_olds[hd].astype(BF16)) for hd in heads]
    kd_ts = [(ks[hd] * kdec_ref[:, hd * dk:(hd + 1) * dk]).T.astype(BF16) for hd in heads]
    upds = [_dot(kd_ts[hd], vs[hd]) for hd in heads]
    inner = [_dot(ss[hd].astype(BF16), vs[hd]) for hd in heads]
    for hd in heads:
        r_ref[hd] = r_olds[hd] * cdec_ref[:, hd * dv:(hd + 1) * dv] + upds[hd]
        y = inner[hd] + cross[hd] * qdec_ref[:, hd * dv:(hd + 1) * dv]
        y = y * lax.rsqrt(jnp.mean(y * y, axis=-1, keepdims=True) + RMS_EPS)
        y = y * _silu(g_ref[0, rows, hd * dv:(hd + 1) * dv].astype(F32))
        lo = lane0 + hd * dv
        o_ref[0, rows, lo:lo + dv] = y.astype(o_ref.dtype)


N_SSD_IN = 13
N_RET_IN = 10


def _mix0_kernel(*refs):
    ssd_in = refs[:N_SSD_IN]
    ret_in = refs[N_SSD_IN:N_SSD_IN + N_RET_IN]
    o_ref, state_ref, xtail, bctail, ybuf, r_ref = refs[N_SSD_IN + N_RET_IN:]

    @pl.when(pl.program_id(1) == 0)
    def _():
        state_ref[...] = jnp.zeros_like(state_ref)
        xtail[...] = jnp.zeros_like(xtail)
        bctail[...] = jnp.zeros_like(bctail)
        r_ref[...] = jnp.zeros_like(r_ref)

    for ci in range(ybuf.shape[0]):
        rows = slice(ci * SSD_CHUNK, (ci + 1) * SSD_CHUNK)
        _ssd_chunk(*ssd_in, o_ref, state_ref, xtail, bctail, ybuf.at[ci], rows=rows, lane0=0)
        _ret_chunk(*ret_in, o_ref, r_ref, rows=rows, lane0=D_MODEL)


def _mix0(proj3, dt3, ssd_params, ret_params):
    B, S, _ = proj3.shape
    n_chunks = _pick(S // SSD_CHUNK, (MIX0_CHUNKS_PER_STEP, 1))
    L = n_chunks * SSD_CHUNK
    D = D_MODEL
    assert RET_CHUNK == SSD_CHUNK and len(ssd_params) + 4 == N_SSD_IN and len(ret_params) + 4 == N_RET_IN
    full = lambda arr: pl.BlockSpec(arr.shape, lambda b, c: (0,) * arr.ndim)
    cc, ss = ret_params[:2]
    ret_specs = [pl.BlockSpec((L, RET_QK_HEAD), lambda b, c: (c, 0))] * 2 + [full(p) for p in ret_params[2:]]
    return pl.pallas_call(
        _mix0_kernel,
        grid=(B, S // L),
        in_specs=[
            pl.BlockSpec((1, L, D), lambda b, c: (b, c, PROJ_BLOCK_Z)),
            pl.BlockSpec((1, L, D), lambda b, c: (b, c, PROJ_BLOCK_X)),
            pl.BlockSpec((1, L, PROJ_NARROW), lambda b, c: (b, c, PROJ_BLOCK_BC)),
            pl.BlockSpec((1, L, LANES), lambda b, c: (b, c, 0)),
        ] + [full(p) for p in ssd_params] + [
            pl.BlockSpec((1, L, PROJ_NARROW), lambda b, c: (b, c, PROJ_BLOCK_Q)),
            pl.BlockSpec((1, L, PROJ_NARROW), lambda b, c: (b, c, PROJ_BLOCK_K)),
            pl.BlockSpec((1, L, D), lambda b, c: (b, c, PROJ_BLOCK_V)),
            pl.BlockSpec((1, L, D), lambda b, c: (b, c, PROJ_BLOCK_G)),
        ] + ret_specs,
        out_specs=pl.BlockSpec((1, L, 2 * D), lambda b, c: (b, c, 0)),
        out_shape=jax.ShapeDtypeStruct((B, S, 2 * D), BF16),
        scratch_shapes=[
            pltpu.VMEM((SSD_GROUPS, SSD_STATE, SSD_GROUP_WIDTH), F32),
            pltpu.VMEM((SSD_CONV_TAIL, D), BF16),
            pltpu.VMEM((SSD_CONV_TAIL, PROJ_NARROW), BF16),
            pltpu.VMEM((n_chunks, SSD_CHUNK, D), F32),
            pltpu.VMEM((RET_HEADS, RET_QK_HEAD, RET_V_HEAD), F32),
        ],
        compiler_params=_cparams(("parallel", "arbitrary")),
    )(proj3, proj3, proj3, dt3, *ssd_params, proj3, proj3, proj3, proj3, *ret_params)


def _outproj_kernel(y_ref, w_ref, h_ref, nw_ref, wr_ref, br_ref, o_ref, t_ref, cw_ref):
    h = h_ref[...] + _dot(y_ref[...], w_ref[...])
    o_ref[...] = h
    _route_rows(h, nw_ref, wr_ref, br_ref, t_ref, cw_ref)


def _outproj(y, w, h2, router, tm):
    T, D = h2.shape
    K = y.shape[1]
    r_in, r_out, r_shapes = _router_specs(tm, D)
    return pl.pallas_call(
        _outproj_kernel,
        grid=(T // tm,),
        in_specs=[
            pl.BlockSpec((tm, K), lambda i: (i, 0)),
            pl.BlockSpec((K, D), lambda i: (0, 0)),
            pl.BlockSpec((tm, D), lambda i: (i, 0)),
        ] + r_in,
        out_specs=[pl.BlockSpec((tm, D), lambda i: (i, 0))] + r_out,
        out_shape=[jax.ShapeDtypeStruct((T, D), F32)] + r_shapes(T),
        compiler_params=_cparams(("parallel",)),
    )(y, w, h2, *router)


def _rwkv_pre_kernel(h_ref, hp_ref, nw_ref, mu_ref, wr_ref, wk_ref, wv_ref,
                     w1_ref, w2_ref, w0_ref, a1_ref, a2_ref, a0_ref, g1_ref, g2_ref,
                     kkw_ref, kaw_ref, hs_ref, hst_ref,
                     r_o, k_o, v_o, kk_o, a_o, lw_o, g_o, ubuf, xxbuf, *, tiles_per_seq):
    tm = h_ref.shape[0]
    i = pl.program_id(0)
    nw = nw_ref[...]
    u = _rms(h_ref[...], nw)
    up = _rms(hp_ref[...], nw)
    seq_start = (i % tiles_per_seq) == 0
    P = SUBLANES
    ubuf[P:P + tm, :] = u
    ubuf[0:P, :] = jnp.where(seq_start, 0.0, up)
    xxbuf[...] = ubuf[P - 1:P - 1 + tm, :] - u

    def mix(j, lo, n):
        return (ubuf[P + lo:P + lo + n, :] + xxbuf[lo:lo + n, :] * mu_ref[j:j + 1, :]).astype(BF16)

    n = tm // RWKV_PRE_SPLIT
    parts = [s * n for s in range(RWKV_PRE_SPLIT)]
    first = []
    for lo in parts:
        first.append(dict(
            r=_dot(mix(0, lo, n), wr_ref[...]), k=_dot(mix(2, lo, n), wk_ref[...]),
            v=_dot(mix(3, lo, n), wv_ref[...]), w1=_dot(mix(1, lo, n), w1_ref[...]),
            a1=_dot(mix(4, lo, n), a1_ref[...]), g1=_dot(mix(5, lo, n), g1_ref[...])))
    second = []
    for f in first:
        kk = f["k"] * kkw_ref[...]
        second.append(dict(
            wl=w0_ref[...] + _dot(jnp.tanh(f["w1"]).astype(BF16), w2_ref[...]),
            al=a0_ref[...] + _dot(f["a1"].astype(BF16), a2_ref[...]),
            g=_dot(_sigmoid(f["g1"]).astype(BF16), g2_ref[...]),
            kk=kk, ssq=_dot((kk * kk).astype(BF16), hs_ref[...])))
    for lo, f, s in zip(parts, first, second):
        rows = slice(lo, lo + n)
        w_log = -_softplus(-s["wl"]) - 0.5
        a = _sigmoid(s["al"])
        inv = lax.rsqrt(jnp.maximum(s["ssq"], 1e-24))
        r_o[rows, :] = f["r"].astype(r_o.dtype)
        k_o[rows, :] = (f["k"] * (1.0 + (a - 1.0) * kaw_ref[...])).astype(k_o.dtype)
        v_o[rows, :] = f["v"].astype(v_o.dtype)
        kk_o[rows, :] = (s["kk"] * _dot_x2(inv, hst_ref[...])).astype(kk_o.dtype)
        a_o[rows, :] = a.astype(a_o.dtype)
        lw_o[rows, :] = -jnp.exp(w_log)
        g_o[rows, :] = s["g"].astype(g_o.dtype)


def _rwkv_pre(h2, nw, mu, wr, wk, wv, w1, w2, w0, a1, a2, a0, g1, g2, kkw, kaw, hs, hst, S, tm):
    T, D = h2.shape
    full = lambda arr: pl.BlockSpec(arr.shape, lambda i: (0,) * arr.ndim)
    row = pl.BlockSpec((tm, D), lambda i: (i, 0))
    params = (nw, mu, wr, wk, wv, w1, w2, w0, a1, a2, a0, g1, g2, kkw, kaw, hs, hst)
    bf = jax.ShapeDtypeStruct((T, D), BF16)
    return pl.pallas_call(
        functools.partial(_rwkv_pre_kernel, tiles_per_seq=S // tm),
        grid=(T // tm,),
        in_specs=[row, pl.BlockSpec((SUBLANES, D), lambda i: (jnp.maximum(i * (tm // SUBLANES) - 1, 0), 0))]
        + [full(p) for p in params],
        out_specs=[row] * 7,
        out_shape=[bf, bf, bf, bf, bf, jax.ShapeDtypeStruct((T, D), F32), bf],
        scratch_shapes=[pltpu.VMEM((tm + SUBLANES, D), F32), pltpu.VMEM((tm, D), F32)],
        compiler_params=_cparams(("parallel",)),
    )(h2, h2, *params)


def _rwkv_scan4_kernel(r_ref, k_ref, v_ref, kk_ref, a_ref, lw_ref, g_ref, lnw_ref, lnb_ref, rk_ref,
                       o_ref, state_ref, y_ref, *, n_chunks, n_quads):
    C = RWKV_CHUNK
    N = RWKV_HEAD
    HQ = RWKV_QUAD
    QW = HQ * N
    t = pl.program_id(2)

    @pl.when(t == 0)
    def _():
        state_ref[...] = jnp.zeros_like(state_ref)

    row = lax.broadcasted_iota(jnp.int32, (C, C), 0)
    col = lax.broadcasted_iota(jnp.int32, (C, C), 1)
    tri_incl = jnp.where(row >= col, 1.0, 0.0).astype(BF16)
    prow = lax.broadcasted_iota(jnp.int32, (C, QW), 0)
    plane = lax.broadcasted_iota(jnp.int32, (C, QW), 1)
    eye_p = jnp.where(prow == plane % C, 1.0, 0.0).astype(F32)
    head_masks = [plane // N == h for h in range(HQ)]
    grow = lax.broadcasted_iota(jnp.int32, (2 * C, 2 * QW), 0)
    gcol = lax.broadcasted_iota(jnp.int32, (2 * C, 2 * QW), 1)
    gmask = (grow % C + grow // C) > (gcol % C)
    srow = lax.broadcasted_iota(jnp.int32, (QW, QW), 0)
    scol = lax.broadcasted_iota(jnp.int32, (QW, QW), 1)
    state_blocks = (srow // N) == (scol // N)
    n_levels = int(math.log2(C))
    n_batch = r_ref.shape[0]
    chains = [(bi, q) for bi in range(n_batch) for q in range(n_quads)]

    head_ones = [jnp.where(m, 1.0, 0.0).astype(BF16) for m in head_masks]

    def bdiag(x):
        xb = x.astype(BF16)
        return jnp.concatenate([xb * one for one in head_ones], axis=0)

    first_of_pair = lax.broadcasted_iota(jnp.int32, (C, LANES), 1) < N

    def head_sum(x):
        halves = []
        for lo in range(0, QW, LANES):
            xh = x[:, lo:lo + LANES]
            total = jnp.sum(xh, axis=-1, keepdims=True)
            first = jnp.sum(jnp.where(first_of_pair, xh, 0.0), axis=-1, keepdims=True)
            halves.append(jnp.where(first_of_pair, first, total - first))
        return jnp.concatenate(halves, axis=1)

    def chunk_body(ci, carry):
        rows = pl.ds(pl.multiple_of(ci * C, C), C)
        cum_all = [_dot_2x(tri_incl, lw_ref[bi, rows, :]) for bi in range(n_batch)]
        pre = []
        for bi, q in chains:
            lanes = slice(q * QW, (q + 1) * QW)
            r = r_ref[bi, rows, lanes].astype(F32)
            k = k_ref[bi, rows, lanes].astype(F32)
            v = v_ref[bi, rows, lanes].astype(F32)
            kk = kk_ref[bi, rows, lanes].astype(F32)
            a = a_ref[bi, rows, lanes].astype(F32)
            lw = lw_ref[bi, rows, lanes]
            cum = cum_all[bi][:, lanes]
            wc = cum[C - 1:C, :]
            e_out = jnp.exp(-cum)
            e_end = jnp.exp(wc - cum)
            b = kk * a
            at = -kk * jnp.exp(cum - lw)
            rt = r * jnp.exp(cum)
            pre.append(dict(lanes=lanes, r=r, k=k, v=v, wc=wc, at=at, rt=rt,
                            bh=b * e_out, kh=k * e_out,
                            kb_end=jnp.concatenate([k * e_end, b * e_end], axis=0).astype(BF16)))

        grams = []
        for d in pre:
            lhs = jnp.concatenate([d["at"], d["rt"]], axis=0).astype(BF16)
            rhs = jnp.concatenate([bdiag(d["bh"]), bdiag(d["kh"])], axis=0)
            grams.append(jnp.where(gmask, _dot_nt(lhs, rhs), 0.0))
        lps = [g[0:C, 0:QW] for g in grams]
        bd_vs = [bdiag(d["v"]) for d in pre]
        mvs = [_dot(g[0:C, QW:2 * QW].astype(BF16), bd_v) for g, bd_v in zip(grams, bd_vs)]
        pks = [eye_p + lp for lp in lps]
        lks = [_dot(lp.astype(BF16), bdiag(lp)) for lp in lps]
        for lvl in range(1, n_levels):
            if lvl < n_levels - 1:
                boths = [_dot(jnp.concatenate([pk, lk], axis=0).astype(BF16), bdiag(lk))
                         for pk, lk in zip(pks, lks)]
                pks = [pk + bo[0:C, :] for pk, bo in zip(pks, boths)]
                lks = [bo[C:2 * C, :] for bo in boths]
            else:
                pks = [pk + _dot(pk.astype(BF16), bdiag(lk)) for pk, lk in zip(pks, lks)]
        tws = [_dot(pk.astype(BF16), jnp.concatenate([bdiag(mv), bdiag(d["at"])], axis=1))
               for pk, mv, d in zip(pks, mvs, pre)]

        sts = [state_ref[bi, q] for bi, q in chains]
        zs = [_dot_nt(jnp.concatenate([tw[:, QW:2 * QW], d["rt"]], axis=0).astype(BF16), st.astype(BF16))
              for tw, d, st in zip(tws, pre, sts)]
        us = [tw[:, 0:QW] + z[0:C, :] for tw, z in zip(tws, zs)]
        yss = [_dot(g[C:2 * C, :].astype(BF16), jnp.concatenate([bdiag(u), bd_v], axis=0))
               for g, u, bd_v in zip(grams, us, bd_vs)]
        upds = [_dot(jnp.concatenate([d["v"], u], axis=0).T.astype(BF16), d["kb_end"])
                for d, u in zip(pre, us)]
        for i, (bi, q) in enumerate(chains):
            d = pre[i]
            state_ref[bi, q] = sts[i] * jnp.exp(d["wc"]) + jnp.where(state_blocks, upds[i], 0.0)
            y_ref[bi, :, d["lanes"]] = zs[i][C:2 * C, :] + yss[i]
        return carry

    def finish(ci):
        rows = pl.ds(pl.multiple_of(ci * C, C), C)
        for bi, q in chains:
            lanes = slice(q * QW, (q + 1) * QW)
            y = y_ref[bi, :, lanes]
            mean = head_sum(y) * (1.0 / N)
            yc = y - mean
            var = head_sum(yc * yc) * (1.0 / N)
            yn = yc * lax.rsqrt(var + RWKV_LN_EPS) * lnw_ref[:, lanes] + lnb_ref[:, lanes]
            r = r_ref[bi, rows, lanes].astype(F32)
            k = k_ref[bi, rows, lanes].astype(F32)
            v = v_ref[bi, rows, lanes].astype(F32)
            bonus = head_sum(r * k * rk_ref[:, lanes]) * v
            out = (yn + bonus) * g_ref[bi, rows, lanes].astype(F32)
            o_ref[bi, rows, lanes] = out.astype(o_ref.dtype)

    y_ref[...] = jnp.zeros_like(y_ref)

    def loop_body(ci, carry):
        finish(jnp.maximum(ci - 1, 0))
        return chunk_body(ci, carry)

    lax.fori_loop(0, n_chunks, loop_body, 0)
    finish(n_chunks - 1)


def _rwkv_scan(r, k, v, kk, a, lw, g, lnw, lnb, rk, tb, n_quads, nb):
    B, S, D = r.shape
    QW = RWKV_QUAD * RWKV_HEAD
    W = n_quads * QW
    blk = pl.BlockSpec((nb, tb, W), lambda b, j, t: (b, t, j))
    par = pl.BlockSpec((1, W), lambda b, j, t: (0, j))
    return pl.pallas_call(
        functools.partial(_rwkv_scan4_kernel, n_chunks=tb // RWKV_CHUNK, n_quads=n_quads),
        grid=(B // nb, D // W, S // tb),
        in_specs=[blk] * 7 + [par] * 3,
        out_specs=blk,
        out_shape=jax.ShapeDtypeStruct((B, S, D), BF16),
        scratch_shapes=[pltpu.VMEM((nb, n_quads, QW, QW), F32), pltpu.VMEM((nb, RWKV_CHUNK, W), F32)],
        compiler_params=_cparams(("parallel", "parallel", "arbitrary")),
    )(r, k, v, kk, a, lw, g, lnw, lnb, rk)


def _rwkv_out_kernel(y_ref, w_ref, h_ref, nw_ref, wr_ref, br_ref, o_ref, t_ref, cw_ref):
    h = h_ref[...] + _dot(y_ref[...], w_ref[...])
    o_ref[...] = h
    _route_rows(h, nw_ref, wr_ref, br_ref, t_ref, cw_ref)


def _rwkv_out(y, wo, h2, router, tm):
    T, D = h2.shape
    r_in, r_out, r_shapes = _router_specs(tm, D)
    return pl.pallas_call(
        _rwkv_out_kernel,
        grid=(T // tm,),
        in_specs=[pl.BlockSpec((tm, D), lambda i: (i, 0)), pl.BlockSpec((D, D), lambda i: (0, 0)),
                  pl.BlockSpec((tm, D), lambda i: (i, 0))] + r_in,
        out_specs=[pl.BlockSpec((tm, D), lambda i: (i, 0))] + r_out,
        out_shape=[jax.ShapeDtypeStruct((T, D), F32)] + r_shapes(T),
        compiler_params=_cparams(("parallel",)),
    )(y, wo, h2, *router)


def _route_rows(h, nw_ref, wr_ref, br_ref, t_ref, cw_ref):
    tn = _rms(h, nw_ref[...])
    t_ref[...] = tn.astype(t_ref.dtype)
    logits = _dot_x3(tn, wr_ref[...]) + br_ref[...]
    lane = lax.broadcasted_iota(jnp.int32, logits.shape, 1).astype(F32)
    neg = jnp.float32(-jnp.inf)
    big = jnp.float32(1 << 20)
    is_g = lane < MOE_GROUPS
    gl = jnp.where(is_g, logits, neg)
    gmax = jnp.max(gl, axis=-1, keepdims=True)
    g_idx = jnp.min(jnp.where(gl == gmax, lane, big), axis=-1, keepdims=True)
    g_gate = 1.0 / jnp.sum(jnp.where(is_g, jnp.exp(logits - gmax), 0.0), axis=-1, keepdims=True)
    lo = MOE_GROUPS + g_idx * MOE_EPG
    sel = (lane >= lo) & (lane < lo + MOE_EPG)
    el = jnp.where(sel, logits, neg)
    m1 = jnp.max(el, axis=-1, keepdims=True)
    i1 = jnp.min(jnp.where(el == m1, lane, big), axis=-1, keepdims=True)
    el2 = jnp.where(lane == i1, neg, el)
    m2 = jnp.max(el2, axis=-1, keepdims=True)
    i2 = jnp.min(jnp.where(el2 == m2, lane, big), axis=-1, keepdims=True)
    e21 = jnp.exp(m2 - m1)
    w1 = g_gate / (1.0 + e21)
    w2 = w1 * e21
    shift = g_idx * MOE_EPG
    cw_ref[...] = jnp.where(lane == 0.0, g_idx,
                            jnp.where(lane == i1 - shift, w1, jnp.where(lane == i2 - shift, w2, 0.0)))


def _router_specs(tm, D):
    ins = [pl.BlockSpec((1, D), lambda i: (0, 0)), pl.BlockSpec((D, ROUTER_LANES), lambda i: (0, 0)),
           pl.BlockSpec((1, ROUTER_LANES), lambda i: (0, 0))]
    outs = [pl.BlockSpec((tm, D), lambda i: (i, 0)), pl.BlockSpec((tm, ROUTER_LANES), lambda i: (i, 0))]
    shapes = lambda T: [jax.ShapeDtypeStruct((T, D), BF16), jax.ShapeDtypeStruct((T, ROUTER_LANES), F32)]
    return ins, outs, shapes


def _router_params(nw, wg, bg, we, be):
    D = D_MODEL
    pad = ROUTER_LANES - MOE_GROUPS - MOE_EXPERTS
    wr = jnp.concatenate([wg, we, jnp.zeros((D, pad), F32)], axis=1)
    br = jnp.concatenate([bg, be, jnp.zeros((pad,), F32)]).reshape(1, ROUTER_LANES)
    return nw.reshape(1, D), wr, br


def _dot_tn(a, b):
    return lax.dot_general(a, b, (((0,), (0,)), ((), ())), preferred_element_type=F32)


def _experts_kernel(t_ref, cw_ref, h_ref, w13_ref, w2_ref, fnw_ref, o_ref,
                    pt_ref, xs_ref, cws_ref, oacc_ref, meta_ref, *, final_norm):
    tm = t_ref.shape[0]
    cap = pt_ref.shape[1]
    R = MOE_SUB
    F = MOE_D_FF
    g = pl.program_id(1)
    part = pl.program_id(2)
    n_parts = MOE_EPG // MOE_EXPERTS_PER_STEP

    @pl.when((g == 0) & (part == 0))
    def _():
        cw = cw_ref[...]
        lane_t = lax.broadcasted_iota(jnp.int32, (tm, ROUTER_LANES), 1).astype(F32)
        memb = (cw[:, 0:1] == lane_t) & (lane_t < MOE_GROUPS)
        membf = jnp.where(memb, 1.0, 0.0)
        tr = lax.broadcasted_iota(jnp.int32, (tm, tm), 0)
        tc = lax.broadcasted_iota(jnp.int32, (tm, tm), 1)
        before = jnp.where(tr > tc, 1.0, 0.0).astype(BF16)
        ranks = _dot(before, membf.astype(BF16))
        cnt = jnp.sum(membf, axis=0, keepdims=True)
        nsub = jnp.floor((cnt + (R - 1.0)) * (1.0 / R))
        ur = lax.broadcasted_iota(jnp.int32, (ROUTER_LANES, ROUTER_LANES), 0)
        uc = lax.broadcasted_iota(jnp.int32, (ROUTER_LANES, ROUTER_LANES), 1)
        prefix = jnp.where(ur < uc, 1.0, 0.0).astype(BF16)
        base = _dot_x2(jnp.broadcast_to(nsub * R, (8, ROUTER_LANES)), prefix)[0:1, :]
        pos = jnp.sum(jnp.where(memb, ranks + base, 0.0), axis=-1, keepdims=True)
        col = lax.broadcasted_iota(jnp.int32, (tm, cap), 1).astype(F32)
        pt = jnp.where(pos == col, 1.0, 0.0).astype(BF16)
        pt_ref[...] = pt
        cw_hi, cw_lo = _split2(cw)
        D = t_ref.shape[1]
        gathered = _dot_tn(pt, jnp.concatenate([t_ref[...], cw_hi, cw_lo], axis=1))
        xs_ref[...] = gathered[:, 0:D].astype(BF16)
        cws_ref[...] = gathered[:, D:D + ROUTER_LANES] + gathered[:, D + ROUTER_LANES:D + 2 * ROUTER_LANES]
        oacc_ref[...] = jnp.zeros_like(oacc_ref)
        meta_ref[0:1, :] = nsub
        meta_ref[1:2, :] = base

    lane1 = lax.broadcasted_iota(jnp.int32, (1, ROUTER_LANES), 1)
    n_sub = jnp.sum(jnp.where(lane1 == g, meta_ref[0:1, :], 0.0)).astype(jnp.int32)
    row0 = jnp.sum(jnp.where(lane1 == g, meta_ref[1:2, :], 0.0)).astype(jnp.int32)
    def run_rows(start, n_rows):
        rows = pl.ds(pl.multiple_of(start, R), n_rows)
        lane_r = lax.broadcasted_iota(jnp.int32, (n_rows, ROUTER_LANES), 1)
        xs = xs_ref[rows, :]
        cws = cws_ref[rows, :]
        acc = oacc_ref[rows, :] if n_parts > 1 else None
        for j in range(MOE_EXPERTS_PER_STEP):
            h13 = _dot(xs, w13_ref[j])
            ce = jnp.sum(jnp.where(lane_r == MOE_GROUPS + part * MOE_EXPERTS_PER_STEP + j, cws, 0.0),
                         axis=-1, keepdims=True)
            he = (_silu(h13[:, 0:F]) * h13[:, F:2 * F] * ce).astype(BF16)
            o = _dot(he, w2_ref[j])
            acc = o if acc is None else acc + o
        oacc_ref[rows, :] = acc.astype(oacc_ref.dtype)

    big = MOE_SUBS_PER_PASS
    n_big = jnp.maximum(n_sub - MOE_MAX_PASS + big - 1, 0) // big

    def big_body(s, carry):
        run_rows(row0 + s * (big * R), big * R)
        return carry

    lax.fori_loop(0, n_big, big_body, 0)
    rem = n_sub - n_big * big
    done = row0 + n_big * (big * R)
    for k in range(1, MOE_MAX_PASS + 1):
        @pl.when(rem == k)
        def _(k=k):
            run_rows(done, k * R)

    @pl.when((g == MOE_GROUPS - 1) & (part == n_parts - 1))
    def _():
        out = h_ref[...] + _dot(pt_ref[...], oacc_ref[...].astype(BF16))
        if final_norm:
            out = _rms(out, fnw_ref[...])
        o_ref[...] = out


def _experts(t, cw, h2, w13, w2, fnw, tm, final_norm):
    T, D = h2.shape
    F = MOE_D_FF
    E = MOE_EXPERTS_PER_STEP
    n_parts = MOE_EPG // E
    cap = tm + MOE_GROUPS * MOE_SUB
    return pl.pallas_call(
        functools.partial(_experts_kernel, final_norm=final_norm),
        grid=(T // tm, MOE_GROUPS, n_parts),
        in_specs=[
            pl.BlockSpec((tm, D), lambda i, g, p: (i, 0)),
            pl.BlockSpec((tm, ROUTER_LANES), lambda i, g, p: (i, 0)),
            pl.BlockSpec((tm, D), lambda i, g, p: (i, 0)),
            pl.BlockSpec((E, D, 2 * F), lambda i, g, p: (g * n_parts + p, 0, 0)),
            pl.BlockSpec((E, F, D), lambda i, g, p: (g * n_parts + p, 0, 0)),
            pl.BlockSpec((1, D), lambda i, g, p: (0, 0)),
        ],
        out_specs=pl.BlockSpec((tm, D), lambda i, g, p: (i, 0)),
        out_shape=jax.ShapeDtypeStruct((T, D), F32),
        scratch_shapes=[
            pltpu.VMEM((tm, cap), BF16),
            pltpu.VMEM((cap, D), BF16),
            pltpu.VMEM((cap, ROUTER_LANES), F32),
            pltpu.VMEM((cap, D), F32 if n_parts > 1 else BF16),
            pltpu.VMEM((8, ROUTER_LANES), F32),
        ],
        compiler_params=_cparams(("parallel", "arbitrary", "arbitrary"), MOE_VMEM_LIMIT_BYTES),
    )(t, cw, h2, w13, w2, fnw)


def _moe_experts(t, cw, h2, w1, w3, w2, fnw, final_norm, tm):
    w13 = jnp.concatenate([w1, w3], axis=-1).astype(BF16)
    return _experts(t, cw, h2, w13, w2.astype(BF16), fnw.reshape(1, D_MODEL), tm, final_norm)


def _pick(n, prefs):
    for p in prefs:
        if n % p == 0:
            return p
    return n


def _ssd_retention_layer(h2, B, S, nw, w_in, conv_w, conv_b, dt_bias, a_log, d_skip, norm_w, w_out, router):
    D = D_MODEL
    T = B * S
    qk = RET_HEADS * RET_QK_HEAD
    o_x = D
    o_bc = o_x + D
    o_dt = o_bc + 2 * SSD_GROUPS * SSD_STATE
    o_q = o_dt + SSD_HEADS
    o_k = o_q + qk
    o_v = o_k + qk
    o_g = o_v + D
    w_main = jnp.concatenate([w_in[:, 0:o_x], w_in[:, o_x:o_bc], w_in[:, o_v:o_g], w_in[:, o_g:o_g + D],
                              w_in[:, o_bc:o_dt], w_in[:, o_q:o_k], w_in[:, o_k:o_v]], axis=1).astype(BF16)
    w_dt = jnp.pad(w_in[:, o_dt:o_q], ((0, 0), (0, LANES - SSD_HEADS))).astype(BF16)
    tm = _pick(T, (1024, 512, 256, 128))
    proj, dt = _inproj(h2, nw.reshape(1, D), w_main, w_dt, _pick(T, (2048, 1024, 512, 256, 128)), INPROJ_TN)
    proj3 = proj.reshape(B, S, proj.shape[1])
    dt3 = dt.reshape(B, S, LANES)

    head_of_channel = jnp.arange(D) // SSD_HEAD_DIM
    expand = (jnp.arange(LANES)[:, None] == head_of_channel[None, :]).astype(BF16)
    pad16 = lambda x: jnp.pad(x, (0, LANES - SSD_HEADS)).reshape(1, LANES)
    ssd_params = (conv_w[:, :D], conv_b[:D].reshape(1, D), conv_w[:, D:], conv_b[D:].reshape(1, PROJ_NARROW),
                  pad16(dt_bias), pad16(a_log), expand, jnp.repeat(d_skip, SSD_HEAD_DIM).reshape(1, D),
                  norm_w.reshape(1, D))

    C = RET_CHUNK
    half = RET_QK_HEAD // 2
    pos = jnp.arange(S, dtype=F32)
    inv_freq = ROPE_BASE ** (-jnp.arange(half, dtype=F32) / half)
    ang = pos[:, None] * inv_freq[None, :]
    cc = jnp.concatenate([jnp.cos(ang), jnp.cos(ang)], axis=1)
    ss = jnp.concatenate([-jnp.sin(ang), jnp.sin(ang)], axis=1)
    log_gamma = jnp.log(1.0 - 2.0 ** (-5.0 - jnp.arange(RET_HEADS, dtype=F32)))
    idx = jnp.arange(C, dtype=F32)
    diff = idx[:, None] - idx[None, :]
    dmask = jnp.where(diff[None] >= 0, jnp.exp(jnp.maximum(diff, 0.0)[None] * log_gamma[:, None, None]), 0.0)
    qdec = jnp.repeat(jnp.exp((idx[:, None] + 1.0) * log_gamma[None, :]), RET_V_HEAD, axis=1)
    kdec = jnp.repeat(jnp.exp((C - 1.0 - idx[:, None]) * log_gamma[None, :]), RET_QK_HEAD, axis=1)
    cdec = jnp.repeat(jnp.exp(C * log_gamma), RET_V_HEAD).reshape(1, D)
    y = _mix0(proj3, dt3, ssd_params, (cc, ss, dmask, qdec, kdec, cdec))
    return _outproj(y.reshape(T, 2 * D), w_out.astype(BF16), h2, router, tm)


def _rwkv_layer(h2, B, S, nw, mu, w_r, w_k, w_v, w_o, w0, w1, w2, a0, a1, a2, g1, g2, k_k, k_a, r_k, lnx_w, lnx_b,
                router):
    D = D_MODEL
    T = B * S
    padc = lambda w, n: jnp.pad(w, ((0, 0), (0, n - w.shape[1]))).astype(BF16)
    padr = lambda w, n: jnp.pad(w, ((0, n - w.shape[0]), (0, 0))).astype(BF16)
    head_of_channel = jnp.arange(D) // RWKV_HEAD
    hs = (head_of_channel[:, None] == jnp.arange(LANES)[None, :]).astype(BF16)
    row = lambda x: x.reshape(1, D)
    tm = _pick(S, (512, 256, 128))
    r, k, v, kk, a, lw, g = _rwkv_pre(
        h2, row(nw), mu, w_r.astype(BF16), w_k.astype(BF16), w_v.astype(BF16),
        padc(w1, RWKV_LORA_PAD), padr(w2, RWKV_LORA_PAD), row(w0),
        padc(a1, RWKV_LORA_PAD), padr(a2, RWKV_LORA_PAD), row(a0),
        padc(g1, RWKV_GATE_PAD), padr(g2, RWKV_GATE_PAD), row(k_k), row(k_a), hs, hs.T, S, tm)
    sh = lambda x: x.reshape(B, S, D)
    tb = _pick(S, (512, 256, 128, 64))
    y = _rwkv_scan(sh(r), sh(k), sh(v), sh(kk), sh(a), sh(lw), sh(g),
                   row(lnx_w), row(lnx_b), r_k.reshape(1, D), tb, RWKV_HEADS // RWKV_QUAD,
                   _pick(B, (RWKV_SEQS_PER_STEP, 1)))
    return _rwkv_out(y.reshape(T, D), w_o.astype(BF16), h2, router, _pick(T, (1024, 512, 256, 128)))


def kernel(x, norm_mix_w, norm_ffn_w, norm_final_w, w_in_e, ssd_conv_w, ssd_conv_b, ssd_dt_bias, ssd_a_log, ssd_d, ssd_norm_w, w_out_e, rw_mu, rw_wr, rw_wk, rw_wv, rw_wo, rw_w0, rw_w1, rw_w2, rw_a0, rw_a1, rw_a2, rw_g1, rw_g2, rw_kk, rw_ka, rw_rk, rw_lnx_w, rw_lnx_b, moe_wg, moe_bg, moe_we, moe_be, moe_w1, moe_w3, moe_w2):
    B, S, D = x.shape
    T = B * S
    depth = norm_mix_w.shape[0]
    h = x.reshape(T, D)
    tm_moe = _pick(T, (1024, 512, 256, 128))
    for layer in range(depth):
        i = layer // 2
        router = _router_params(norm_ffn_w[layer], moe_wg[layer], moe_bg[layer], moe_we[layer], moe_be[layer])
        if layer % 2 == 0:
            h, t, cw = _ssd_retention_layer(h, B, S, norm_mix_w[layer], w_in_e[i], ssd_conv_w[i], ssd_conv_b[i],
                                            ssd_dt_bias[i], ssd_a_log[i], ssd_d[i], ssd_norm_w[i], w_out_e[i],
                                            router)
        else:
            h, t, cw = _rwkv_layer(h, B, S, norm_mix_w[layer], rw_mu[i], rw_wr[i], rw_wk[i], rw_wv[i], rw_wo[i],
                                   rw_w0[i], rw_w1[i], rw_w2[i], rw_a0[i], rw_a1[i], rw_a2[i], rw_g1[i],
                                   rw_g2[i], rw_kk[i], rw_ka[i], rw_rk[i], rw_lnx_w[i], rw_lnx_b[i], router)
        h = _moe_experts(t, cw, h, moe_w1[layer], moe_w3[layer], moe_w2[layer], norm_final_w,
                         final_norm=(layer == depth - 1), tm=tm_moe)
    return h.reshape(B, S, D)
```

```python
import functools
import math

import jax
import jax.numpy as jnp
from jax import lax
from jax.experimental import pallas as pl
from jax.experimental.pallas import tpu as pltpu

F32 = jnp.float32
BF16 = jnp.bfloat16

D_MODEL = 1024
RMS_EPS = 1e-6
SSD_HEADS = 16
SSD_HEAD_DIM = 64
SSD_GROUPS = 2
SSD_STATE = 128
SSD_CONV = 4
SSD_CHUNK = 128
MIX0_CHUNKS_PER_STEP = 4
SSD_CONV_TAIL = 16
SSD_NORM_EPS = 1e-5
SSD_GROUP_WIDTH = D_MODEL // SSD_GROUPS
RET_HEADS = 4
RET_QK_HEAD = 128
RET_V_HEAD = 256
RET_CHUNK = 128
ROPE_BASE = 10000.0
RWKV_HEAD = 64
RWKV_HEADS = 16
RWKV_LN_EPS = 64e-5
RWKV_CHUNK = 64
RWKV_QUAD = 4
RWKV_SEQS_PER_STEP = 2
RWKV_LORA_PAD = 128
RWKV_GATE_PAD = 256
RWKV_PRE_SPLIT = 2
MOE_GROUPS = 4
MOE_EPG = 4
MOE_EXPERTS = 16
MOE_D_FF = 512
ROUTER_LANES = 128
MOE_SUB = 64
MOE_SUBS_PER_PASS = 4
MOE_MAX_PASS = 6
MOE_EXPERTS_PER_STEP = 4

LANES = 128
SUBLANES = 8
VMEM_LIMIT_BYTES = 56 * 1024 * 1024
MOE_VMEM_LIMIT_BYTES = 60 * 1024 * 1024

PROJ_NARROW = 512
PROJ_BLOCK_Z, PROJ_BLOCK_X, PROJ_BLOCK_V, PROJ_BLOCK_G = 0, 1, 2, 3
PROJ_BLOCK_BC, PROJ_BLOCK_Q, PROJ_BLOCK_K = 8, 9, 10
PROJ_WIDTH = 4 * D_MODEL + 3 * PROJ_NARROW
INPROJ_TN = PROJ_WIDTH // 4


def _cparams(sem, vmem_limit_bytes=VMEM_LIMIT_BYTES):
    return pltpu.CompilerParams(dimension_semantics=sem, vmem_limit_bytes=vmem_limit_bytes)


def _dot(a, b):
    return jnp.dot(a, b, preferred_element_type=F32)


def _dot_nt(a, b):
    return lax.dot_general(a, b, (((1,), (1,)), ((), ())), preferred_element_type=F32)


def _split2(x):
    hi = x.astype(BF16)
    lo = (x - hi.astype(F32)).astype(BF16)
    return hi, lo


def _dot_x2(x, w_exact):
    hi, lo = _split2(x)
    return _dot(jnp.concatenate([hi, lo], axis=1), jnp.concatenate([w_exact, w_exact], axis=0))


def _dot_2x(ww_exact, x):
    hi, lo = _split2(x)
    return _dot(ww_exact, jnp.concatenate([hi, lo], axis=0))


def _dot_x3(x, w):
    xh, xl = _split2(x)
    wh, wl = _split2(w)
    n = w.shape[1]
    both = _dot(xh, jnp.concatenate([wh, wl], axis=1))
    return both[:, 0:n] + both[:, n:2 * n] + _dot(xl, wh)


def _sigmoid(x):
    return 0.5 * jnp.tanh(0.5 * x) + 0.5


def _silu(x):
    hx = 0.5 * x
    return hx * jnp.tanh(hx) + hx


def _softplus(x):
    return jnp.maximum(x, 0.0) + jnp.log(1.0 + jnp.exp(-jnp.abs(x)))


def _rms(x, w, eps=RMS_EPS):
    return x * lax.rsqrt(jnp.mean(x * x, axis=-1, keepdims=True) + eps) * w


def _inproj_kernel(x_ref, nw_ref, w_ref, wdt_ref, o_ref, dt_ref, u_ref):
    @pl.when(pl.program_id(1) == 0)
    def _():
        ub = _rms(x_ref[...], nw_ref[...]).astype(BF16)
        u_ref[...] = ub
        dt_ref[...] = _dot(ub, wdt_ref[...])

    o_ref[...] = _dot(u_ref[...], w_ref[...]).astype(o_ref.dtype)


def _inproj(h2, nw, w_main, w_dt, tm, tn):
    T, D = h2.shape
    N = w_main.shape[1]
    return pl.pallas_call(
        _inproj_kernel,
        grid=(T // tm, N // tn),
        in_specs=[
            pl.BlockSpec((tm, D), lambda i, j: (i, 0)),
            pl.BlockSpec((1, D), lambda i, j: (0, 0)),
            pl.BlockSpec((D, tn), lambda i, j: (0, j)),
            pl.BlockSpec((D, LANES), lambda i, j: (0, 0)),
        ],
        out_specs=[
            pl.BlockSpec((tm, tn), lambda i, j: (i, j)),
            pl.BlockSpec((tm, LANES), lambda i, j: (i, 0)),
        ],
        out_shape=[
            jax.ShapeDtypeStruct((T, N), BF16),
            jax.ShapeDtypeStruct((T, LANES), F32),
        ],
        scratch_shapes=[pltpu.VMEM((tm, D), BF16)],
        compiler_params=_cparams(("parallel", "arbitrary")),
    )(h2, nw, w_main, w_dt)


def _ssd_chunk(z_ref, x_ref, bc_ref, dt_ref, cwx_ref, cbx_ref, cwb_ref, cbb_ref,
               dtb_ref, alog_ref, e_ref, dskip_ref, nw_ref, o_ref,
               state_ref, xtail, bctail, ybuf, *, rows, lane0):
    L = SSD_CHUNK
    TAIL = xtail.shape[0]
    srow = lax.broadcasted_iota(jnp.int32, (SSD_CONV * L, TAIL + L), 0)
    scol = lax.broadcasted_iota(jnp.int32, (SSD_CONV * L, TAIL + L), 1)
    shift = jnp.where(scol == (srow % L) + (srow // L) + (TAIL - SSD_CONV + 1), 1.0, 0.0).astype(BF16)

    def conv(cur_ref, tail, w_ref, b_ref):
        cur = cur_ref[0, rows, :]
        taps = _dot(shift, jnp.concatenate([tail[...], cur], axis=0))
        acc = b_ref[...] + w_ref[0:1, :] * taps[0:L, :]
        for k in range(1, SSD_CONV):
            acc = acc + w_ref[k:k + 1, :] * taps[k * L:(k + 1) * L, :]
        tail[...] = cur[L - TAIL:L, :]
        return _silu(acc)

    xs = conv(x_ref, xtail, cwx_ref, cbx_ref)
    bc = conv(bc_ref, bctail, cwb_ref, cbb_ref)

    dt = _softplus(dt_ref[0, rows, :] + dtb_ref[...])
    a = dt * (-jnp.exp(alog_ref[...]))
    row = lax.broadcasted_iota(jnp.int32, (L, L), 0)
    col = lax.broadcasted_iota(jnp.int32, (L, L), 1)
    causal = row >= col
    row2 = lax.broadcasted_iota(jnp.int32, (L, 2 * L), 0)
    col2 = lax.broadcasted_iota(jnp.int32, (L, 2 * L), 1)
    tri2 = jnp.where(row2 >= col2 % L, 1.0, 0.0).astype(BF16)
    acum = _dot_2x(tri2, a)
    acum_t = acum.T
    e = e_ref[...]
    dt_full = _dot_x2(dt, e)
    acum_full = _dot_x2(acum, e)
    xdt = xs * dt_full
    alast_full = acum_full[L - 1:L, :]
    decay_in = jnp.exp(acum_full)
    xdt_end = (xdt * jnp.exp(alast_full - acum_full)).astype(BF16)
    xdt_b = xdt.astype(BF16)
    lane = lax.broadcasted_iota(jnp.int32, (L, LANES), 1)
    first_head = lane < SSD_HEAD_DIM

    GW = SSD_GROUP_WIDTH
    for g in range(SSD_GROUPS):
        bg = bc[:, g * SSD_STATE:(g + 1) * SSD_STATE]
        cg = bc[:, (SSD_GROUPS + g) * SSD_STATE:(SSD_GROUPS + g + 1) * SSD_STATE].astype(BF16)
        cb = _dot_nt(cg, bg.astype(BF16))
        st = state_ref[g]
        y_off = _dot(cg, st.astype(BF16)) * decay_in[:, g * GW:(g + 1) * GW]
        for p in range(GW // LANES):
            xp = xdt_b[:, g * GW + p * LANES:g * GW + (p + 1) * LANES]
            ys = []
            for s in range(2):
                hd = g * (SSD_HEADS // SSD_GROUPS) + 2 * p + s
                seg = acum[:, hd:hd + 1] - acum_t[hd:hd + 1, :]
                dec = jnp.where(causal, jnp.exp(seg), 0.0)
                ys.append(_dot((cb * dec).astype(BF16), xp))
            yd = jnp.where(first_head, ys[0], ys[1])
            lo = g * GW + p * LANES
            ybuf[:, lo:lo + LANES] = yd + y_off[:, p * LANES:(p + 1) * LANES]
        bg_t = bg.T.astype(BF16)
        state_ref[g] = st * jnp.exp(alast_full[:, g * GW:(g + 1) * GW]) + _dot(
            bg_t, xdt_end[:, g * GW:(g + 1) * GW])

    y = ybuf[...] + xs * dskip_ref[...]
    y = y * _silu(z_ref[0, rows, :].astype(F32))
    for g in range(SSD_GROUPS):
        yg = y[:, g * GW:(g + 1) * GW]
        yg = yg * lax.rsqrt(jnp.mean(yg * yg, axis=-1, keepdims=True) + SSD_NORM_EPS)
        lo = lane0 + g * GW
        o_ref[0, rows, lo:lo + GW] = (yg * nw_ref[:, g * GW:(g + 1) * GW]).astype(o_ref.dtype)


def _ret_chunk(q_ref, k_ref, v_ref, g_ref, cc_ref, ss_ref, dmask_ref, qdec_ref, kdec_ref,
               cdec_ref, o_ref, r_ref, *, rows, lane0):
    cc = cc_ref[rows, :]
    ss = ss_ref[rows, :]
    dk, dv = RET_QK_HEAD, RET_V_HEAD
    half = dk // 2

    def rope(x):
        return x * cc + pltpu.roll(x, half, 1) * ss

    heads = range(RET_HEADS)
    qs = [rope(q_ref[0, rows, hd * dk:(hd + 1) * dk].astype(F32)) for hd in heads]
    ks = [rope(k_ref[0, rows, hd * dk:(hd + 1) * dk].astype(F32)) * (dk ** -0.5) for hd in heads]
    vs = [v_ref[0, rows, hd * dv:(hd + 1) * dv] for hd in heads]
    qbs = [q.astype(BF16) for q in qs]
    ss = [_dot_nt(qbs[hd], ks[hd].astype(BF16)) * dmask_ref[hd] for hd in heads]
    r_olds = [r_ref[hd] for hd in heads]
    cross = [_dot(qbs[hd], r_olds[hd].astype(BF16)) for hd in heads]
    kd_ts = [(ks[hd] * kdec_ref[:, hd * dk:(hd + 1) * dk]).T.astype(BF16) for hd in heads]
    upds = [_dot(kd_ts[hd], vs[hd]) for hd in heads]
    inner = [_dot(ss[hd].astype(BF16), vs[hd]) for hd in heads]
    for hd in heads:
        r_ref[hd] = r_olds[hd] * cdec_ref[:, hd * dv:(hd + 1) * dv] + upds[hd]
        y = inner[hd] + cross[hd] * qdec_ref[:, hd * dv:(hd + 1) * dv]
        y = y * lax.rsqrt(jnp.mean(y * y, axis=-1, keepdims=True) + RMS_EPS)
        y = y * _silu(g_ref[0, rows, hd * dv:(hd + 1) * dv].astype(F32))
        lo = lane0 + hd * dv
        o_ref[0, rows, lo:lo + dv] = y.astype(o_ref.dtype)


N_SSD_IN = 13
N_RET_IN = 10


def _mix0_kernel(*refs):
    ssd_in = refs[:N_SSD_IN]
    ret_in = refs[N_SSD_IN:N_SSD_IN + N_RET_IN]
    o_ref, state_ref, xtail, bctail, ybuf, r_ref = refs[N_SSD_IN + N_RET_IN:]

    @pl.when(pl.program_id(1) == 0)
    def _():
        state_ref[...] = jnp.zeros_like(state_ref)
        xtail[...] = jnp.zeros_like(xtail)
        bctail[...] = jnp.zeros_like(bctail)
        r_ref[...] = jnp.zeros_like(r_ref)

    for ci in range(ybuf.shape[0]):
        rows = slice(ci * SSD_CHUNK, (ci + 1) * SSD_CHUNK)
        _ssd_chunk(*ssd_in, o_ref, state_ref, xtail, bctail, ybuf.at[ci], rows=rows, lane0=0)
        _ret_chunk(*ret_in, o_ref, r_ref, rows=rows, lane0=D_MODEL)


def _mix0(proj3, dt3, ssd_params, ret_params):
    B, S, _ = proj3.shape
    n_chunks = _pick(S // SSD_CHUNK, (MIX0_CHUNKS_PER_STEP, 1))
    L = n_chunks * SSD_CHUNK
    D = D_MODEL
    assert RET_CHUNK == SSD_CHUNK and len(ssd_params) + 4 == N_SSD_IN and len(ret_params) + 4 == N_RET_IN
    full = lambda arr: pl.BlockSpec(arr.shape, lambda b, c: (0,) * arr.ndim)
    cc, ss = ret_params[:2]
    ret_specs = [pl.BlockSpec((L, RET_QK_HEAD), lambda b, c: (c, 0))] * 2 + [full(p) for p in ret_params[2:]]
    return pl.pallas_call(
        _mix0_kernel,
        grid=(B, S // L),
        in_specs=[
            pl.BlockSpec((1, L, D), lambda b, c: (b, c, PROJ_BLOCK_Z)),
            pl.BlockSpec((1, L, D), lambda b, c: (b, c, PROJ_BLOCK_X)),
            pl.BlockSpec((1, L, PROJ_NARROW), lambda b, c: (b, c, PROJ_BLOCK_BC)),
            pl.BlockSpec((1, L, LANES), lambda b, c: (b, c, 0)),
        ] + [full(p) for p in ssd_params] + [
            pl.BlockSpec((1, L, PROJ_NARROW), lambda b, c: (b, c, PROJ_BLOCK_Q)),
            pl.BlockSpec((1, L, PROJ_NARROW), lambda b, c: (b, c, PROJ_BLOCK_K)),
            pl.BlockSpec((1, L, D), lambda b, c: (b, c, PROJ_BLOCK_V)),
            pl.BlockSpec((1, L, D), lambda b, c: (b, c, PROJ_BLOCK_G)),
        ] + ret_specs,
        out_specs=pl.BlockSpec((1, L, 2 * D), lambda b, c: (b, c, 0)),
        out_shape=jax.ShapeDtypeStruct((B, S, 2 * D), BF16),
        scratch_shapes=[
            pltpu.VMEM((SSD_GROUPS, SSD_STATE, SSD_GROUP_WIDTH), F32),
            pltpu.VMEM((SSD_CONV_TAIL, D), BF16),
            pltpu.VMEM((SSD_CONV_TAIL, PROJ_NARROW), BF16),
            pltpu.VMEM((n_chunks, SSD_CHUNK, D), F32),
            pltpu.VMEM((RET_HEADS, RET_QK_HEAD, RET_V_HEAD), F32),
        ],
        compiler_params=_cparams(("parallel", "arbitrary")),
    )(proj3, proj3, proj3, dt3, *ssd_params, proj3, proj3, proj3, proj3, *ret_params)


def _outproj_kernel(y_ref, w_ref, h_ref, nw_ref, wr_ref, br_ref, o_ref, t_ref, cw_ref):
    h = h_ref[...] + _dot(y_ref[...], w_ref[...])
    o_ref[...] = h
    _route_rows(h, nw_ref, wr_ref, br_ref, t_ref, cw_ref)


def _outproj(y, w, h2, router, tm):
    T, D = h2.shape
    K = y.shape[1]
    r_in, r_out, r_shapes = _router_specs(tm, D)
    return pl.pallas_call(
        _outproj_kernel,
        grid=(T // tm,),
        in_specs=[
            pl.BlockSpec((tm, K), lambda i: (i, 0)),
            pl.BlockSpec((K, D), lambda i: (0, 0)),
            pl.BlockSpec((tm, D), lambda i: (i, 0)),
        ] + r_in,
        out_specs=[pl.BlockSpec((tm, D), lambda i: (i, 0))] + r_out,
        out_shape=[jax.ShapeDtypeStruct((T, D), F32)] + r_shapes(T),
        compiler_params=_cparams(("parallel",)),
    )(y, w, h2, *router)


def _rwkv_pre_kernel(h_ref, hp_ref, nw_ref, mu_ref, wr_ref, wk_ref, wv_ref,
                     w1_ref, w2_ref, w0_ref, a1_ref, a2_ref, a0_ref, g1_ref, g2_ref,
                     kkw_ref, kaw_ref, hs_ref, hst_ref,
                     r_o, k_o, v_o, kk_o, a_o, lw_o, g_o, ubuf, xxbuf, *, tiles_per_seq):
    tm = h_ref.shape[0]
    i = pl.program_id(0)
    nw = nw_ref[...]
    u = _rms(h_ref[...], nw)
    up = _rms(hp_ref[...], nw)
    seq_start = (i % tiles_per_seq) == 0
    P = SUBLANES
    ubuf[P:P + tm, :] = u
    ubuf[0:P, :] = jnp.where(seq_start, 0.0, up)
    xxbuf[...] = ubuf[P - 1:P - 1 + tm, :] - u

    def mix(j, lo, n):
        return (ubuf[P + lo:P + lo + n, :] + xxbuf[lo:lo + n, :] * mu_ref[j:j + 1, :]).astype(BF16)

    n = tm // RWKV_PRE_SPLIT
    parts = [s * n for s in range(RWKV_PRE_SPLIT)]
    first = []
    for lo in parts:
        first.append(dict(
            r=_dot(mix(0, lo, n), wr_ref[...]), k=_dot(mix(2, lo, n), wk_ref[...]),
            v=_dot(mix(3, lo, n), wv_ref[...]), w1=_dot(mix(1, lo, n), w1_ref[...]),
            a1=_dot(mix(4, lo, n), a1_ref[...]), g1=_dot(mix(5, lo, n), g1_ref[...])))
    second = []
    for f in first:
        kk = f["k"] * kkw_ref[...]
        second.append(dict(
            wl=w0_ref[...] + _dot(jnp.tanh(f["w1"]).astype(BF16), w2_ref[...]),
            al=a0_ref[...] + _dot(f["a1"].astype(BF16), a2_ref[...]),
            g=_dot(_sigmoid(f["g1"]).astype(BF16), g2_ref[...]),
            kk=kk, ssq=_dot((kk * kk).astype(BF16), hs_ref[...])))
    for lo, f, s in zip(parts, first, second):
        rows = slice(lo, lo + n)
        w_log = -_softplus(-s["wl"]) - 0.5
        a = _sigmoid(s["al"])
        inv = lax.rsqrt(jnp.maximum(s["ssq"], 1e-24))
        r_o[rows, :] = f["r"].astype(r_o.dtype)
        k_o[rows, :] = (f["k"] * (1.0 + (a - 1.0) * kaw_ref[...])).astype(k_o.dtype)
        v_o[rows, :] = f["v"].astype(v_o.dtype)
        kk_o[rows, :] = (s["kk"] * _dot_x2(inv, hst_ref[...])).astype(kk_o.dtype)
        a_o[rows, :] = a.astype(a_o.dtype)
        lw_o[rows, :] = -jnp.exp(w_log)
        g_o[rows, :] = s["g"].astype(g_o.dtype)


def _rwkv_pre(h2, nw, mu, wr, wk, wv, w1, w2, w0, a1, a2, a0, g1, g2, kkw, kaw, hs, hst, S, tm):
    T, D = h2.shape
    full = lambda arr: pl.BlockSpec(arr.shape, lambda i: (0,) * arr.ndim)
    row = pl.BlockSpec((tm, D), lambda i: (i, 0))
    params = (nw, mu, wr, wk, wv, w1, w2, w0, a1, a2, a0, g1, g2, kkw, kaw, hs, hst)
    bf = jax.ShapeDtypeStruct((T, D), BF16)
    return pl.pallas_call(
        functools.partial(_rwkv_pre_kernel, tiles_per_seq=S // tm),
        grid=(T // tm,),
        in_specs=[row, pl.BlockSpec((SUBLANES, D), lambda i: (jnp.maximum(i * (tm // SUBLANES) - 1, 0), 0))]
        + [full(p) for p in params],
        out_specs=[row] * 7,
        out_shape=[bf, bf, bf, bf, bf, jax.ShapeDtypeStruct((T, D), F32), bf],
        scratch_shapes=[pltpu.VMEM((tm + SUBLANES, D), F32), pltpu.VMEM((tm, D), F32)],
        compiler_params=_cparams(("parallel",)),
    )(h2, h2, *params)


def _rwkv_scan4_kernel(r_ref, k_ref, v_ref, kk_ref, a_ref, lw_ref, g_ref, lnw_ref, lnb_ref, rk_ref,
                       o_ref, state_ref, y_ref, *, n_chunks, n_quads):
    C = RWKV_CHUNK
    N = RWKV_HEAD
    HQ = RWKV_QUAD
    QW = HQ * N
    t = pl.program_id(2)

    @pl.when(t == 0)
    def _():
        state_ref[...] = jnp.zeros_like(state_ref)

    row2 = lax.broadcasted_iota(jnp.int32, (C, 2 * C), 0)
    col2 = lax.broadcasted_iota(jnp.int32, (C, 2 * C), 1)
    tri_incl2 = jnp.where(row2 >= col2 % C, 1.0, 0.0).astype(BF16)
    prow = lax.broadcasted_iota(jnp.int32, (C, QW), 0)
    plane = lax.broadcasted_iota(jnp.int32, (C, QW), 1)
    eye_p = jnp.where(prow == plane % C, 1.0, 0.0).astype(F32)
    head_masks = [plane // N == h for h in range(HQ)]
    grow = lax.broadcasted_iota(jnp.int32, (2 * C, 2 * QW), 0)
    gcol = lax.broadcasted_iota(jnp.int32, (2 * C, 2 * QW), 1)
    gmask = (grow % C + grow // C) > (gcol % C)
    srow = lax.broadcasted_iota(jnp.int32, (QW, QW), 0)
    scol = lax.broadcasted_iota(jnp.int32, (QW, QW), 1)
    state_blocks = (srow // N) == (scol // N)
    n_levels = int(math.log2(C))
    n_batch = r_ref.shape[0]
    chains = [(bi, q) for bi in range(n_batch) for q in range(n_quads)]

    head_ones = [jnp.where(m, 1.0, 0.0).astype(BF16) for m in head_masks]

    def bdiag(x):
        xb = x.astype(BF16)
        return jnp.concatenate([xb * one for one in head_ones], axis=0)

    first_of_pair = lax.broadcasted_iota(jnp.int32, (C, LANES), 1) < N

    def head_sum(x):
        halves = []
        for lo in range(0, QW, LANES):
            xh = x[:, lo:lo + LANES]
            total = jnp.sum(xh, axis=-1, keepdims=True)
            first = jnp.sum(jnp.where(first_of_pair, xh, 0.0), axis=-1, keepdims=True)
            halves.append(jnp.where(first_of_pair, first, total - first))
        return jnp.concatenate(halves, axis=1)

    def chunk_body(ci, carry):
        rows = pl.ds(pl.multiple_of(ci * C, C), C)
        cum_all = [_dot_2x(tri_incl2, lw_ref[bi, rows, :]) for bi in range(n_batch)]
        pre = []
        for bi, q in chains:
            lanes = slice(q * QW, (q + 1) * QW)
            r = r_ref[bi, rows, lanes].astype(F32)
            k = k_ref[bi, rows, lanes].astype(F32)
            v = v_ref[bi, rows, lanes].astype(F32)
            kk = kk_ref[bi, rows, lanes].astype(F32)
            a = a_ref[bi, rows, lanes].astype(F32)
            lw = lw_ref[bi, rows, lanes]
            cum = cum_all[bi][:, lanes]
            wc = cum[C - 1:C, :]
            e_out = jnp.exp(-cum)
            e_end = jnp.exp(wc - cum)
            b = kk * a
            at = -kk * jnp.exp(cum - lw)
            rt = r * jnp.exp(cum)
            pre.append(dict(lanes=lanes, r=r, k=k, v=v, wc=wc, at=at, rt=rt,
                            bh=b * e_out, kh=k * e_out,
                            kb_end=jnp.concatenate([k * e_end, b * e_end], axis=0).astype(BF16)))

        grams = []
        for d in pre:
            lhs = jnp.concatenate([d["at"], d["rt"]], axis=0).astype(BF16)
            rhs = jnp.concatenate([bdiag(d["bh"]), bdiag(d["kh"])], axis=0)
            grams.append(jnp.where(gmask, _dot_nt(lhs, rhs), 0.0))
        lps = [g[0:C, 0:QW] for g in grams]
        bd_vs = [bdiag(d["v"]) for d in pre]
        mvs = [_dot(g[0:C, QW:2 * QW].astype(BF16), bd_v) for g, bd_v in zip(grams, bd_vs)]
        pks = [eye_p + lp for lp in lps]
        lks = [_dot(lp.astype(BF16), bdiag(lp)) for lp in lps]
        for lvl in range(1, n_levels):
            if lvl < n_levels - 1:
                boths = [_dot(jnp.concatenate([pk, lk], axis=0).astype(BF16), bdiag(lk))
                         for pk, lk in zip(pks, lks)]
                pks = [pk + bo[0:C, :] for pk, bo in zip(pks, boths)]
                lks = [bo[C:2 * C, :] for bo in boths]
            else:
                pks = [pk + _dot(pk.astype(BF16), bdiag(lk)) for pk, lk in zip(pks, lks)]
        tws = [_dot(pk.astype(BF16), jnp.concatenate([bdiag(mv), bdiag(d["at"])], axis=1))
               for pk, mv, d in zip(pks, mvs, pre)]

        sts = [state_ref[bi, q] for bi, q in chains]
        zs = [_dot_nt(jnp.concatenate([tw[:, QW:2 * QW], d["rt"]], axis=0).astype(BF16), st.astype(BF16))
              for tw, d, st in zip(tws, pre, sts)]
        us = [tw[:, 0:QW] + z[0:C, :] for tw, z in zip(tws, zs)]
        yss = [_dot(g[C:2 * C, :].astype(BF16), jnp.concatenate([bdiag(u), bd_v], axis=0))
               for g, u, bd_v in zip(grams, us, bd_vs)]
        upds = [_dot(jnp.concatenate([d["v"], u], axis=0).T.astype(BF16), d["kb_end"])
                for d, u in zip(pre, us)]
        for i, (bi, q) in enumerate(chains):
            d = pre[i]
            state_ref[bi, q] = sts[i] * jnp.exp(d["wc"]) + jnp.where(state_blocks, upds[i], 0.0)
            y_ref[bi, :, d["lanes"]] = zs[i][C:2 * C, :] + yss[i]
        return carry

    def finish(ci):
        rows = pl.ds(pl.multiple_of(ci * C, C), C)
        for bi, q in chains:
            lanes = slice(q * QW, (q + 1) * QW)
            y = y_ref[bi, :, lanes]
            mean = head_sum(y) * (1.0 / N)
            yc = y - mean
            var = head_sum(yc * yc) * (1.0 / N)
            yn = yc * lax.rsqrt(var + RWKV_LN_EPS) * lnw_ref[:, lanes] + lnb_ref[:, lanes]
            r = r_ref[bi, rows, lanes].astype(F32)
            k = k_ref[bi, rows, lanes].astype(F32)
            v = v_ref[bi, rows, lanes].astype(F32)
            bonus = head_sum(r * k * rk_ref[:, lanes]) * v
            out = (yn + bonus) * g_ref[bi, rows, lanes].astype(F32)
            o_ref[bi, rows, lanes] = out.astype(o_ref.dtype)

    y_ref[...] = jnp.zeros_like(y_ref)

    def loop_body(ci, carry):
        finish(jnp.maximum(ci - 1, 0))
        return chunk_body(ci, carry)

    lax.fori_loop(0, n_chunks, loop_body, 0)
    finish(n_chunks - 1)


def _rwkv_scan(r, k, v, kk, a, lw, g, lnw, lnb, rk, tb, n_quads, nb):
    B, S, D = r.shape
    QW = RWKV_QUAD * RWKV_HEAD
    W = n_quads * QW
    blk = pl.BlockSpec((nb, tb, W), lambda b, j, t: (b, t, j))
    par = pl.BlockSpec((1, W), lambda b, j, t: (0, j))
    return pl.pallas_call(
        functools.partial(_rwkv_scan4_kernel, n_chunks=tb // RWKV_CHUNK, n_quads=n_quads),
        grid=(B // nb, D // W, S // tb),
        in_specs=[blk] * 7 + [par] * 3,
        out_specs=blk,
        out_shape=jax.ShapeDtypeStruct((B, S, D), BF16),
        scratch_shapes=[pltpu.VMEM((nb, n_quads, QW, QW), F32), pltpu.VMEM((nb, RWKV_CHUNK, W), F32)],
        compiler_params=_cparams(("parallel", "parallel", "arbitrary")),
    )(r, k, v, kk, a, lw, g, lnw, lnb, rk)


def _rwkv_out_kernel(y_ref, w_ref, h_ref, nw_ref, wr_ref, br_ref, o_ref, t_ref, cw_ref):
    h = h_ref[...] + _dot(y_ref[...], w_ref[...])
    o_ref[...] = h
    _route_rows(h, nw_ref, wr_ref, br_ref, t_ref, cw_ref)


def _rwkv_out(y, wo, h2, router, tm):
    T, D = h2.shape
    r_in, r_out, r_shapes = _router_specs(tm, D)
    return pl.pallas_call(
        _rwkv_out_kernel,
        grid=(T // tm,),
        in_specs=[pl.BlockSpec((tm, D), lambda i: (i, 0)), pl.BlockSpec((D, D), lambda i: (0, 0)),
                  pl.BlockSpec((tm, D), lambda i: (i, 0))] + r_in,
        out_specs=[pl.BlockSpec((tm, D), lambda i: (i, 0))] + r_out,
        out_shape=[jax.ShapeDtypeStruct((T, D), F32)] + r_shapes(T),
        compiler_params=_cparams(("parallel",)),
    )(y, wo, h2, *router)


def _route_rows(h, nw_ref, wr_ref, br_ref, t_ref, cw_ref):
    tn = _rms(h, nw_ref[...])
    t_ref[...] = tn.astype(t_ref.dtype)
    logits = _dot_x3(tn, wr_ref[...]) + br_ref[...]
    lane = lax.broadcasted_iota(jnp.int32, logits.shape, 1).astype(F32)
    neg = jnp.float32(-jnp.inf)
    big = jnp.float32(1 << 20)
    is_g = lane < MOE_GROUPS
    gl = jnp.where(is_g, logits, neg)
    gmax = jnp.max(gl, axis=-1, keepdims=True)
    g_idx = jnp.min(jnp.where(gl == gmax, lane, big), axis=-1, keepdims=True)
    g_gate = 1.0 / jnp.sum(jnp.where(is_g, jnp.exp(logits - gmax), 0.0), axis=-1, keepdims=True)
    lo = MOE_GROUPS + g_idx * MOE_EPG
    sel = (lane >= lo) & (lane < lo + MOE_EPG)
    el = jnp.where(sel, logits, neg)
    m1 = jnp.max(el, axis=-1, keepdims=True)
    i1 = jnp.min(jnp.where(el == m1, lane, big), axis=-1, keepdims=True)
    el2 = jnp.where(lane == i1, neg, el)
    m2 = jnp.max(el2, axis=-1, keepdims=True)
    i2 = jnp.min(jnp.where(el2 == m2, lane, big), axis=-1, keepdims=True)
    e21 = jnp.exp(m2 - m1)
    w1 = g_gate / (1.0 + e21)
    w2 = w1 * e21
    shift = g_idx * MOE_EPG
    cw_ref[...] = jnp.where(lane == 0.0, g_idx,
                            jnp.where(lane == i1 - shift, w1, jnp.where(lane == i2 - shift, w2, 0.0)))


def _router_specs(tm, D):
    ins = [pl.BlockSpec((1, D), lambda i: (0, 0)), pl.BlockSpec((D, ROUTER_LANES), lambda i: (0, 0)),
           pl.BlockSpec((1, ROUTER_LANES), lambda i: (0, 0))]
    outs = [pl.BlockSpec((tm, D), lambda i: (i, 0)), pl.BlockSpec((tm, ROUTER_LANES), lambda i: (i, 0))]
    shapes = lambda T: [jax.ShapeDtypeStruct((T, D), BF16), jax.ShapeDtypeStruct((T, ROUTER_LANES), F32)]
    return ins, outs, shapes


def _router_params(nw, wg, bg, we, be):
    D = D_MODEL
    pad = ROUTER_LANES - MOE_GROUPS - MOE_EXPERTS
    wr = jnp.concatenate([wg, we, jnp.zeros((D, pad), F32)], axis=1)
    br = jnp.concatenate([bg, be, jnp.zeros((pad,), F32)]).reshape(1, ROUTER_LANES)
    return nw.reshape(1, D), wr, br


def _dot_tn(a, b):
    return lax.dot_general(a, b, (((0,), (0,)), ((), ())), preferred_element_type=F32)


def _experts_kernel(t_ref, cw_ref, h_ref, w13_ref, w2_ref, fnw_ref, o_ref,
                    pt_ref, xs_ref, cws_ref, oacc_ref, meta_ref, *, final_norm):
    tm = t_ref.shape[0]
    cap = pt_ref.shape[1]
    R = MOE_SUB
    F = MOE_D_FF
    g = pl.program_id(1)
    part = pl.program_id(2)
    n_parts = MOE_EPG // MOE_EXPERTS_PER_STEP

    @pl.when((g == 0) & (part == 0))
    def _():
        cw = cw_ref[...]
        lane_t = lax.broadcasted_iota(jnp.int32, (tm, ROUTER_LANES), 1).astype(F32)
        memb = (cw[:, 0:1] == lane_t) & (lane_t < MOE_GROUPS)
        membf = jnp.where(memb, 1.0, 0.0)
        tr = lax.broadcasted_iota(jnp.int32, (tm, tm), 0)
        tc = lax.broadcasted_iota(jnp.int32, (tm, tm), 1)
        before = jnp.where(tr > tc, 1.0, 0.0).astype(BF16)
        ranks = _dot(before, membf.astype(BF16))
        cnt = jnp.sum(membf, axis=0, keepdims=True)
        nsub = jnp.floor((cnt + (R - 1.0)) * (1.0 / R))
        ur = lax.broadcasted_iota(jnp.int32, (ROUTER_LANES, ROUTER_LANES), 0)
        uc = lax.broadcasted_iota(jnp.int32, (ROUTER_LANES, ROUTER_LANES), 1)
        prefix = jnp.where(ur < uc, 1.0, 0.0).astype(BF16)
        base = _dot_x2(jnp.broadcast_to(nsub * R, (8, ROUTER_LANES)), prefix)[0:1, :]
        pos = jnp.sum(jnp.where(memb, ranks + base, 0.0), axis=-1, keepdims=True)
        col = lax.broadcasted_iota(jnp.int32, (tm, cap), 1).astype(F32)
        pt = jnp.where(pos == col, 1.0, 0.0).astype(BF16)
        pt_ref[...] = pt
        cw_hi, cw_lo = _split2(cw)
        D = t_ref.shape[1]
        gathered = _dot_tn(pt, jnp.concatenate([t_ref[...], cw_hi, cw_lo], axis=1))
        xs_ref[...] = gathered[:, 0:D].astype(BF16)
        cws_ref[...] = gathered[:, D:D + ROUTER_LANES] + gathered[:, D + ROUTER_LANES:D + 2 * ROUTER_LANES]
        oacc_ref[...] = jnp.zeros_like(oacc_ref)
        meta_ref[0:1, :] = nsub
        meta_ref[1:2, :] = base

    lane1 = lax.broadcasted_iota(jnp.int32, (1, ROUTER_LANES), 1)
    n_sub = jnp.sum(jnp.where(lane1 == g, meta_ref[0:1, :], 0.0)).astype(jnp.int32)
    row0 = jnp.sum(jnp.where(lane1 == g, meta_ref[1:2, :], 0.0)).astype(jnp.int32)
    def run_rows(start, n_rows):
        rows = pl.ds(pl.multiple_of(start, R), n_rows)
        lane_r = lax.broadcasted_iota(jnp.int32, (n_rows, ROUTER_LANES), 1)
        xs = xs_ref[rows, :]
        cws = cws_ref[rows, :]
        acc = oacc_ref[rows, :] if n_parts > 1 else None
        for j in range(MOE_EXPERTS_PER_STEP):
            h13 = _dot(xs, w13_ref[j])
            ce = jnp.sum(jnp.where(lane_r == MOE_GROUPS + part * MOE_EXPERTS_PER_STEP + j, cws, 0.0),
                         axis=-1, keepdims=True)
            he = (_silu(h13[:, 0:F]) * h13[:, F:2 * F] * ce).astype(BF16)
            o = _dot(he, w2_ref[j])
            acc = o if acc is None else acc + o
        oacc_ref[rows, :] = acc.astype(oacc_ref.dtype)

    big = MOE_SUBS_PER_PASS
    n_big = jnp.maximum(n_sub - MOE_MAX_PASS + big - 1, 0) // big

    def big_body(s, carry):
        run_rows(row0 + s * (big * R), big * R)
        return carry

    lax.fori_loop(0, n_big, big_body, 0)
    rem = n_sub - n_big * big
    done = row0 + n_big * (big * R)
    for k in range(1, MOE_MAX_PASS + 1):
        @pl.when(rem == k)
        def _(k=k):
            run_rows(done, k * R)

    @pl.when((g == MOE_GROUPS - 1) & (part == n_parts - 1))
    def _():
        out = h_ref[...] + _dot(pt_ref[...], oacc_ref[...].astype(BF16))
        if final_norm:
            out = _rms(out, fnw_ref[...])
        o_ref[...] = out


def _experts(t, cw, h2, w13, w2, fnw, tm, final_norm):
    T, D = h2.shape
    F = MOE_D_FF
    E = MOE_EXPERTS_PER_STEP
    n_parts = MOE_EPG // E
    cap = tm + MOE_GROUPS * MOE_SUB
    return pl.pallas_call(
        functools.partial(_experts_kernel, final_norm=final_norm),
        grid=(T // tm, MOE_GROUPS, n_parts),
        in_specs=[
            pl.BlockSpec((tm, D), lambda i, g, p: (i, 0)),
            pl.BlockSpec((tm, ROUTER_LANES), lambda i, g, p: (i, 0)),
            pl.BlockSpec((tm, D), lambda i, g, p: (i, 0)),
            pl.BlockSpec((E, D, 2 * F), lambda i, g, p: (g * n_parts + p, 0, 0)),
            pl.BlockSpec((E, F, D), lambda i, g, p: (g * n_parts + p, 0, 0)),
            pl.BlockSpec((1, D), lambda i, g, p: (0, 0)),
        ],
        out_specs=pl.BlockSpec((tm, D), lambda i, g, p: (i, 0)),
        out_shape=jax.ShapeDtypeStruct((T, D), F32),
        scratch_shapes=[
            pltpu.VMEM((tm, cap), BF16),
            pltpu.VMEM((cap, D), BF16),
            pltpu.VMEM((cap, ROUTER_LANES), F32),
            pltpu.VMEM((cap, D), F32 if n_parts > 1 else BF16),
            pltpu.VMEM((8, ROUTER_LANES), F32),
        ],
        compiler_params=_cparams(("parallel", "arbitrary", "arbitrary"), MOE_VMEM_LIMIT_BYTES),
    )(t, cw, h2, w13, w2, fnw)


def _moe_experts(t, cw, h2, w1, w3, w2, fnw, final_norm, tm):
    w13 = jnp.concatenate([w1, w3], axis=-1).astype(BF16)
    return _experts(t, cw, h2, w13, w2.astype(BF16), fnw.reshape(1, D_MODEL), tm, final_norm)


def _pick(n, prefs):
    for p in prefs:
        if n % p == 0:
            return p
    return n


def _ssd_retention_layer(h2, B, S, nw, w_in, conv_w, conv_b, dt_bias, a_log, d_skip, norm_w, w_out, router):
    D = D_MODEL
    T = B * S
    qk = RET_HEADS * RET_QK_HEAD
    o_x = D
    o_bc = o_x + D
    o_dt = o_bc + 2 * SSD_GROUPS * SSD_STATE
    o_q = o_dt + SSD_HEADS
    o_k = o_q + qk
    o_v = o_k + qk
    o_g = o_v + D
    w_main = jnp.concatenate([w_in[:, 0:o_x], w_in[:, o_x:o_bc], w_in[:, o_v:o_g], w_in[:, o_g:o_g + D],
                              w_in[:, o_bc:o_dt], w_in[:, o_q:o_k], w_in[:, o_k:o_v]], axis=1).astype(BF16)
    w_dt = jnp.pad(w_in[:, o_dt:o_q], ((0, 0), (0, LANES - SSD_HEADS))).astype(BF16)
    tm = _pick(T, (1024, 512, 256, 128))
    proj, dt = _inproj(h2, nw.reshape(1, D), w_main, w_dt, _pick(T, (2048, 1024, 512, 256, 128)), INPROJ_TN)
    proj3 = proj.reshape(B, S, proj.shape[1])
    dt3 = dt.reshape(B, S, LANES)

    head_of_channel = jnp.arange(D) // SSD_HEAD_DIM
    expand = (jnp.arange(LANES)[:, None] == head_of_channel[None, :]).astype(BF16)
    pad16 = lambda x: jnp.pad(x, (0, LANES - SSD_HEADS)).reshape(1, LANES)
    ssd_params = (conv_w[:, :D], conv_b[:D].reshape(1, D), conv_w[:, D:], conv_b[D:].reshape(1, PROJ_NARROW),
                  pad16(dt_bias), pad16(a_log), expand, jnp.repeat(d_skip, SSD_HEAD_DIM).reshape(1, D),
                  norm_w.reshape(1, D))

    C = RET_CHUNK
    half = RET_QK_HEAD // 2
    pos = jnp.arange(S, dtype=F32)
    inv_freq = ROPE_BASE ** (-jnp.arange(half, dtype=F32) / half)
    ang = pos[:, None] * inv_freq[None, :]
    cc = jnp.concatenate([jnp.cos(ang), jnp.cos(ang)], axis=1)
    ss = jnp.concatenate([-jnp.sin(ang), jnp.sin(ang)], axis=1)
    log_gamma = jnp.log(1.0 - 2.0 ** (-5.0 - jnp.arange(RET_HEADS, dtype=F32)))
    idx = jnp.arange(C, dtype=F32)
    diff = idx[:, None] - idx[None, :]
    dmask = jnp.where(diff[None] >= 0, jnp.exp(jnp.maximum(diff, 0.0)[None] * log_gamma[:, None, None]), 0.0)
    qdec = jnp.repeat(jnp.exp((idx[:, None] + 1.0) * log_gamma[None, :]), RET_V_HEAD, axis=1)
    kdec = jnp.repeat(jnp.exp((C - 1.0 - idx[:, None]) * log_gamma[None, :]), RET_QK_HEAD, axis=1)
    cdec = jnp.repeat(jnp.exp(C * log_gamma), RET_V_HEAD).reshape(1, D)
    y = _mix0(proj3, dt3, ssd_params, (cc, ss, dmask, qdec, kdec, cdec))
    return _outproj(y.reshape(T, 2 * D), w_out.astype(BF16), h2, router, tm)


def _rwkv_layer(h2, B, S, nw, mu, w_r, w_k, w_v, w_o, w0, w1, w2, a0, a1, a2, g1, g2, k_k, k_a, r_k, lnx_w, lnx_b,
                router):
    D = D_MODEL
    T = B * S
    padc = lambda w, n: jnp.pad(w, ((0, 0), (0, n - w.shape[1]))).astype(BF16)
    padr = lambda w, n: jnp.pad(w, ((0, n - w.shape[0]), (0, 0))).astype(BF16)
    head_of_channel = jnp.arange(D) // RWKV_HEAD
    hs = (head_of_channel[:, None] == jnp.arange(LANES)[None, :]).astype(BF16)
    row = lambda x: x.reshape(1, D)
    tm = _pick(S, (512, 256, 128))
    r, k, v, kk, a, lw, g = _rwkv_pre(
        h2, row(nw), mu, w_r.astype(BF16), w_k.astype(BF16), w_v.astype(BF16),
        padc(w1, RWKV_LORA_PAD), padr(w2, RWKV_LORA_PAD), row(w0),
        padc(a1, RWKV_LORA_PAD), padr(a2, RWKV_LORA_PAD), row(a0),
        padc(g1, RWKV_GATE_PAD), padr(g2, RWKV_GATE_PAD), row(k_k), row(k_a), hs, hs.T, S, tm)
    sh = lambda x: x.reshape(B, S, D)
    tb = _pick(S, (512, 256, 128, 64))
    y = _rwkv_scan(sh(r), sh(k), sh(v), sh(kk), sh(a), sh(lw), sh(g),
                   row(lnx_w), row(lnx_b), r_k.reshape(1, D), tb, RWKV_HEADS // RWKV_QUAD,
                   _pick(B, (RWKV_SEQS_PER_STEP, 1)))
    return _rwkv_out(y.reshape(T, D), w_o.astype(BF16), h2, router, _pick(T, (1024, 512, 256, 128)))


def kernel(x, norm_mix_w, norm_ffn_w, norm_final_w, w_in_e, ssd_conv_w, ssd_conv_b, ssd_dt_bias, ssd_a_log, ssd_d, ssd_norm_w, w_out_e, rw_mu, rw_wr, rw_wk, rw_wv, rw_wo, rw_w0, rw_w1, rw_w2, rw_a0, rw_a1, rw_a2, rw_g1, rw_g2, rw_kk, rw_ka, rw_rk, rw_lnx_w, rw_lnx_b, moe_wg, moe_bg, moe_we, moe_be, moe_w1, moe_w3, moe_w2):
    B, S, D = x.shape
    T = B * S
    depth = norm_mix_w.shape[0]
    h = x.reshape(T, D)
    tm_moe = _pick(T, (1024, 512, 256, 128))
    for layer in range(depth):
        i = layer // 2
        router = _router_params(norm_ffn_w[layer], moe_wg[layer], moe_bg[layer], moe_we[layer], moe_be[layer])
        if layer % 2 == 0:
            h, t, cw = _ssd_retention_layer(h, B, S, norm_mix_w[layer], w_in_e[i], ssd_conv_w[i], ssd_conv_b[i],
                                            ssd_dt_bias[i], ssd_a_log[i], ssd_d[i], ssd_norm_w[i], w_out_e[i],
                                            router)
        else:
            h, t, cw = _rwkv_layer(h, B, S, norm_mix_w[layer], rw_mu[i], rw_wr[i], rw_wk[i], rw_wv[i], rw_wo[i],
                                   rw_w0[i], rw_w1[i], rw_w2[i], rw_a0[i], rw_a1[i], rw_a2[i], rw_g1[i],
                                   rw_g2[i], rw_kk[i], rw_ka[i], rw_rk[i], rw_lnx_w[i], rw_lnx_b[i], router)
        h = _moe_experts(t, cw, h, moe_w1[layer], moe_w3[layer], moe_w2[layer], norm_final_w,
                         final_norm=(layer == depth - 1), tm=tm_moe)
    return h.reshape(B, S, D)
```
